```python
import jax, jax.numpy as jnp
from jax import lax
import numpy as np

D_MODEL = 1024
BATCH = 1
SEQ = 16384
DEPTH = 1

HEAD_DIM = 64
SB_HEADS = 8
DSA_GROUPS = ((128, 1), (512, 4), (2048, 16))
DSA_HEADS_PER_GROUP = 4
DSA_HEADS = DSA_HEADS_PER_GROUP * len(DSA_GROUPS)
MEM_HEADS = 4
MEM_LEN = 256
D_FF = 2816
ROPE_THETA = 10000.0
NORM_EPS = 1e-6
Q_BLOCK = 128
N_BRANCH = 3
SB_W = SB_HEADS * HEAD_DIM
DSA_W = DSA_HEADS * HEAD_DIM
DSA_OUT_W = DSA_HEADS_PER_GROUP * HEAD_DIM
MEM_W = MEM_HEADS * HEAD_DIM
IN_COLS = 3 * SB_W + 3 * DSA_W + MEM_W
MAX_DIL = max(r for _, r in DSA_GROUPS)

kernel_name = "hybrid_stickbreak_dilated_memory_block"

F32 = jnp.float32


def rms_norm(x, g):
    xf = x.astype(F32)
    y = xf * lax.rsqrt(jnp.mean(xf * xf, axis=-1, keepdims=True) + NORM_EPS)
    return (y * g.astype(F32)).astype(x.dtype)


def swiglu(x, w1, w3, w2):
    return (jax.nn.silu(x @ w1) * (x @ w3)) @ w2


def split_heads(t, n):
    b, s, _ = t.shape
    return t.reshape(b, s, n, HEAD_DIM).transpose(0, 2, 1, 3)


def merge_heads(t):
    b, n, s, hd = t.shape
    return t.transpose(0, 2, 1, 3).reshape(b, s, n * hd)


def rope(x, positions):
    half = HEAD_DIM // 2
    inv_freq = jnp.power(ROPE_THETA, -jnp.arange(half, dtype=F32) / half)
    ang = positions.astype(F32)[:, None] * inv_freq[None, :]
    cos, sin = jnp.cos(ang), jnp.sin(ang)
    xf = x.astype(F32)
    x1, x2 = xf[..., :half], xf[..., half:]
    return jnp.concatenate([x1 * cos - x2 * sin, x2 * cos + x1 * sin], axis=-1).astype(x.dtype)


def stick_breaking_attention(q, k, v):
    b, h, s, hd = q.shape
    nb = s // Q_BLOCK
    scale = hd ** -0.5
    qb = q.reshape(b, h, nb, Q_BLOCK, hd).transpose(2, 0, 1, 3, 4)
    key_pos = jnp.arange(s)
    vf = v.astype(F32)

    def block(args):
        qi, bi = args
        z = jnp.einsum('bhqd,bhkd->bhqk', qi, k).astype(F32) * scale
        q_pos = bi * Q_BLOCK + jnp.arange(Q_BLOCK)
        before = key_pos[None, :] < q_pos[:, None]
        log_fail = jnp.where(before, jax.nn.log_sigmoid(-z), 0.0)
        later = lax.cumsum(log_fail, axis=3, reverse=True) - log_fail
        w = jnp.where(before, jnp.exp(jax.nn.log_sigmoid(z) + later), 0.0)
        return jnp.einsum('bhqk,bhkd->bhqd', w, vf)

    out = lax.map(block, (qb, jnp.arange(nb)))
    return out.transpose(1, 2, 0, 3, 4).reshape(b, h, s, hd).astype(q.dtype)


def banded_window_attention(q, k, v, n_back):
    *lead, n, hd = q.shape
    nb = n // Q_BLOCK
    scale = hd ** -0.5
    qb = q.reshape(*lead, nb, Q_BLOCK, hd)

    def with_prev(t):
        tp = jnp.concatenate([jnp.zeros_like(t[..., :Q_BLOCK, :]), t], axis=-2)
        tp = tp.reshape(*lead, nb + 1, Q_BLOCK, hd)
        return jnp.concatenate([tp[..., :-1, :, :], tp[..., 1:, :, :]], axis=-2)

    kb, vb = with_prev(k), with_prev(v)
    sc = jnp.einsum('...qd,...kd->...qk', qb, kb).astype(F32) * scale
    qi = jnp.arange(Q_BLOCK)[:, None]
    kj = jnp.arange(2 * Q_BLOCK)[None, :]
    dist = Q_BLOCK + qi - kj
    blk = jnp.arange(nb)[:, None, None]
    valid = (dist >= 0) & (dist <= n_back) & ((blk > 0) | (kj >= Q_BLOCK))
    sc = jnp.where(valid, sc, -jnp.inf)
    m = jnp.max(sc, axis=-1, keepdims=True)
    p = jnp.exp(sc - m)
    den = jnp.sum(p, axis=-1, keepdims=True)
    out = jnp.einsum('...qk,...kd->...qd', p, vb.astype(F32)) / den
    lse = (m + jnp.log(den))[..., 0]
    return out.reshape(*lead, n, hd), lse.reshape(*lead, n)


def dilated_mixture_attention(q, k, v):
    b, _, s, hd = q.shape
    unit = Q_BLOCK * MAX_DIL
    sp = ((s + unit - 1) // unit) * unit
    pad = ((0, 0), (0, 0), (0, sp - s), (0, 0))
    q, k, v = jnp.pad(q, pad), jnp.pad(k, pad), jnp.pad(v, pad)
    outs, lses = [], []
    for g, (window, dil) in enumerate(DSA_GROUPS):
        sl = slice(g * DSA_HEADS_PER_GROUP, (g + 1) * DSA_HEADS_PER_GROUP)

        def stride_gather(t):
            return t[:, sl].reshape(b, DSA_HEADS_PER_GROUP, sp // dil, dil, hd).swapaxes(2, 3)

        o, l = banded_window_attention(stride_gather(q), stride_gather(k), stride_gather(v), window // dil)
        outs.append(o.swapaxes(2, 3).reshape(b, DSA_HEADS_PER_GROUP, sp, hd))
        lses.append(l.swapaxes(2, 3).reshape(b, DSA_HEADS_PER_GROUP, sp))
    alpha = jax.nn.softmax(jnp.stack(lses, axis=0), axis=0)
    o = jnp.sum(alpha[..., None] * jnp.stack(outs, axis=0), axis=0)
    return o[:, :, :s].astype(q.dtype)


def memory_cross_attention(q, mem_h, w_mem_kv, qn, kn):
    kv = mem_h @ w_mem_kv
    km, vm = jnp.split(kv, 2, axis=-1)
    km = rms_norm(split_heads(km, MEM_HEADS), kn)
    vm = split_heads(vm, MEM_HEADS)
    q = rms_norm(q, qn)
    sc = jnp.einsum('bhqd,bhkd->bhqk', q, km).astype(F32) * (HEAD_DIM ** -0.5)
    p = jax.nn.softmax(sc, axis=-1)
    return jnp.einsum('bhqk,bhkd->bhqd', p, vm.astype(F32)).astype(q.dtype)


def setup_inputs(seed: int = 0) -> dict:
    key = jax.random.key(seed)
    ks = iter(jax.random.split(key, 32))

    def w(shape, fan_in):
        return jax.random.normal(next(ks), (DEPTH,) + shape, F32) * (fan_in ** -0.5)

    def gain(shape):
        return 1.0 + 0.02 * jax.random.normal(next(ks), (DEPTH,) + shape, F32)

    return {
        "x": jax.random.normal(next(ks), (BATCH, SEQ, D_MODEL), F32),
        "mem": jax.random.normal(next(ks), (BATCH, MEM_LEN, D_MODEL), F32),
        "ffn1_norm": gain((D_MODEL,)),
        "ffn1_w1": w((D_MODEL, D_FF), D_MODEL),
        "ffn1_w3": w((D_MODEL, D_FF), D_MODEL),
        "ffn1_w2": w((D_FF, D_MODEL), D_FF),
        "mix_norm": gain((D_MODEL,)),
        "mem_norm": gain((D_MODEL,)),
        "w_in": w((D_MODEL, IN_COLS), D_MODEL),
        "w_mem_kv": w((D_MODEL, 2 * MEM_W), D_MODEL),
        "qn_dsa": gain((HEAD_DIM,)),
        "kn_dsa": gain((HEAD_DIM,)),
        "qn_mem": gain((HEAD_DIM,)),
        "kn_mem": gain((HEAD_DIM,)),
        "w_branch_sb": w((SB_W, D_MODEL), SB_W),
        "w_branch_dsa": w((DSA_OUT_W, D_MODEL), DSA_OUT_W),
        "w_branch_mem": w((MEM_W, D_MODEL), MEM_W),
        "w_gate": w((D_MODEL, N_BRANCH * D_MODEL), D_MODEL),
        "b_gate": 0.01 * jax.random.normal(next(ks), (DEPTH, N_BRANCH * D_MODEL), F32),
        "w_out": w((D_MODEL, D_MODEL), D_MODEL),
        "ffn2_norm": gain((D_MODEL,)),
        "ffn2_w1": w((D_MODEL, D_FF), D_MODEL),
        "ffn2_w3": w((D_MODEL, D_FF), D_MODEL),
        "ffn2_w2": w((D_FF, D_MODEL), D_FF),
    }


def reference(x, mem, ffn1_norm, ffn1_w1, ffn1_w3, ffn1_w2, mix_norm, mem_norm, w_in, w_mem_kv,
              qn_dsa, kn_dsa, qn_mem, kn_mem, w_branch_sb, w_branch_dsa, w_branch_mem,
              w_gate, b_gate, w_out, ffn2_norm, ffn2_w1, ffn2_w3, ffn2_w2):
    b, s, d = x.shape
    positions = jnp.arange(s)
    cuts = np.cumsum([SB_W, SB_W, SB_W, DSA_W, DSA_W, DSA_W])
    for l in range(DEPTH):
        x = x + 0.5 * swiglu(rms_norm(x, ffn1_norm[l]), ffn1_w1[l], ffn1_w3[l], ffn1_w2[l])

        h = rms_norm(x, mix_norm[l])
        qa, ka, va, qb, kb, vb, qc = jnp.split(h @ w_in[l], cuts, axis=-1)

        ya = stick_breaking_attention(split_heads(qa, SB_HEADS), split_heads(ka, SB_HEADS),
                                      split_heads(va, SB_HEADS))
        ya = merge_heads(ya) @ w_branch_sb[l]

        qb_h = rope(rms_norm(split_heads(qb, DSA_HEADS), qn_dsa[l]), positions)
        kb_h = rope(rms_norm(split_heads(kb, DSA_HEADS), kn_dsa[l]), positions)
        yb = dilated_mixture_attention(qb_h, kb_h, split_heads(vb, DSA_HEADS))
        yb = merge_heads(yb) @ w_branch_dsa[l]

        yc = memory_cross_attention(split_heads(qc, MEM_HEADS), rms_norm(mem, mem_norm[l]),
                                    w_mem_kv[l], qn_mem[l], kn_mem[l])
        yc = merge_heads(yc) @ w_branch_mem[l]

        gates = jax.nn.sigmoid(h @ w_gate[l] + b_gate[l]).reshape(b, s, N_BRANCH, d)
        merged = gates[:, :, 0] * ya + gates[:, :, 1] * yb + gates[:, :, 2] * yc
        x = x + merged @ w_out[l]

        x = x + 0.5 * swiglu(rms_norm(x, ffn2_norm[l]), ffn2_w1[l], ffn2_w3[l], ffn2_w2[l])
    return x
```

```python
import functools

import jax
import jax.numpy as jnp
from jax import lax
from jax.experimental import pallas as pl
from jax.experimental.pallas import tpu as pltpu

F32 = jnp.float32
BF16 = jnp.bfloat16

HEAD_DIM = 64
SB_HEADS = 8
DSA_GROUPS = ((128, 1), (512, 4), (2048, 16))
DSA_HEADS_PER_GROUP = 4
MEM_HEADS = 4
ROPE_THETA = 10000.0
NORM_EPS = 1e-6
Q_BLOCK = 128
SB_W = SB_HEADS * HEAD_DIM
DSA_W = DSA_HEADS_PER_GROUP * len(DSA_GROUPS) * HEAD_DIM
DSA_OUT_W = DSA_HEADS_PER_GROUP * HEAD_DIM
MEM_W = MEM_HEADS * HEAD_DIM
QK_SCALE = HEAD_DIM ** -0.5

LANES = 128
DSA_UNIT = Q_BLOCK * max(r for _, r in DSA_GROUPS)
ROW_TILE = 512
VMEM_LIMIT = 56 * 1024 * 1024
SB_LOG_CUTOFF = -104.0


def _resident(shape):
    zeros = (0,) * len(shape)
    return pl.BlockSpec(shape, lambda *_: zeros, pipeline_mode=pl.Buffered(1))


def _params(n_axes):
    return pltpu.CompilerParams(dimension_semantics=("arbitrary",) * n_axes,
                                vmem_limit_bytes=VMEM_LIMIT)


def _rms(x, g):
    return x * lax.rsqrt(jnp.mean(x * x, axis=-1, keepdims=True) + NORM_EPS) * g


def _dot(a, b):
    return jnp.dot(a, b, preferred_element_type=F32)


def _dot_nt(a, b):
    return lax.dot_general(a, b, (((1,), (1,)), ((), ())), preferred_element_type=F32)


def _split_bf16(x):
    hi = x.astype(BF16)
    lo = (x - hi.astype(F32)).astype(BF16)
    return hi, lo


def _head_norm(x, g):
    n = x.shape[-1]
    r = lax.broadcasted_iota(jnp.int32, (n, n), 0) // HEAD_DIM
    c = lax.broadcasted_iota(jnp.int32, (n, n), 1) // HEAD_DIM
    bd = jnp.where(r == c, 1.0, 0.0).astype(BF16)
    hi, lo = _split_bf16(x * x)
    ms = (_dot(hi, bd) + _dot(lo, bd)) * (1.0 / HEAD_DIM)
    return x * lax.rsqrt(ms + NORM_EPS) * g


def _ffn_kernel(x_ref, g_ref, w1_ref, w3_ref, w2_ref, o_ref):
    x = x_ref[...]
    h = _rms(x, g_ref[...]).astype(BF16)
    a = _dot(h, w1_ref[...])
    b = _dot(h, w3_ref[...])
    act = (a * b / (1.0 + jnp.exp(-a))).astype(BF16)
    o_ref[...] = x + 0.5 * _dot(act, w2_ref[...])


def _ffn(x, g, w1, w3, w2):
    s, d = x.shape
    f = w1.shape[1]
    row = lambda i: (i, 0)
    return pl.pallas_call(
        _ffn_kernel,
        out_shape=jax.ShapeDtypeStruct((s, d), F32),
        grid=(s // ROW_TILE,),
        in_specs=[pl.BlockSpec((ROW_TILE, d), row), _resident((1, d)),
                  _resident((d, f)), _resident((d, f)), _resident((f, d))],
        out_specs=pl.BlockSpec((ROW_TILE, d), row),
        compiler_params=_params(1),
        name="ffn",
    )(x, g, w1, w3, w2)


def _memkv_kernel(mem_ref, g_ref, w_ref, kn_ref, k_ref, v_ref):
    h = _rms(mem_ref[...], g_ref[...]).astype(BF16)
    kv = _dot(h, w_ref[...])
    k_ref[...] = _head_norm(kv[:, :MEM_W], kn_ref[...]).astype(BF16)
    v_ref[...] = kv[:, MEM_W:].astype(BF16)


def _memkv(mem, g, w, kn):
    m = mem.shape[0]
    out = jax.ShapeDtypeStruct((m, MEM_W), BF16)
    return pl.pallas_call(_memkv_kernel, out_shape=(out, out), name="memkv",
                          compiler_params=pltpu.CompilerParams(vmem_limit_bytes=VMEM_LIMIT),
                          )(mem, g, w, kn)


def _rope(x, cos, sin_signed):
    lane = lax.broadcasted_iota(jnp.int32, (x.shape[0], LANES), 1)
    first_half = (lane % HEAD_DIM) < (HEAD_DIM // 2)
    out = []
    for j in range(x.shape[1] // LANES):
        xs = x[:, j * LANES:(j + 1) * LANES]
        partner = jnp.where(first_half, pltpu.roll(xs, LANES - HEAD_DIM // 2, 1),
                            pltpu.roll(xs, HEAD_DIM // 2, 1))
        out.append(xs * cos + partner * sin_signed)
    return jnp.concatenate(out, axis=1)


def _proj_kernel(x_ref, g_ref, w_ref, cos_ref, sin_ref, qn_d_ref, kn_d_ref, qn_m_ref, km_ref, vm_ref,
                 qa_ref, ka_ref, va_ref, qb_ref, kb_ref, vb_ref, yc_ref):
    h = _rms(x_ref[...], g_ref[...]).astype(BF16)
    cos, sin_signed = cos_ref[...], sin_ref[...]

    def cols(lo, width):
        return _dot(h, w_ref[:, lo:lo + width])

    qa_ref[...] = (cols(0, SB_W) * QK_SCALE).astype(BF16)
    ka_ref[...] = cols(SB_W, SB_W).astype(BF16)
    va_ref[...] = cols(2 * SB_W, SB_W).astype(BF16)
    base = 3 * SB_W
    qb_ref[...] = _rope(_head_norm(cols(base, DSA_W), qn_d_ref[...]), cos, sin_signed) * QK_SCALE
    kb_ref[...] = _rope(_head_norm(cols(base + DSA_W, DSA_W), kn_d_ref[...]), cos, sin_signed)
    vb_ref[...] = cols(base + 2 * DSA_W, DSA_W)

    qc = _head_norm(cols(base + 3 * DSA_W, MEM_W), qn_m_ref[...]) * QK_SCALE
    lane = lax.broadcasted_iota(jnp.int32, (qc.shape[0], LANES), 1)
    for j in range(MEM_W // LANES):
        sl = slice(j * LANES, (j + 1) * LANES)
        qs, km, vm = qc[:, sl], km_ref[:, sl], vm_ref[:, sl]
        outs = []
        for half in range(2):
            in_head = (lane < HEAD_DIM) == (half == 0)
            sc = _dot_nt(jnp.where(in_head, qs, 0.0).astype(BF16), km)
            p = jnp.exp(sc - jnp.max(sc, axis=-1, keepdims=True))
            outs.append(_dot(p.astype(BF16), vm) / jnp.sum(p, axis=-1, keepdims=True))
        yc_ref[:, sl] = jnp.where(lane < HEAD_DIM, outs[0], outs[1]).astype(BF16)


def _proj(x, g, w_in, cos, sin_signed, qn_d, kn_d, qn_m, km, vm):
    s, d = x.shape
    row = lambda i: (i, 0)
    tile = lambda w: pl.BlockSpec((ROW_TILE, w), row)
    sb = jax.ShapeDtypeStruct((s, SB_W), BF16)
    dsa = jax.ShapeDtypeStruct((s, DSA_W), F32)
    return pl.pallas_call(
        _proj_kernel,
        out_shape=(sb, sb, sb, dsa, dsa, dsa, jax.ShapeDtypeStruct((s, MEM_W), BF16)),
        grid=(s // ROW_TILE,),
        in_specs=[tile(d), _resident((1, d)), _resident(w_in.shape), tile(LANES), tile(LANES),
                  _resident((1, DSA_W)), _resident((1, DSA_W)), _resident((1, MEM_W)),
                  _resident(km.shape), _resident(vm.shape)],
        out_specs=(tile(SB_W), tile(SB_W), tile(SB_W), tile(DSA_W), tile(DSA_W), tile(DSA_W),
                   tile(MEM_W)),
        compiler_params=_params(1),
        name="proj",
    )(x, g, w_in, cos, sin_signed, qn_d, kn_d, qn_m, km, vm)


def _sb_kernel(q_ref, k_ref, v_ref, o_ref):
    qb = pl.program_id(0)
    lane = lax.broadcasted_iota(jnp.int32, (Q_BLOCK, LANES), 1)
    row = lax.broadcasted_iota(jnp.int32, (Q_BLOCK, LANES), 0)
    before = lane < row
    j = lax.broadcasted_iota(jnp.int32, (2 * Q_BLOCK, 2 * Q_BLOCK), 0) % Q_BLOCK
    s = lax.broadcasted_iota(jnp.int32, (2 * Q_BLOCK, 2 * Q_BLOCK), 1)
    suffix = jnp.where((s >= Q_BLOCK) | (j > s), 1.0, 0.0).astype(BF16)

    for pair in range(SB_W // LANES):
        sl = slice(pair * LANES, (pair + 1) * LANES)
        qs = q_ref[:, sl]
        outs = []
        for half in range(2):
            in_head = (lane < HEAD_DIM) == (half == 0)
            qh = jnp.where(in_head, qs, jnp.zeros_like(qs))

            def block(kb, carry, acc, diagonal):
                rows = pl.ds(pl.multiple_of(kb * Q_BLOCK, Q_BLOCK), Q_BLOCK)
                z = _dot_nt(qh, k_ref[rows, sl])
                lf = -(jnp.maximum(z, 0.0) + jnp.log(1.0 + jnp.exp(-jnp.abs(z))))
                if diagonal:
                    lf = jnp.where(before, lf, 0.0)
                hi, lo = _split_bf16(lf)
                sums = _dot(jnp.concatenate([hi, lo], axis=1), suffix)
                w = jnp.exp(z + lf + carry + sums[:, :Q_BLOCK])
                if diagonal:
                    w = jnp.where(before, w, 0.0)
                acc = acc + _dot(w.astype(BF16), v_ref[rows, sl])
                return carry + sums[:, Q_BLOCK:], acc

            zeros = jnp.zeros((Q_BLOCK, LANES), F32)
            carry, acc = block(qb, zeros, zeros, True)

            def cond(state):
                kb, carry, _ = state
                return jnp.logical_and(kb >= 0, jnp.max(carry) > SB_LOG_CUTOFF)

            def body(state):
                kb, carry, acc = state
                carry, acc = block(kb, carry, acc, False)
                return kb - 1, carry, acc

            outs.append(lax.while_loop(cond, body, (qb - 1, carry, acc))[2])
        o_ref[:, sl] = jnp.where(lane < HEAD_DIM, outs[0], outs[1]).astype(BF16)


def _stickbreak(q, k, v):
    s, w = q.shape
    return pl.pallas_call(
        _sb_kernel,
        out_shape=jax.ShapeDtypeStruct((s, w), BF16),
        grid=(s // Q_BLOCK,),
        in_specs=[pl.BlockSpec((Q_BLOCK, w), lambda i: (i, 0)), _resident((s, w)), _resident((s, w))],
        out_specs=pl.BlockSpec((Q_BLOCK, w), lambda i: (i, 0)),
        compiler_params=_params(1),
        name="stickbreak",
    )(q, k, v)


def _dsa_kernel(*refs):
    n_g = len(DSA_GROUPS)
    ins = [refs[5 * g:5 * g + 5] for g in range(n_g)]
    o_ref = refs[5 * n_g]
    scratch = refs[5 * n_g + 1:]
    step = pl.program_id(0)
    lane = lax.broadcasted_iota(jnp.int32, (Q_BLOCK, LANES), 1)
    qi = lax.broadcasted_iota(jnp.int32, (Q_BLOCK, 2 * Q_BLOCK), 0)
    kj = lax.broadcasted_iota(jnp.int32, (Q_BLOCK, 2 * Q_BLOCK), 1)
    dist = Q_BLOCK + qi - kj

    for g, (window, r) in enumerate(DSA_GROUPS):
        q_ref, k_ref, kp_ref, v_ref, vp_ref = ins[g]
        kf_ref, vf_ref, og_ref, lg_ref = scratch[4 * g:4 * g + 4]
        prev = Q_BLOCK * r
        kf_ref[:prev, :] = kp_ref[...]
        kf_ref[prev:, :] = k_ref[...]
        vf_ref[:prev, :] = vp_ref[...]
        vf_ref[prev:, :] = v_ref[...]
        in_band = (dist >= 0) & (dist <= window // r)
        n_sub = DSA_UNIT // prev

        def unit(u, _, r=r, prev=prev, n_sub=n_sub, q_ref=q_ref, kf_ref=kf_ref, vf_ref=vf_ref,
                 og_ref=og_ref, lg_ref=lg_ref, in_band=in_band):
            sub, c = u // r, u % r
            base = sub * prev + c
            q = q_ref[pl.ds(base, Q_BLOCK, stride=r), :]
            k = kf_ref[pl.ds(base, 2 * Q_BLOCK, stride=r), :].astype(BF16)
            v = vf_ref[pl.ds(base, 2 * Q_BLOCK, stride=r), :].astype(BF16)
            first_key = jnp.where(step * n_sub + sub > 0, 0, Q_BLOCK)
            valid = in_band & (kj >= first_key)
            outs, lses = [], []
            for half in range(2):
                in_head = (lane < HEAD_DIM) == (half == 0)
                sc = _dot_nt(jnp.where(in_head, q, 0.0).astype(BF16), k)
                sc = jnp.where(valid, sc, -jnp.inf)
                m = jnp.max(sc, axis=-1, keepdims=True)
                p = jnp.exp(sc - m)
                den = jnp.sum(p, axis=-1, keepdims=True)
                outs.append(_dot(p.astype(BF16), v) / den)
                lses.append(m + jnp.log(den))
            rows = pl.ds(base, Q_BLOCK, stride=r)
            og_ref[rows, :] = jnp.where(lane < HEAD_DIM, outs[0], outs[1])
            lg_ref[rows, :] = jnp.where(lane < HEAD_DIM, lses[0], lses[1])
            return 0

        lax.fori_loop(0, DSA_UNIT // Q_BLOCK, unit, 0)

    lse = [scratch[4 * g + 3][...] for g in range(n_g)]
    top = functools.reduce(jnp.maximum, lse)
    e = [jnp.exp(l - top) for l in lse]
    num = sum(e[g] * scratch[4 * g + 2][...] for g in range(n_g))
    o_ref[...] = (num / sum(e)).astype(o_ref.dtype)


def _dilated(q, k, v):
    s = q.shape[0]
    n_pairs = DSA_OUT_W // LANES
    in_specs, scratch = [], []
    for g, (_, r) in enumerate(DSA_GROUPS):
        prev = Q_BLOCK * r
        per_unit = DSA_UNIT // prev
        cur = pl.BlockSpec((DSA_UNIT, LANES), lambda i, p, g=g: (i, n_pairs * g + p))
        prv = pl.BlockSpec((prev, LANES),
                           lambda i, p, g=g, n=per_unit: (jnp.maximum(i * n - 1, 0), n_pairs * g + p))
        in_specs += [cur, cur, prv, cur, prv]
        scratch += [pltpu.VMEM((prev + DSA_UNIT, LANES), F32), pltpu.VMEM((prev + DSA_UNIT, LANES), F32),
                    pltpu.VMEM((DSA_UNIT, LANES), F32), pltpu.VMEM((DSA_UNIT, LANES), F32)]
    args = []
    for _ in DSA_GROUPS:
        args += [q, k, k, v, v]
    return pl.pallas_call(
        _dsa_kernel,
        out_shape=jax.ShapeDtypeStruct((s, DSA_OUT_W), BF16),
        grid=(s // DSA_UNIT, n_pairs),
        in_specs=in_specs,
        out_specs=pl.BlockSpec((DSA_UNIT, LANES), lambda i, p: (i, p)),
        scratch_shapes=scratch,
        compiler_params=_params(2),
        name="dilated",
    )(*args)


def _merge_kernel(x_ref, g_ref, wg_ref, bg_ref, ya_ref, yb_ref, yc_ref, wa_ref, wb_ref, wc_ref,
                  wo_ref, o_ref):
    x = x_ref[...]
    d = x.shape[1]
    h = _rms(x, g_ref[...]).astype(BF16)
    merged = None
    for i, (y_ref, w_ref) in enumerate(((ya_ref, wa_ref), (yb_ref, wb_ref), (yc_ref, wc_ref))):
        logits = _dot(h, wg_ref[:, i * d:(i + 1) * d]) + bg_ref[:, i * d:(i + 1) * d]
        term = _dot(y_ref[...], w_ref[...]) / (1.0 + jnp.exp(-logits))
        merged = term if merged is None else merged + term
    o_ref[...] = x + _dot(merged.astype(BF16), wo_ref[...])


def _merge(x, g, wg, bg, ya, yb, yc, wa, wb, wc, wo):
    s, d = x.shape
    row = lambda i: (i, 0)
    tile = lambda w: pl.BlockSpec((ROW_TILE, w), row)
    return pl.pallas_call(
        _merge_kernel,
        out_shape=jax.ShapeDtypeStruct((s, d), F32),
        grid=(s // ROW_TILE,),
        in_specs=[tile(d), _resident((1, d)), _resident(wg.shape), _resident(bg.shape),
                  tile(SB_W), tile(DSA_OUT_W), tile(MEM_W),
                  _resident(wa.shape), _resident(wb.shape), _resident(wc.shape), _resident(wo.shape)],
        out_specs=tile(d),
        compiler_params=_params(1),
        name="merge",
    )(x, g, wg, bg, ya, yb, yc, wa, wb, wc, wo)


def _rope_tables(s):
    half = HEAD_DIM // 2
    inv_freq = jnp.power(ROPE_THETA, -jnp.arange(half, dtype=F32) / half)
    ang = jnp.arange(s).astype(F32)[:, None] * inv_freq[None, :]
    cos, sin = jnp.cos(ang), jnp.sin(ang)
    reps = LANES // HEAD_DIM
    return (jnp.tile(jnp.concatenate([cos, cos], axis=1), (1, reps)),
            jnp.tile(jnp.concatenate([-sin, sin], axis=1), (1, reps)))


def _layer(x, mem, p):
    s = x.shape[0]
    bf = lambda w: w.astype(BF16)
    vec = lambda v: v.reshape(1, -1)
    heads = lambda v, n: jnp.tile(v, n).reshape(1, -1)
    cos, sin_signed = _rope_tables(s)

    x = _ffn(x, vec(p["ffn1_norm"]), bf(p["ffn1_w1"]), bf(p["ffn1_w3"]), bf(p["ffn1_w2"]))
    km, vm = _memkv(mem, vec(p["mem_norm"]), bf(p["w_mem_kv"]), heads(p["kn_mem"], MEM_HEADS))
    n_dsa = DSA_W // HEAD_DIM
    qa, ka, va, qb, kb, vb, yc = _proj(
        x, vec(p["mix_norm"]), bf(p["w_in"]), cos, sin_signed,
        heads(p["qn_dsa"], n_dsa), heads(p["kn_dsa"], n_dsa), heads(p["qn_mem"], MEM_HEADS), km, vm)
    ya = _stickbreak(qa, ka, va)
    yb = _dilated(qb, kb, vb)
    x = _merge(x, vec(p["mix_norm"]), bf(p["w_gate"]), vec(p["b_gate"]), ya, yb, yc,
               bf(p["w_branch_sb"]), bf(p["w_branch_dsa"]), bf(p["w_branch_mem"]), bf(p["w_out"]))
    return _ffn(x, vec(p["ffn2_norm"]), bf(p["ffn2_w1"]), bf(p["ffn2_w3"]), bf(p["ffn2_w2"]))


_PARAM_NAMES = ("ffn1_norm", "ffn1_w1", "ffn1_w3", "ffn1_w2", "mix_norm", "mem_norm", "w_in", "w_mem_kv",
                "qn_dsa", "kn_dsa", "qn_mem", "kn_mem", "w_branch_sb", "w_branch_dsa", "w_branch_mem",
                "w_gate", "b_gate", "w_out", "ffn2_norm", "ffn2_w1", "ffn2_w3", "ffn2_w2")


def kernel(x, mem, ffn1_norm, ffn1_w1, ffn1_w3, ffn1_w2, mix_norm, mem_norm, w_in, w_mem_kv, qn_dsa, kn_dsa, qn_mem, kn_mem, w_branch_sb, w_branch_dsa, w_branch_mem, w_gate, b_gate, w_out, ffn2_norm, ffn2_w1, ffn2_w3, ffn2_w2):
    stacked = dict(zip(_PARAM_NAMES, (ffn1_norm, ffn1_w1, ffn1_w3, ffn1_w2, mix_norm, mem_norm, w_in,
                                      w_mem_kv, qn_dsa, kn_dsa, qn_mem, kn_mem, w_branch_sb, w_branch_dsa,
                                      w_branch_mem, w_gate, b_gate, w_out, ffn2_norm, ffn2_w1, ffn2_w3,
                                      ffn2_w2)))
    depth = ffn1_norm.shape[0]
    outs = []
    for b in range(x.shape[0]):
        xb = x[b]
        for l in range(depth):
            xb = _layer(xb, mem[b], {k: v[l] for k, v in stacked.items()})
        outs.append(xb)
    return jnp.stack(outs)
```

```python
import functools

import jax
import jax.numpy as jnp
from jax import lax
from jax.experimental import pallas as pl
from jax.experimental.pallas import tpu as pltpu

F32 = jnp.float32
BF16 = jnp.bfloat16

HEAD_DIM = 64
SB_HEADS = 8
DSA_GROUPS = ((128, 1), (512, 4), (2048, 16))
DSA_HEADS_PER_GROUP = 4
MEM_HEADS = 4
ROPE_THETA = 10000.0
NORM_EPS = 1e-6
Q_BLOCK = 128
SB_W = SB_HEADS * HEAD_DIM
DSA_W = DSA_HEADS_PER_GROUP * len(DSA_GROUPS) * HEAD_DIM
DSA_OUT_W = DSA_HEADS_PER_GROUP * HEAD_DIM
MEM_W = MEM_HEADS * HEAD_DIM
QK_SCALE = HEAD_DIM ** -0.5

LANES = 128
DSA_UNIT = Q_BLOCK * max(r for _, r in DSA_GROUPS)
DSA_UNROLL = 4
ROW_TILE = 512
VMEM_LIMIT = 56 * 1024 * 1024
SB_LOG_CUTOFF = -104.0


def _resident(shape):
    zeros = (0,) * len(shape)
    return pl.BlockSpec(shape, lambda *_: zeros, pipeline_mode=pl.Buffered(1))


def _params(n_axes):
    return pltpu.CompilerParams(dimension_semantics=("arbitrary",) * n_axes,
                                vmem_limit_bytes=VMEM_LIMIT)


def _rms(x, g):
    return x * lax.rsqrt(jnp.mean(x * x, axis=-1, keepdims=True) + NORM_EPS) * g


def _dot(a, b):
    return jnp.dot(a, b, preferred_element_type=F32)


def _dot_nt(a, b):
    return lax.dot_general(a, b, (((1,), (1,)), ((), ())), preferred_element_type=F32)


def _split_bf16(x):
    hi = x.astype(BF16)
    lo = (x - hi.astype(F32)).astype(BF16)
    return hi, lo


def _head_norm(x, g):
    n = x.shape[-1]
    r = lax.broadcasted_iota(jnp.int32, (n, n), 0) // HEAD_DIM
    c = lax.broadcasted_iota(jnp.int32, (n, n), 1) // HEAD_DIM
    bd = jnp.where(r == c, 1.0, 0.0).astype(BF16)
    hi, lo = _split_bf16(x * x)
    ms = (_dot(hi, bd) + _dot(lo, bd)) * (1.0 / HEAD_DIM)
    return x * lax.rsqrt(ms + NORM_EPS) * g


def _ffn_kernel(x_ref, g_ref, w1_ref, w3_ref, w2_ref, o_ref):
    x = x_ref[...]
    h = _rms(x, g_ref[...]).astype(BF16)
    a = _dot(h, w1_ref[...])
    b = _dot(h, w3_ref[...])
    act = (a * b / (1.0 + jnp.exp(-a))).astype(BF16)
    o_ref[...] = x + 0.5 * _dot(act, w2_ref[...])


def _ffn(x, g, w1, w3, w2):
    s, d = x.shape
    f = w1.shape[1]
    row = lambda i: (i, 0)
    return pl.pallas_call(
        _ffn_kernel,
        out_shape=jax.ShapeDtypeStruct((s, d), F32),
        grid=(s // ROW_TILE,),
        in_specs=[pl.BlockSpec((ROW_TILE, d), row), _resident((1, d)),
                  _resident((d, f)), _resident((d, f)), _resident((f, d))],
        out_specs=pl.BlockSpec((ROW_TILE, d), row),
        compiler_params=_params(1),
        name="ffn",
    )(x, g, w1, w3, w2)


def _memkv_kernel(mem_ref, g_ref, w_ref, kn_ref, k_ref, v_ref):
    h = _rms(mem_ref[...], g_ref[...]).astype(BF16)
    kv = _dot(h, w_ref[...])
    k_ref[...] = _head_norm(kv[:, :MEM_W], kn_ref[...]).astype(BF16)
    v_ref[...] = kv[:, MEM_W:].astype(BF16)


def _memkv(mem, g, w, kn):
    m = mem.shape[0]
    out = jax.ShapeDtypeStruct((m, MEM_W), BF16)
    return pl.pallas_call(_memkv_kernel, out_shape=(out, out), name="memkv",
                          compiler_params=pltpu.CompilerParams(vmem_limit_bytes=VMEM_LIMIT),
                          )(mem, g, w, kn)


def _rope(x, cos, sin_signed):
    lane = lax.broadcasted_iota(jnp.int32, (x.shape[0], LANES), 1)
    first_half = (lane % HEAD_DIM) < (HEAD_DIM // 2)
    out = []
    for j in range(x.shape[1] // LANES):
        xs = x[:, j * LANES:(j + 1) * LANES]
        partner = jnp.where(first_half, pltpu.roll(xs, LANES - HEAD_DIM // 2, 1),
                            pltpu.roll(xs, HEAD_DIM // 2, 1))
        out.append(xs * cos + partner * sin_signed)
    return jnp.concatenate(out, axis=1)


def _proj_kernel(x_ref, g_ref, w_ref, cos_ref, sin_ref, qn_d_ref, kn_d_ref, qn_m_ref, km_ref, vm_ref,
                 qa_ref, ka_ref, va_ref, qb_ref, kb_ref, vb_ref, yc_ref):
    h = _rms(x_ref[...], g_ref[...]).astype(BF16)
    cos, sin_signed = cos_ref[...], sin_ref[...]

    def cols(lo, width):
        return _dot(h, w_ref[:, lo:lo + width])

    qa_ref[...] = (cols(0, SB_W) * QK_SCALE).astype(BF16)
    ka_ref[...] = cols(SB_W, SB_W).astype(BF16)
    va_ref[...] = cols(2 * SB_W, SB_W).astype(BF16)
    base = 3 * SB_W
    qb_ref[...] = _rope(_head_norm(cols(base, DSA_W), qn_d_ref[...]), cos, sin_signed) * QK_SCALE
    kb_ref[...] = _rope(_head_norm(cols(base + DSA_W, DSA_W), kn_d_ref[...]), cos, sin_signed)
    vb_ref[...] = cols(base + 2 * DSA_W, DSA_W)

    qc = _head_norm(cols(base + 3 * DSA_W, MEM_W), qn_m_ref[...]) * QK_SCALE
    lane = lax.broadcasted_iota(jnp.int32, (qc.shape[0], LANES), 1)
    for j in range(MEM_W // LANES):
        sl = slice(j * LANES, (j + 1) * LANES)
        qs, km, vm = qc[:, sl], km_ref[:, sl], vm_ref[:, sl]
        outs = []
        for half in range(2):
            in_head = (lane < HEAD_DIM) == (half == 0)
            sc = _dot_nt(jnp.where(in_head, qs, 0.0).astype(BF16), km)
            p = jnp.exp(sc - jnp.max(sc, axis=-1, keepdims=True))
            outs.append(_dot(p.astype(BF16), vm) / jnp.sum(p, axis=-1, keepdims=True))
        yc_ref[:, sl] = jnp.where(lane < HEAD_DIM, outs[0], outs[1]).astype(BF16)


def _proj(x, g, w_in, cos, sin_signed, qn_d, kn_d, qn_m, km, vm):
    s, d = x.shape
    row = lambda i: (i, 0)
    tile = lambda w: pl.BlockSpec((ROW_TILE, w), row)
    sb = jax.ShapeDtypeStruct((s, SB_W), BF16)
    dsa = jax.ShapeDtypeStruct((s, DSA_W), F32)
    return pl.pallas_call(
        _proj_kernel,
        out_shape=(sb, sb, sb, dsa, dsa, dsa, jax.ShapeDtypeStruct((s, MEM_W), BF16)),
        grid=(s // ROW_TILE,),
        in_specs=[tile(d), _resident((1, d)), _resident(w_in.shape), tile(LANES), tile(LANES),
                  _resident((1, DSA_W)), _resident((1, DSA_W)), _resident((1, MEM_W)),
                  _resident(km.shape), _resident(vm.shape)],
        out_specs=(tile(SB_W), tile(SB_W), tile(SB_W), tile(DSA_W), tile(DSA_W), tile(DSA_W),
                   tile(MEM_W)),
        compiler_params=_params(1),
        name="proj",
    )(x, g, w_in, cos, sin_signed, qn_d, kn_d, qn_m, km, vm)


def _sb_kernel(q_ref, k_ref, v_ref, o_ref, q2_ref, carry_ref, acc_ref):
    qb = pl.program_id(0)
    n_pairs = SB_W // LANES
    lane = lax.broadcasted_iota(jnp.int32, (2 * Q_BLOCK, LANES), 1)
    row = lax.broadcasted_iota(jnp.int32, (2 * Q_BLOCK, LANES), 0)
    own_lanes = (lane < HEAD_DIM) == (row < Q_BLOCK)
    before = lane < (row % Q_BLOCK)
    j = lax.broadcasted_iota(jnp.int32, (2 * Q_BLOCK, 2 * Q_BLOCK), 0) % Q_BLOCK
    s = lax.broadcasted_iota(jnp.int32, (2 * Q_BLOCK, 2 * Q_BLOCK), 1)
    suffix = jnp.where((s >= Q_BLOCK) | (j > s), 1.0, 0.0).astype(BF16)

    def stack(x):
        return jnp.where(own_lanes, jnp.concatenate([x, x], axis=0), jnp.zeros((), x.dtype))

    for pair in range(n_pairs):
        q2_ref[pair] = stack(q_ref[:, pair * LANES:(pair + 1) * LANES])

    def step(kb, diagonal):
        rows = pl.ds(pl.multiple_of(kb * Q_BLOCK, Q_BLOCK), Q_BLOCK)
        pairs = range(n_pairs)
        sls = [slice(p * LANES, (p + 1) * LANES) for p in pairs]
        z = [_dot_nt(q2_ref[p], k_ref[rows, sls[p]]) for p in pairs]
        lf, split = [], []
        for p in pairs:
            l = -(jnp.maximum(z[p], 0.0) + jnp.log(1.0 + jnp.exp(-jnp.abs(z[p]))))
            if diagonal:
                l = jnp.where(before, l, 0.0)
            lf.append(l)
            split.append(jnp.concatenate(_split_bf16(l), axis=1))
        sums = [_dot(split[p], suffix) for p in pairs]
        w2, top = [], None
        for p in pairs:
            logw = z[p] + lf[p] + sums[p][:, :Q_BLOCK]
            carry = sums[p][:, Q_BLOCK:]
            if not diagonal:
                logw = logw + carry_ref[p]
                carry = carry + carry_ref[p]
            w = jnp.exp(logw)
            if diagonal:
                w = jnp.where(before, w, 0.0)
            w = w.astype(BF16)
            w2.append(jnp.concatenate([w[:Q_BLOCK], w[Q_BLOCK:]], axis=1))
            carry_ref[p] = carry
            top = carry if top is None else jnp.maximum(top, carry)
        pv = [_dot(w2[p], stack(v_ref[rows, sls[p]])) for p in pairs]
        for p in pairs:
            acc_ref[p] = pv[p] if diagonal else acc_ref[p] + pv[p]
        return jnp.max(top)

    def cond(state):
        kb, top = state
        return jnp.logical_and(kb >= 0, top > SB_LOG_CUTOFF)

    def body(state):
        kb, _ = state
        return kb - 1, step(kb, False)

    lax.while_loop(cond, body, (qb - 1, step(qb, True)))
    for pair in range(n_pairs):
        o_ref[:, pair * LANES:(pair + 1) * LANES] = acc_ref[pair].astype(BF16)


def _stickbreak(q, k, v):
    s, w = q.shape
    n_pairs = w // LANES
    return pl.pallas_call(
        _sb_kernel,
        out_shape=jax.ShapeDtypeStruct((s, w), BF16),
        grid=(s // Q_BLOCK,),
        in_specs=[pl.BlockSpec((Q_BLOCK, w), lambda i: (i, 0)), _resident((s, w)), _resident((s, w))],
        out_specs=pl.BlockSpec((Q_BLOCK, w), lambda i: (i, 0)),
        scratch_shapes=[pltpu.VMEM((n_pairs, 2 * Q_BLOCK, LANES), BF16),
                        pltpu.VMEM((n_pairs, 2 * Q_BLOCK, LANES), F32),
                        pltpu.VMEM((n_pairs, Q_BLOCK, LANES), F32)],
        compiler_params=_params(1),
        name="stickbreak",
    )(q, k, v)


def _dsa_kernel(*refs):
    n_g = len(DSA_GROUPS)
    ins = [refs[5 * g:5 * g + 5] for g in range(n_g)]
    o_ref = refs[5 * n_g]
    scratch = refs[5 * n_g + 1:]
    step = pl.program_id(0)
    lane = lax.broadcasted_iota(jnp.int32, (Q_BLOCK, LANES), 1)
    head0 = lane < HEAD_DIM

    def own_lanes(rows):
        ln = lax.broadcasted_iota(jnp.int32, (rows, LANES), 1)
        rw = lax.broadcasted_iota(jnp.int32, (rows, LANES), 0)
        return (ln < HEAD_DIM) == (rw < rows // 2)

    own_q, own_v = own_lanes(2 * Q_BLOCK), own_lanes(4 * Q_BLOCK)
    qi = lax.broadcasted_iota(jnp.int32, (2 * Q_BLOCK, 2 * Q_BLOCK), 0) % Q_BLOCK
    kj = lax.broadcasted_iota(jnp.int32, (2 * Q_BLOCK, 2 * Q_BLOCK), 1)
    dist = Q_BLOCK + qi - kj

    for g, (window, r) in enumerate(DSA_GROUPS):
        q_ref, k_ref, kp_ref, v_ref, vp_ref = ins[g]
        kf_ref, vf_ref, og_ref, lg_ref = scratch[4 * g:4 * g + 4]
        prev = Q_BLOCK * r
        kf_ref[:prev, :] = kp_ref[...]
        kf_ref[prev:, :] = k_ref[...]
        vf_ref[:prev, :] = vp_ref[...]
        vf_ref[prev:, :] = v_ref[...]
        in_band = (dist >= 0) & (dist <= window // r)
        n_sub = DSA_UNIT // prev

        def trip(t, _, r=r, prev=prev, n_sub=n_sub, q_ref=q_ref, kf_ref=kf_ref, vf_ref=vf_ref,
                 og_ref=og_ref, lg_ref=lg_ref, in_band=in_band):
            units = range(DSA_UNROLL)
            bases, subs = [], []
            for j in units:
                u = t * DSA_UNROLL + j
                subs.append(u // r)
                bases.append((u // r) * prev + u % r)
            sc, v2 = [], []
            for j in units:
                q = q_ref[pl.ds(bases[j], Q_BLOCK, stride=r), :]
                k = kf_ref[pl.ds(bases[j], 2 * Q_BLOCK, stride=r), :].astype(BF16)
                v = vf_ref[pl.ds(bases[j], 2 * Q_BLOCK, stride=r), :].astype(BF16)
                q2 = jnp.where(own_q, jnp.concatenate([q, q], axis=0), 0.0).astype(BF16)
                v2.append(jnp.where(own_v, jnp.concatenate([v, v], axis=0), jnp.zeros((), BF16)))
                sc.append(_dot_nt(q2, k))
            p2, m, den = [], [], []
            for j in units:
                first_key = jnp.where(step * n_sub + subs[j] > 0, 0, Q_BLOCK)
                s_j = jnp.where(in_band & (kj >= first_key), sc[j], -jnp.inf)
                m.append(jnp.max(s_j, axis=-1, keepdims=True))
                p = jnp.exp(s_j - m[j])
                den.append(jnp.sum(p, axis=-1, keepdims=True))
                p = p.astype(BF16)
                p2.append(jnp.concatenate([p[:Q_BLOCK], p[Q_BLOCK:]], axis=1))
            pv = [_dot(p2[j], v2[j]) for j in units]
            for j in units:
                rows = pl.ds(bases[j], Q_BLOCK, stride=r)
                inv = 1.0 / den[j]
                lse = m[j] + jnp.log(den[j])
                og_ref[rows, :] = pv[j] * jnp.where(head0, inv[:Q_BLOCK], inv[Q_BLOCK:])
                lg_ref[rows, :] = jnp.where(head0, lse[:Q_BLOCK], lse[Q_BLOCK:])
            return 0

        lax.fori_loop(0, DSA_UNIT // Q_BLOCK // DSA_UNROLL, trip, 0)

    lse = [scratch[4 * g + 3][...] for g in range(n_g)]
    top = functools.reduce(jnp.maximum, lse)
    e = [jnp.exp(l - top) for l in lse]
    num = sum(e[g] * scratch[4 * g + 2][...] for g in range(n_g))
    o_ref[...] = (num / sum(e)).astype(o_ref.dtype)


def _dilated(q, k, v):
    s = q.shape[0]
    n_pairs = DSA_OUT_W // LANES
    in_specs, scratch = [], []
    for g, (_, r) in enumerate(DSA_GROUPS):
        prev = Q_BLOCK * r
        per_unit = DSA_UNIT // prev
        cur = pl.BlockSpec((DSA_UNIT, LANES), lambda i, p, g=g: (i, n_pairs * g + p))
        prv = pl.BlockSpec((prev, LANES),
                           lambda i, p, g=g, n=per_unit: (jnp.maximum(i * n - 1, 0), n_pairs * g + p))
        in_specs += [cur, cur, prv, cur, prv]
        scratch += [pltpu.VMEM((prev + DSA_UNIT, LANES), F32), pltpu.VMEM((prev + DSA_UNIT, LANES), F32),
                    pltpu.VMEM((DSA_UNIT, LANES), F32), pltpu.VMEM((DSA_UNIT, LANES), F32)]
    args = []
    for _ in DSA_GROUPS:
        args += [q, k, k, v, v]
    return pl.pallas_call(
        _dsa_kernel,
        out_shape=jax.ShapeDtypeStruct((s, DSA_OUT_W), BF16),
        grid=(s // DSA_UNIT, n_pairs),
        in_specs=in_specs,
        out_specs=pl.BlockSpec((DSA_UNIT, LANES), lambda i, p: (i, p)),
        scratch_shapes=scratch,
        compiler_params=_params(2),
        name="dilated",
    )(*args)


def _merge_kernel(x_ref, g_ref, wg_ref, bg_ref, ya_ref, yb_ref, yc_ref, wa_ref, wb_ref, wc_ref,
                  wo_ref, o_ref):
    x = x_ref[...]
    d = x.shape[1]
    h = _rms(x, g_ref[...]).astype(BF16)
    merged = None
    for i, (y_ref, w_ref) in enumerate(((ya_ref, wa_ref), (yb_ref, wb_ref), (yc_ref, wc_ref))):
        logits = _dot(h, wg_ref[:, i * d:(i + 1) * d]) + bg_ref[:, i * d:(i + 1) * d]
        term = _dot(y_ref[...], w_ref[...]) / (1.0 + jnp.exp(-logits))
        merged = term if merged is None else merged + term
    o_ref[...] = x + _dot(merged.astype(BF16), wo_ref[...])


def _merge(x, g, wg, bg, ya, yb, yc, wa, wb, wc, wo):
    s, d = x.shape
    row = lambda i: (i, 0)
    tile = lambda w: pl.BlockSpec((ROW_TILE, w), row)
    return pl.pallas_call(
        _merge_kernel,
        out_shape=jax.ShapeDtypeStruct((s, d), F32),
        grid=(s // ROW_TILE,),
        in_specs=[tile(d), _resident((1, d)), _resident(wg.shape), _resident(bg.shape),
                  tile(SB_W), tile(DSA_OUT_W), tile(MEM_W),
                  _resident(wa.shape), _resident(wb.shape), _resident(wc.shape), _resident(wo.shape)],
        out_specs=tile(d),
        compiler_params=_params(1),
        name="merge",
    )(x, g, wg, bg, ya, yb, yc, wa, wb, wc, wo)


def _rope_tables(s):
    half = HEAD_DIM // 2
    inv_freq = jnp.power(ROPE_THETA, -jnp.arange(half, dtype=F32) / half)
    ang = jnp.arange(s).astype(F32)[:, None] * inv_freq[None, :]
    cos, sin = jnp.cos(ang), jnp.sin(ang)
    reps = LANES // HEAD_DIM
    return (jnp.tile(jnp.concatenate([cos, cos], axis=1), (1, reps)),
            jnp.tile(jnp.concatenate([-sin, sin], axis=1), (1, reps)))


def _layer(x, mem, p):
    s = x.shape[0]
    bf = lambda w: w.astype(BF16)
    vec = lambda v: v.reshape(1, -1)
    heads = lambda v, n: jnp.tile(v, n).reshape(1, -1)
    cos, sin_signed = _rope_tables(s)

    x = _ffn(x, vec(p["ffn1_norm"]), bf(p["ffn1_w1"]), bf(p["ffn1_w3"]), bf(p["ffn1_w2"]))
    km, vm = _memkv(mem, vec(p["mem_norm"]), bf(p["w_mem_kv"]), heads(p["kn_mem"], MEM_HEADS))
    n_dsa = DSA_W // HEAD_DIM
    qa, ka, va, qb, kb, vb, yc = _proj(
        x, vec(p["mix_norm"]), bf(p["w_in"]), cos, sin_signed,
        heads(p["qn_dsa"], n_dsa), heads(p["kn_dsa"], n_dsa), heads(p["qn_mem"], MEM_HEADS), km, vm)
    ya = _stickbreak(qa, ka, va)
    yb = _dilated(qb, kb, vb)
    x = _merge(x, vec(p["mix_norm"]), bf(p["w_gate"]), vec(p["b_gate"]), ya, yb, yc,
               bf(p["w_branch_sb"]), bf(p["w_branch_dsa"]), bf(p["w_branch_mem"]), bf(p["w_out"]))
    return _ffn(x, vec(p["ffn2_norm"]), bf(p["ffn2_w1"]), bf(p["ffn2_w3"]), bf(p["ffn2_w2"]))


_PARAM_NAMES = ("ffn1_norm", "ffn1_w1", "ffn1_w3", "ffn1_w2", "mix_norm", "mem_norm", "w_in", "w_mem_kv",
                "qn_dsa", "kn_dsa", "qn_mem", "kn_mem", "w_branch_sb", "w_branch_dsa", "w_branch_mem",
                "w_gate", "b_gate", "w_out", "ffn2_norm", "ffn2_w1", "ffn2_w3", "ffn2_w2")


def kernel(x, mem, ffn1_norm, ffn1_w1, ffn1_w3, ffn1_w2, mix_norm, mem_norm, w_in, w_mem_kv, qn_dsa, kn_dsa, qn_mem, kn_mem, w_branch_sb, w_branch_dsa, w_branch_mem, w_gate, b_gate, w_out, ffn2_norm, ffn2_w1, ffn2_w3, ffn2_w2):
    stacked = dict(zip(_PARAM_NAMES, (ffn1_norm, ffn1_w1, ffn1_w3, ffn1_w2, mix_norm, mem_norm, w_in,
                                      w_mem_kv, qn_dsa, kn_dsa, qn_mem, kn_mem, w_branch_sb, w_branch_dsa,
                                      w_branch_mem, w_gate, b_gate, w_out, ffn2_norm, ffn2_w1, ffn2_w3,
                                      ffn2_w2)))
    depth = ffn1_norm.shape[0]
    outs = []
    for b in range(x.shape[0]):
        xb = x[b]
        for l in range(depth):
            xb = _layer(xb, mem[b], {k: v[l] for k, v in stacked.items()})
        outs.append(xb)
    return jnp.stack(outs)
```

```python
import functools

import jax
import jax.numpy as jnp
from jax import lax
from jax.experimental import pallas as pl
from jax.experimental.pallas import tpu as pltpu

F32 = jnp.float32
BF16 = jnp.bfloat16

HEAD_DIM = 64
SB_HEADS = 8
DSA_GROUPS = ((128, 1), (512, 4), (2048, 16))
DSA_HEADS_PER_GROUP = 4
MEM_HEADS = 4
ROPE_THETA = 10000.0
NORM_EPS = 1e-6
Q_BLOCK = 128
SB_W = SB_HEADS * HEAD_DIM
DSA_W = DSA_HEADS_PER_GROUP * len(DSA_GROUPS) * HEAD_DIM
DSA_OUT_W = DSA_HEADS_PER_GROUP * HEAD_DIM
MEM_W = MEM_HEADS * HEAD_DIM
QK_SCALE = HEAD_DIM ** -0.5

LANES = 128
MXU_WIDTH = 256
DSA_UNIT = Q_BLOCK * max(r for _, r in DSA_GROUPS)
DSA_UNROLL = 4
ROW_TILE = 512
VMEM_LIMIT = 56 * 1024 * 1024
SB_LOG_CUTOFF = -104.0
SB_MASKED_LOGIT = -1e30
SB_FUSED_BLOCKS = 3


def _resident(shape):
    zeros = (0,) * len(shape)
    return pl.BlockSpec(shape, lambda *_: zeros, pipeline_mode=pl.Buffered(1))


def _params(n_axes):
    return pltpu.CompilerParams(dimension_semantics=("arbitrary",) * n_axes,
                                vmem_limit_bytes=VMEM_LIMIT)


def _rms(x, g):
    return x * lax.rsqrt(jnp.mean(x * x, axis=-1, keepdims=True) + NORM_EPS) * g


def _dot(a, b):
    return jnp.dot(a, b, preferred_element_type=F32)


def _dot_nt(a, b):
    return lax.dot_general(a, b, (((1,), (1,)), ((), ())), preferred_element_type=F32)


def _split_bf16(x):
    hi = x.astype(BF16)
    lo = (x - hi.astype(F32)).astype(BF16)
    return hi, lo


def _head_norm(x, g):
    n = x.shape[-1]
    w = min(n, MXU_WIDTH)
    r = lax.broadcasted_iota(jnp.int32, (w, w), 0) // HEAD_DIM
    c = lax.broadcasted_iota(jnp.int32, (w, w), 1) // HEAD_DIM
    bd = jnp.where(r == c, 1.0, 0.0).astype(BF16)
    hi, lo = _split_bf16(x * x)
    ms = jnp.concatenate([_dot(hi[:, j:j + w], bd) + _dot(lo[:, j:j + w], bd) for j in range(0, n, w)],
                         axis=1) * (1.0 / HEAD_DIM)
    return x * lax.rsqrt(ms + NORM_EPS) * g


def _ffn_kernel(x_ref, g_ref, w1_ref, w3_ref, w2_ref, o_ref):
    x = x_ref[...]
    h = _rms(x, g_ref[...]).astype(BF16)
    a = _dot(h, w1_ref[...])
    b = _dot(h, w3_ref[...])
    act = (a * b / (1.0 + jnp.exp(-a))).astype(BF16)
    o_ref[...] = x + 0.5 * _dot(act, w2_ref[...])


def _ffn(x, g, w1, w3, w2):
    s, d = x.shape
    f = w1.shape[1]
    row = lambda i: (i, 0)
    return pl.pallas_call(
        _ffn_kernel,
        out_shape=jax.ShapeDtypeStruct((s, d), F32),
        grid=(s // ROW_TILE,),
        in_specs=[pl.BlockSpec((ROW_TILE, d), row), _resident((1, d)),
                  _resident((d, f)), _resident((d, f)), _resident((f, d))],
        out_specs=pl.BlockSpec((ROW_TILE, d), row),
        compiler_params=_params(1),
        name="ffn",
    )(x, g, w1, w3, w2)


def _memkv_kernel(mem_ref, g_ref, w_ref, kn_ref, k_ref, v_ref):
    h = _rms(mem_ref[...], g_ref[...]).astype(BF16)
    kv = _dot(h, w_ref[...])
    k_ref[...] = _head_norm(kv[:, :MEM_W], kn_ref[...]).astype(BF16)
    v_ref[...] = kv[:, MEM_W:].astype(BF16)


def _memkv(mem, g, w, kn):
    m = mem.shape[0]
    out = jax.ShapeDtypeStruct((m, MEM_W), BF16)
    return pl.pallas_call(_memkv_kernel, out_shape=(out, out), name="memkv",
                          compiler_params=pltpu.CompilerParams(vmem_limit_bytes=VMEM_LIMIT),
                          )(mem, g, w, kn)


def _rope(x, cos, sin_signed):
    lane = lax.broadcasted_iota(jnp.int32, (x.shape[0], LANES), 1)
    first_half = (lane % HEAD_DIM) < (HEAD_DIM // 2)
    out = []
    for j in range(x.shape[1] // LANES):
        xs = x[:, j * LANES:(j + 1) * LANES]
        partner = jnp.where(first_half, pltpu.roll(xs, LANES - HEAD_DIM // 2, 1),
                            pltpu.roll(xs, HEAD_DIM // 2, 1))
        out.append(xs * cos + partner * sin_signed)
    return jnp.concatenate(out, axis=1)


def _proj_kernel(x_ref, g_ref, w_ref, cos_ref, sin_ref, qn_d_ref, kn_d_ref, qn_m_ref, km_ref, vm_ref,
                 qa_ref, ka_ref, va_ref, qb_ref, kb_ref, vb_ref, yc_ref):
    h = _rms(x_ref[...], g_ref[...]).astype(BF16)
    cos, sin_signed = cos_ref[...], sin_ref[...]

    def cols(lo, width):
        return _dot(h, w_ref[:, lo:lo + width])

    qa_ref[...] = (cols(0, SB_W) * QK_SCALE).astype(BF16)
    ka_ref[...] = cols(SB_W, SB_W).astype(BF16)
    va_ref[...] = cols(2 * SB_W, SB_W).astype(BF16)
    base = 3 * SB_W
    qb_ref[...] = _rope(_head_norm(cols(base, DSA_W), qn_d_ref[...]), cos, sin_signed) * QK_SCALE
    kb_ref[...] = _rope(_head_norm(cols(base + DSA_W, DSA_W), kn_d_ref[...]), cos, sin_signed)
    vb_ref[...] = cols(base + 2 * DSA_W, DSA_W)

    qc = _head_norm(cols(base + 3 * DSA_W, MEM_W), qn_m_ref[...]) * QK_SCALE
    lane = lax.broadcasted_iota(jnp.int32, (qc.shape[0], LANES), 1)
    for j in range(MEM_W // LANES):
        sl = slice(j * LANES, (j + 1) * LANES)
        qs, km, vm = qc[:, sl], km_ref[:, sl], vm_ref[:, sl]
        outs = []
        for half in range(2):
            in_head = (lane < HEAD_DIM) == (half == 0)
            sc = _dot_nt(jnp.where(in_head, qs, 0.0).astype(BF16), km)
            p = jnp.exp(sc - jnp.max(sc, axis=-1, keepdims=True))
            outs.append(_dot(p.astype(BF16), vm) / jnp.sum(p, axis=-1, keepdims=True))
        yc_ref[:, sl] = jnp.where(lane < HEAD_DIM, outs[0], outs[1]).astype(BF16)


def _proj(x, g, w_in, cos, sin_signed, qn_d, kn_d, qn_m, km, vm):
    s, d = x.shape
    row = lambda i: (i, 0)
    tile = lambda w: pl.BlockSpec((ROW_TILE, w), row)
    sb = jax.ShapeDtypeStruct((s, SB_W), BF16)
    dsa = jax.ShapeDtypeStruct((s, DSA_W), F32)
    return pl.pallas_call(
        _proj_kernel,
        out_shape=(sb, sb, sb, dsa, dsa, dsa, jax.ShapeDtypeStruct((s, MEM_W), BF16)),
        grid=(s // ROW_TILE,),
        in_specs=[tile(d), _resident((1, d)), _resident(w_in.shape), tile(LANES), tile(LANES),
                  _resident((1, DSA_W)), _resident((1, DSA_W)), _resident((1, MEM_W)),
                  _resident(km.shape), _resident(vm.shape)],
        out_specs=(tile(SB_W), tile(SB_W), tile(SB_W), tile(DSA_W), tile(DSA_W), tile(DSA_W),
                   tile(MEM_W)),
        compiler_params=_params(1),
        name="proj",
    )(x, g, w_in, cos, sin_signed, qn_d, kn_d, qn_m, km, vm)


def _sb_kernel(q_ref, k_ref, v_ref, o_ref, q2_ref, carry_ref, acc_ref):
    qb = pl.program_id(0)
    n_pairs = SB_W // LANES
    lane = lax.broadcasted_iota(jnp.int32, (2 * Q_BLOCK, LANES), 1)
    row = lax.broadcasted_iota(jnp.int32, (2 * Q_BLOCK, LANES), 0)
    own_lanes = (lane < HEAD_DIM) == (row < Q_BLOCK)
    before = lane < (row % Q_BLOCK)
    j = lax.broadcasted_iota(jnp.int32, (2 * Q_BLOCK, 2 * Q_BLOCK), 0) % Q_BLOCK
    s = lax.broadcasted_iota(jnp.int32, (2 * Q_BLOCK, 2 * Q_BLOCK), 1)
    suffix = jnp.where((s >= Q_BLOCK) | (j > s), 1.0, 0.0).astype(BF16)

    def stack(x):
        return jnp.where(own_lanes, jnp.concatenate([x, x], axis=0), jnp.zeros((), x.dtype))

    for pair in range(n_pairs):
        q2_ref[pair] = stack(q_ref[:, pair * LANES:(pair + 1) * LANES])

    pairs = range(n_pairs)
    sls = [slice(p * LANES, (p + 1) * LANES) for p in pairs]

    def walk(kbs, from_diagonal):
        blocks = range(len(kbs))
        rows = [pl.ds(pl.multiple_of(kb * Q_BLOCK, Q_BLOCK), Q_BLOCK) for kb in kbs]
        z = [[_dot_nt(q2_ref[p], k_ref[rows[b], sls[p]]) for p in pairs] for b in blocks]
        lsg = [[None] * n_pairs for _ in blocks]
        split = [[None] * n_pairs for _ in blocks]
        for b in blocks:
            for p in pairs:
                zz = z[b][p]
                if from_diagonal and b == 0:
                    zz = jnp.where(before, zz, SB_MASKED_LOGIT)
                lsg[b][p] = jnp.minimum(zz, 0.0) - jnp.log(1.0 + jnp.exp(-jnp.abs(zz)))
                lf = lsg[b][p] - zz
                split[b][p] = jnp.concatenate(_split_bf16(lf), axis=1)
        sums = [[_dot(split[b][p], suffix) for p in pairs] for b in blocks]
        w2, top = [], None
        for p in pairs:
            carry = None if from_diagonal else carry_ref[p]
            ws = []
            for b in blocks:
                logw = lsg[b][p] + sums[b][p][:, :Q_BLOCK]
                total = sums[b][p][:, Q_BLOCK:]
                if carry is not None:
                    logw = logw + carry
                    total = total + carry
                carry = total
                w = jnp.exp(logw).astype(BF16)
                ws += [w[:Q_BLOCK], w[Q_BLOCK:]]
            w2.append(jnp.concatenate(ws, axis=1))
            carry_ref[p] = carry
            top = carry if top is None else jnp.maximum(top, carry)
        for p in pairs:
            v2 = jnp.concatenate([stack(v_ref[rows[b], sls[p]]) for b in blocks], axis=0)
            pv = _dot(w2[p], v2)
            acc_ref[p] = pv if from_diagonal else acc_ref[p] + pv
        return jnp.max(top)

    def cond(state):
        kb, top = state
        return jnp.logical_and(kb >= 0, top > SB_LOG_CUTOFF)

    def body(state):
        kb, _ = state
        return kb - 1, walk([kb], False)

    n_fused = SB_FUSED_BLOCKS
    first = lax.cond(qb >= n_fused - 1,
                     lambda: (qb - n_fused, walk([qb - b for b in range(n_fused)], True)),
                     lambda: (qb - 1, walk([qb], True)))
    lax.while_loop(cond, body, first)
    for pair in range(n_pairs):
        o_ref[:, pair * LANES:(pair + 1) * LANES] = acc_ref[pair].astype(BF16)


def _stickbreak(q, k, v):
    s, w = q.shape
    n_pairs = w // LANES
    return pl.pallas_call(
        _sb_kernel,
        out_shape=jax.ShapeDtypeStruct((s, w), BF16),
        grid=(s // Q_BLOCK,),
        in_specs=[pl.BlockSpec((Q_BLOCK, w), lambda i: (i, 0)), _resident((s, w)), _resident((s, w))],
        out_specs=pl.BlockSpec((Q_BLOCK, w), lambda i: (i, 0)),
        scratch_shapes=[pltpu.VMEM((n_pairs, 2 * Q_BLOCK, LANES), BF16),
                        pltpu.VMEM((n_pairs, 2 * Q_BLOCK, LANES), F32),
                        pltpu.VMEM((n_pairs, Q_BLOCK, LANES), F32)],
        compiler_params=_params(1),
        name="stickbreak",
    )(q, k, v)


def _dsa_kernel(*refs):
    n_g = len(DSA_GROUPS)
    ins = [refs[5 * g:5 * g + 5] for g in range(n_g)]
    o_ref = refs[5 * n_g]
    scratch = refs[5 * n_g + 1:]
    step = pl.program_id(0)
    lane = lax.broadcasted_iota(jnp.int32, (Q_BLOCK, LANES), 1)
    head0 = lane < HEAD_DIM

    def own_lanes(rows):
        ln = lax.broadcasted_iota(jnp.int32, (rows, LANES), 1)
        rw = lax.broadcasted_iota(jnp.int32, (rows, LANES), 0)
        return (ln < HEAD_DIM) == (rw < rows // 2)

    own_q, own_v = own_lanes(2 * Q_BLOCK), own_lanes(4 * Q_BLOCK)
    qi = lax.broadcasted_iota(jnp.int32, (2 * Q_BLOCK, 2 * Q_BLOCK), 0) % Q_BLOCK
    kj = lax.broadcasted_iota(jnp.int32, (2 * Q_BLOCK, 2 * Q_BLOCK), 1)
    dist = Q_BLOCK + qi - kj

    for g, (window, r) in enumerate(DSA_GROUPS):
        q_ref, k_ref, kp_ref, v_ref, vp_ref = ins[g]
        kf_ref, vf_ref, og_ref, lg_ref = scratch[4 * g:4 * g + 4]
        prev = Q_BLOCK * r
        kf_ref[:prev, :] = kp_ref[...]
        kf_ref[prev:, :] = k_ref[...]
        vf_ref[:prev, :] = vp_ref[...]
        vf_ref[prev:, :] = v_ref[...]
        in_band = (dist >= 0) & (dist <= window // r)
        n_sub = DSA_UNIT // prev

        def trip(t, _, r=r, prev=prev, n_sub=n_sub, q_ref=q_ref, kf_ref=kf_ref, vf_ref=vf_ref,
                 og_ref=og_ref, lg_ref=lg_ref, in_band=in_band):
            units = range(DSA_UNROLL)
            bases, subs = [], []
            for j in units:
                u = t * DSA_UNROLL + j
                subs.append(u // r)
                bases.append((u // r) * prev + u % r)
            sc, v2 = [], []
            for j in units:
                q = q_ref[pl.ds(bases[j], Q_BLOCK, stride=r), :]
                k = kf_ref[pl.ds(bases[j], 2 * Q_BLOCK, stride=r), :].astype(BF16)
                v = vf_ref[pl.ds(bases[j], 2 * Q_BLOCK, stride=r), :].astype(BF16)
                q2 = jnp.where(own_q, jnp.concatenate([q, q], axis=0), 0.0).astype(BF16)
                v2.append(jnp.where(own_v, jnp.concatenate([v, v], axis=0), jnp.zeros((), BF16)))
                sc.append(_dot_nt(q2, k))
            p2, m, den = [], [], []
            for j in units:
                first_key = jnp.where(step * n_sub + subs[j] > 0, 0, Q_BLOCK)
                s_j = jnp.where(in_band & (kj >= first_key), sc[j], -jnp.inf)
                m.append(jnp.max(s_j, axis=-1, keepdims=True))
                p = jnp.exp(s_j - m[j])
                den.append(jnp.sum(p, axis=-1, keepdims=True))
                p = p.astype(BF16)
                p2.append(jnp.concatenate([p[:Q_BLOCK], p[Q_BLOCK:]], axis=1))
            pv = [_dot(p2[j], v2[j]) for j in units]
            for j in units:
                rows = pl.ds(bases[j], Q_BLOCK, stride=r)
                inv = 1.0 / den[j]
                lse = m[j] + jnp.log(den[j])
                og_ref[rows, :] = pv[j] * jnp.where(head0, inv[:Q_BLOCK], inv[Q_BLOCK:])
                lg_ref[rows, :] = jnp.where(head0, lse[:Q_BLOCK], lse[Q_BLOCK:])
            return 0

        lax.fori_loop(0, DSA_UNIT // Q_BLOCK // DSA_UNROLL, trip, 0)

    lse = [scratch[4 * g + 3][...] for g in range(n_g)]
    top = functools.reduce(jnp.maximum, lse)
    e = [jnp.exp(l - top) for l in lse]
    num = sum(e[g] * scratch[4 * g + 2][...] for g in range(n_g))
    o_ref[...] = (num / sum(e)).astype(o_ref.dtype)


def _dilated(q, k, v):
    s = q.shape[0]
    n_pairs = DSA_OUT_W // LANES
    in_specs, scratch = [], []
    for g, (_, r) in enumerate(DSA_GROUPS):
        prev = Q_BLOCK * r
        per_unit = DSA_UNIT // prev
        cur = pl.BlockSpec((DSA_UNIT, LANES), lambda i, p, g=g: (i, n_pairs * g + p))
        prv = pl.BlockSpec((prev, LANES),
                           lambda i, p, g=g, n=per_unit: (jnp.maximum(i * n - 1, 0), n_pairs * g + p))
        in_specs += [cur, cur, prv, cur, prv]
        scratch += [pltpu.VMEM((prev + DSA_UNIT, LANES), F32), pltpu.VMEM((prev + DSA_UNIT, LANES), F32),
                    pltpu.VMEM((DSA_UNIT, LANES), F32), pltpu.VMEM((DSA_UNIT, LANES), F32)]
    args = []
    for _ in DSA_GROUPS:
        args += [q, k, k, v, v]
    return pl.pallas_call(
        _dsa_kernel,
        out_shape=jax.ShapeDtypeStruct((s, DSA_OUT_W), BF16),
        grid=(s // DSA_UNIT, n_pairs),
        in_specs=in_specs,
        out_specs=pl.BlockSpec((DSA_UNIT, LANES), lambda i, p: (i, p)),
        scratch_shapes=scratch,
        compiler_params=_params(2),
        name="dilated",
    )(*args)


def _merge_kernel(x_ref, g_ref, wg_ref, bg_ref, ya_ref, yb_ref, yc_ref, wa_ref, wb_ref, wc_ref,
                  wo_ref, o_ref):
    x = x_ref[...]
    d = x.shape[1]
    h = _rms(x, g_ref[...]).astype(BF16)
    merged = None
    for i, (y_ref, w_ref) in enumerate(((ya_ref, wa_ref), (yb_ref, wb_ref), (yc_ref, wc_ref))):
        logits = _dot(h, wg_ref[:, i * d:(i + 1) * d]) + bg_ref[:, i * d:(i + 1) * d]
        term = _dot(y_ref[...], w_ref[...]) / (1.0 + jnp.exp(-logits))
        merged = term if merged is None else merged + term
    o_ref[...] = x + _dot(merged.astype(BF16), wo_ref[...])


def _merge(x, g, wg, bg, ya, yb, yc, wa, wb, wc, wo):
    s, d = x.shape
    row = lambda i: (i, 0)
    tile = lambda w: pl.BlockSpec((ROW_TILE, w), row)
    return pl.pallas_call(
        _merge_kernel,
        out_shape=jax.ShapeDtypeStruct((s, d), F32),
        grid=(s // ROW_TILE,),
        in_specs=[tile(d), _resident((1, d)), _resident(wg.shape), _resident(bg.shape),
                  tile(SB_W), tile(DSA_OUT_W), tile(MEM_W),
                  _resident(wa.shape), _resident(wb.shape), _resident(wc.shape), _resident(wo.shape)],
        out_specs=tile(d),
        compiler_params=_params(1),
        name="merge",
    )(x, g, wg, bg, ya, yb, yc, wa, wb, wc, wo)


def _rope_tables(s):
    half = HEAD_DIM // 2
    inv_freq = jnp.power(ROPE_THETA, -jnp.arange(half, dtype=F32) / half)
    ang = jnp.arange(s).astype(F32)[:, None] * inv_freq[None, :]
    cos, sin = jnp.cos(ang), jnp.sin(ang)
    reps = LANES // HEAD_DIM
    return (jnp.tile(jnp.concatenate([cos, cos], axis=1), (1, reps)),
            jnp.tile(jnp.concatenate([-sin, sin], axis=1), (1, reps)))


def _layer(x, mem, p):
    s = x.shape[0]
    bf = lambda w: w.astype(BF16)
    vec = lambda v: v.reshape(1, -1)
    heads = lambda v, n: jnp.tile(v, n).reshape(1, -1)
    cos, sin_signed = _rope_tables(s)

    x = _ffn(x, vec(p["ffn1_norm"]), bf(p["ffn1_w1"]), bf(p["ffn1_w3"]), bf(p["ffn1_w2"]))
    km, vm = _memkv(mem, vec(p["mem_norm"]), bf(p["w_mem_kv"]), heads(p["kn_mem"], MEM_HEADS))
    n_dsa = DSA_W // HEAD_DIM
    qa, ka, va, qb, kb, vb, yc = _proj(
        x, vec(p["mix_norm"]), bf(p["w_in"]), cos, sin_signed,
        heads(p["qn_dsa"], n_dsa), heads(p["kn_dsa"], n_dsa), heads(p["qn_mem"], MEM_HEADS), km, vm)
    ya = _stickbreak(qa, ka, va)
    yb = _dilated(qb, kb, vb)
    x = _merge(x, vec(p["mix_norm"]), bf(p["w_gate"]), vec(p["b_gate"]), ya, yb, yc,
               bf(p["w_branch_sb"]), bf(p["w_branch_dsa"]), bf(p["w_branch_mem"]), bf(p["w_out"]))
    return _ffn(x, vec(p["ffn2_norm"]), bf(p["ffn2_w1"]), bf(p["ffn2_w3"]), bf(p["ffn2_w2"]))


_PARAM_NAMES = ("ffn1_norm", "ffn1_w1", "ffn1_w3", "ffn1_w2", "mix_norm", "mem_norm", "w_in", "w_mem_kv",
                "qn_dsa", "kn_dsa", "qn_mem", "kn_mem", "w_branch_sb", "w_branch_dsa", "w_branch_mem",
                "w_gate", "b_gate", "w_out", "ffn2_norm", "ffn2_w1", "ffn2_w3", "ffn2_w2")


def kernel(x, mem, ffn1_norm, ffn1_w1, ffn1_w3, ffn1_w2, mix_norm, mem_norm, w_in, w_mem_kv, qn_dsa, kn_dsa, qn_mem, kn_mem, w_branch_sb, w_branch_dsa, w_branch_mem, w_gate, b_gate, w_out, ffn2_norm, ffn2_w1, ffn2_w3, ffn2_w2):
    stacked = dict(zip(_PARAM_NAMES, (ffn1_norm, ffn1_w1, ffn1_w3, ffn1_w2, mix_norm, mem_norm, w_in,
                                      w_mem_kv, qn_dsa, kn_dsa, qn_mem, kn_mem, w_branch_sb, w_branch_dsa,
                                      w_branch_mem, w_gate, b_gate, w_out, ffn2_norm, ffn2_w1, ffn2_w3,
                                      ffn2_w2)))
    depth = ffn1_norm.shape[0]
    outs = []
    for b in range(x.shape[0]):
        xb = x[b]
        for l in range(depth):
            xb = _layer(xb, mem[b], {k: v[l] for k, v in stacked.items()})
        outs.append(xb)
    return jnp.stack(outs)
```

```python
import functools

import jax
import jax.numpy as jnp
from jax import lax
from jax.experimental import pallas as pl
from jax.experimental.pallas import tpu as pltpu

F32 = jnp.float32
BF16 = jnp.bfloat16

HEAD_DIM = 64
SB_HEADS = 8
DSA_GROUPS = ((128, 1), (512, 4), (2048, 16))
DSA_HEADS_PER_GROUP = 4
MEM_HEADS = 4
ROPE_THETA = 10000.0
NORM_EPS = 1e-6
Q_BLOCK = 128
SB_W = SB_HEADS * HEAD_DIM
DSA_W = DSA_HEADS_PER_GROUP * len(DSA_GROUPS) * HEAD_DIM
DSA_OUT_W = DSA_HEADS_PER_GROUP * HEAD_DIM
MEM_W = MEM_HEADS * HEAD_DIM
QK_SCALE = HEAD_DIM ** -0.5

LANES = 128
MXU_WIDTH = 256
DSA_UNIT = Q_BLOCK * max(r for _, r in DSA_GROUPS)
DSA_UNROLL = 4
DSA_STAGE = 4
ROW_TILE = 512
VMEM_LIMIT = 56 * 1024 * 1024
SB_LOG_CUTOFF = -104.0
SB_MASKED_LOGIT = -1e30
SB_FUSED_BLOCKS = 3


def _resident(shape):
    zeros = (0,) * len(shape)
    return pl.BlockSpec(shape, lambda *_: zeros, pipeline_mode=pl.Buffered(1))


def _params(n_axes):
    return pltpu.CompilerParams(dimension_semantics=("arbitrary",) * n_axes,
                                vmem_limit_bytes=VMEM_LIMIT)


def _rms(x, g):
    return x * lax.rsqrt(jnp.mean(x * x, axis=-1, keepdims=True) + NORM_EPS) * g


def _dot(a, b):
    return jnp.dot(a, b, preferred_element_type=F32)


def _dot_nt(a, b):
    return lax.dot_general(a, b, (((1,), (1,)), ((), ())), preferred_element_type=F32)


def _split_bf16(x):
    hi = x.astype(BF16)
    lo = (x - hi.astype(F32)).astype(BF16)
    return hi, lo


def _head_norm(x, g):
    n = x.shape[-1]
    w = min(n, MXU_WIDTH)
    r = lax.broadcasted_iota(jnp.int32, (w, w), 0) // HEAD_DIM
    c = lax.broadcasted_iota(jnp.int32, (w, w), 1) // HEAD_DIM
    bd = jnp.where(r == c, 1.0, 0.0).astype(BF16)
    hi, lo = _split_bf16(x * x)
    ms = jnp.concatenate([_dot(hi[:, j:j + w], bd) + _dot(lo[:, j:j + w], bd) for j in range(0, n, w)],
                         axis=1) * (1.0 / HEAD_DIM)
    return x * lax.rsqrt(ms + NORM_EPS) * g


def _ffn_kernel(x_ref, g_ref, w1_ref, w3_ref, w2_ref, o_ref):
    x = x_ref[...]
    h = _rms(x, g_ref[...]).astype(BF16)
    a = _dot(h, w1_ref[...])
    b = _dot(h, w3_ref[...])
    act = (a * b / (1.0 + jnp.exp(-a))).astype(BF16)
    o_ref[...] = x + 0.5 * _dot(act, w2_ref[...])


def _ffn(x, g, w1, w3, w2):
    s, d = x.shape
    f = w1.shape[1]
    row = lambda i: (i, 0)
    return pl.pallas_call(
        _ffn_kernel,
        out_shape=jax.ShapeDtypeStruct((s, d), F32),
        grid=(s // ROW_TILE,),
        in_specs=[pl.BlockSpec((ROW_TILE, d), row), _resident((1, d)),
                  _resident((d, f)), _resident((d, f)), _resident((f, d))],
        out_specs=pl.BlockSpec((ROW_TILE, d), row),
        compiler_params=_params(1),
        name="ffn",
    )(x, g, w1, w3, w2)


def _memkv_kernel(mem_ref, g_ref, w_ref, kn_ref, k_ref, v_ref):
    h = _rms(mem_ref[...], g_ref[...]).astype(BF16)
    kv = _dot(h, w_ref[...])
    k_ref[...] = _head_norm(kv[:, :MEM_W], kn_ref[...]).astype(BF16)
    v_ref[...] = kv[:, MEM_W:].astype(BF16)


def _memkv(mem, g, w, kn):
    m = mem.shape[0]
    out = jax.ShapeDtypeStruct((m, MEM_W), BF16)
    return pl.pallas_call(_memkv_kernel, out_shape=(out, out), name="memkv",
                          compiler_params=pltpu.CompilerParams(vmem_limit_bytes=VMEM_LIMIT),
                          )(mem, g, w, kn)


def _rope(x, cos, sin_signed):
    lane = lax.broadcasted_iota(jnp.int32, (x.shape[0], LANES), 1)
    first_half = (lane % HEAD_DIM) < (HEAD_DIM // 2)
    out = []
    for j in range(x.shape[1] // LANES):
        xs = x[:, j * LANES:(j + 1) * LANES]
        partner = jnp.where(first_half, pltpu.roll(xs, LANES - HEAD_DIM // 2, 1),
                            pltpu.roll(xs, HEAD_DIM // 2, 1))
        out.append(xs * cos + partner * sin_signed)
    return jnp.concatenate(out, axis=1)


def _proj_kernel(x_ref, g_ref, w_ref, cos_ref, sin_ref, qn_d_ref, kn_d_ref, qn_m_ref, km_ref, vm_ref,
                 qa_ref, ka_ref, va_ref, qb_ref, kb_ref, vb_ref, yc_ref):
    h = _rms(x_ref[...], g_ref[...]).astype(BF16)
    cos, sin_signed = cos_ref[...], sin_ref[...]

    def cols(lo, width):
        return _dot(h, w_ref[:, lo:lo + width])

    qa_ref[...] = (cols(0, SB_W) * QK_SCALE).astype(BF16)
    ka_ref[...] = cols(SB_W, SB_W).astype(BF16)
    va_ref[...] = cols(2 * SB_W, SB_W).astype(BF16)
    base = 3 * SB_W
    qb_ref[...] = _rope(_head_norm(cols(base, DSA_W), qn_d_ref[...]), cos, sin_signed) * QK_SCALE
    kb_ref[...] = _rope(_head_norm(cols(base + DSA_W, DSA_W), kn_d_ref[...]), cos, sin_signed)
    vb_ref[...] = cols(base + 2 * DSA_W, DSA_W)

    qc = _head_norm(cols(base + 3 * DSA_W, MEM_W), qn_m_ref[...]) * QK_SCALE
    lane = lax.broadcasted_iota(jnp.int32, (qc.shape[0], LANES), 1)
    for j in range(MEM_W // LANES):
        sl = slice(j * LANES, (j + 1) * LANES)
        qs, km, vm = qc[:, sl], km_ref[:, sl], vm_ref[:, sl]
        outs = []
        for half in range(2):
            in_head = (lane < HEAD_DIM) == (half == 0)
            sc = _dot_nt(jnp.where(in_head, qs, 0.0).astype(BF16), km)
            p = jnp.exp(sc - jnp.max(sc, axis=-1, keepdims=True))
            outs.append(_dot(p.astype(BF16), vm) / jnp.sum(p, axis=-1, keepdims=True))
        yc_ref[:, sl] = jnp.where(lane < HEAD_DIM, outs[0], outs[1]).astype(BF16)


def _proj(x, g, w_in, cos, sin_signed, qn_d, kn_d, qn_m, km, vm):
    s, d = x.shape
    row = lambda i: (i, 0)
    tile = lambda w: pl.BlockSpec((ROW_TILE, w), row)
    sb = jax.ShapeDtypeStruct((s, SB_W), BF16)
    dsa = jax.ShapeDtypeStruct((s, DSA_W), F32)
    return pl.pallas_call(
        _proj_kernel,
        out_shape=(sb, sb, sb, dsa, dsa, dsa, jax.ShapeDtypeStruct((s, MEM_W), BF16)),
        grid=(s // ROW_TILE,),
        in_specs=[tile(d), _resident((1, d)), _resident(w_in.shape), tile(LANES), tile(LANES),
                  _resident((1, DSA_W)), _resident((1, DSA_W)), _resident((1, MEM_W)),
                  _resident(km.shape), _resident(vm.shape)],
        out_specs=(tile(SB_W), tile(SB_W), tile(SB_W), tile(DSA_W), tile(DSA_W), tile(DSA_W),
                   tile(MEM_W)),
        compiler_params=_params(1),
        name="proj",
    )(x, g, w_in, cos, sin_signed, qn_d, kn_d, qn_m, km, vm)


def _sb_kernel(q_ref, k_ref, v_ref, o_ref, q2_ref, carry_ref, acc_ref):
    qb = pl.program_id(0)
    n_pairs = SB_W // LANES
    lane = lax.broadcasted_iota(jnp.int32, (2 * Q_BLOCK, LANES), 1)
    row = lax.broadcasted_iota(jnp.int32, (2 * Q_BLOCK, LANES), 0)
    own_lanes = (lane < HEAD_DIM) == (row < Q_BLOCK)
    before = lane < (row % Q_BLOCK)
    j = lax.broadcasted_iota(jnp.int32, (2 * Q_BLOCK, 2 * Q_BLOCK), 0) % Q_BLOCK
    s = lax.broadcasted_iota(jnp.int32, (2 * Q_BLOCK, 2 * Q_BLOCK), 1)
    suffix = jnp.where((s >= Q_BLOCK) | (j > s), 1.0, 0.0).astype(BF16)

    def stack(x):
        return jnp.where(own_lanes, jnp.concatenate([x, x], axis=0), jnp.zeros((), x.dtype))

    for pair in range(n_pairs):
        q2_ref[pair] = stack(q_ref[:, pair * LANES:(pair + 1) * LANES])

    pairs = range(n_pairs)
    sls = [slice(p * LANES, (p + 1) * LANES) for p in pairs]

    def walk(kbs, from_diagonal):
        blocks = range(len(kbs))
        rows = [pl.ds(pl.multiple_of(kb * Q_BLOCK, Q_BLOCK), Q_BLOCK) for kb in kbs]
        z = [[_dot_nt(q2_ref[p], k_ref[rows[b], sls[p]]) for p in pairs] for b in blocks]
        lsg = [[None] * n_pairs for _ in blocks]
        split = [[None] * n_pairs for _ in blocks]
        for b in blocks:
            for p in pairs:
                zz = z[b][p]
                if from_diagonal and b == 0:
                    zz = jnp.where(before, zz, SB_MASKED_LOGIT)
                lsg[b][p] = jnp.minimum(zz, 0.0) - jnp.log(1.0 + jnp.exp(-jnp.abs(zz)))
                lf = lsg[b][p] - zz
                split[b][p] = jnp.concatenate(_split_bf16(lf), axis=1)
        sums = [[_dot(split[b][p], suffix) for p in pairs] for b in blocks]
        w2, top = [], None
        for p in pairs:
            carry = None if from_diagonal else carry_ref[p]
            ws = []
            for b in blocks:
                logw = lsg[b][p] + sums[b][p][:, :Q_BLOCK]
                total = sums[b][p][:, Q_BLOCK:]
                if carry is not None:
                    logw = logw + carry
                    total = total + carry
                carry = total
                w = jnp.exp(logw).astype(BF16)
                ws += [w[:Q_BLOCK], w[Q_BLOCK:]]
            w2.append(jnp.concatenate(ws, axis=1))
            carry_ref[p] = carry
            top = carry if top is None else jnp.maximum(top, carry)
        for p in pairs:
            v2 = jnp.concatenate([stack(v_ref[rows[b], sls[p]]) for b in blocks], axis=0)
            pv = _dot(w2[p], v2)
            acc_ref[p] = pv if from_diagonal else acc_ref[p] + pv
        return jnp.max(top)

    def cond(state):
        kb, top = state
        return jnp.logical_and(kb >= 0, top > SB_LOG_CUTOFF)

    def body(state):
        kb, _ = state
        return kb - 1, walk([kb], False)

    n_fused = SB_FUSED_BLOCKS
    first = lax.cond(qb >= n_fused - 1,
                     lambda: (qb - n_fused, walk([qb - b for b in range(n_fused)], True)),
                     lambda: (qb - 1, walk([qb], True)))
    lax.while_loop(cond, body, first)
    for pair in range(n_pairs):
        o_ref[:, pair * LANES:(pair + 1) * LANES] = acc_ref[pair].astype(BF16)


def _stickbreak(q, k, v):
    s, w = q.shape
    n_pairs = w // LANES
    return pl.pallas_call(
        _sb_kernel,
        out_shape=jax.ShapeDtypeStruct((s, w), BF16),
        grid=(s // Q_BLOCK,),
        in_specs=[pl.BlockSpec((Q_BLOCK, w), lambda i: (i, 0)), _resident((s, w)), _resident((s, w))],
        out_specs=pl.BlockSpec((Q_BLOCK, w), lambda i: (i, 0)),
        scratch_shapes=[pltpu.VMEM((n_pairs, 2 * Q_BLOCK, LANES), BF16),
                        pltpu.VMEM((n_pairs, 2 * Q_BLOCK, LANES), F32),
                        pltpu.VMEM((n_pairs, Q_BLOCK, LANES), F32)],
        compiler_params=_params(1),
        name="stickbreak",
    )(q, k, v)


def _dsa_kernel(*refs):
    n_g = len(DSA_GROUPS)
    ins = [refs[5 * g:5 * g + 5] for g in range(n_g)]
    o_ref = refs[5 * n_g]
    bias_ref = refs[5 * n_g + 1]
    out_scratch = refs[5 * n_g + 2:5 * n_g + 2 + 2 * n_g]
    stage_scratch = refs[5 * n_g + 2 + 2 * n_g:]
    step = pl.program_id(0)
    lane = lax.broadcasted_iota(jnp.int32, (Q_BLOCK, LANES), 1)
    head0 = lane < HEAD_DIM

    def own_lanes(rows):
        ln = lax.broadcasted_iota(jnp.int32, (rows, LANES), 1)
        rw = lax.broadcasted_iota(jnp.int32, (rows, LANES), 0)
        return (ln < HEAD_DIM) == (rw < rows // 2)

    own_q, own_v = own_lanes(2 * Q_BLOCK), own_lanes(4 * Q_BLOCK)
    qi = lax.broadcasted_iota(jnp.int32, (2 * Q_BLOCK, 2 * Q_BLOCK), 0) % Q_BLOCK
    kj = lax.broadcasted_iota(jnp.int32, (2 * Q_BLOCK, 2 * Q_BLOCK), 1)
    dist = Q_BLOCK + qi - kj
    in_band = (dist >= 0) & (dist <= Q_BLOCK)
    bias_ref[0] = jnp.where(in_band, 0.0, -jnp.inf)
    bias_ref[1] = jnp.where(in_band & (kj >= Q_BLOCK), 0.0, -jnp.inf)
    seq_start = jnp.where(step == 0, 1, 0)

    def attend(qs, ks, vs, biases):
        units = range(len(qs))
        sc, v2 = [], []
        for j in units:
            q2 = jnp.where(own_q, jnp.concatenate([qs[j], qs[j]], axis=0), 0.0).astype(BF16)
            v = vs[j].astype(BF16)
            v2.append(jnp.where(own_v, jnp.concatenate([v, v], axis=0), jnp.zeros((), BF16)))
            sc.append(_dot_nt(q2, ks[j].astype(BF16)))
        p2, m, den = [], [], []
        for j in units:
            s_j = sc[j] + biases[j]
            m.append(jnp.max(s_j, axis=-1, keepdims=True))
            p = jnp.exp(s_j - m[j])
            den.append(jnp.sum(p, axis=-1, keepdims=True))
            p = p.astype(BF16)
            p2.append(jnp.concatenate([p[:Q_BLOCK], p[Q_BLOCK:]], axis=1))
        pv = [_dot(p2[j], v2[j]) for j in units]
        outs = []
        for j in units:
            inv = 1.0 / den[j]
            lse = m[j] + jnp.log(den[j])
            outs.append((pv[j] * jnp.where(head0, inv[:Q_BLOCK], inv[Q_BLOCK:]),
                         jnp.where(head0, lse[:Q_BLOCK], lse[Q_BLOCK:])))
        return outs

    n_trips = DSA_UNIT // Q_BLOCK // DSA_UNROLL
    staged = 0
    for g, (window, r) in enumerate(DSA_GROUPS):
        assert window // r == Q_BLOCK
        q_ref, k_ref, kp_ref, v_ref, vp_ref = ins[g]
        og_ref, lg_ref = out_scratch[2 * g:2 * g + 2]
        prev = Q_BLOCK * r

        if r <= DSA_STAGE:
            assert DSA_UNROLL % r == 0

            def trip(t, first, r=r, prev=prev, q_ref=q_ref, k_ref=k_ref, kp_ref=kp_ref, v_ref=v_ref,
                     vp_ref=vp_ref, og_ref=og_ref, lg_ref=lg_ref):
                qs, ks, vs, biases, rows = [], [], [], [], []
                for j in range(DSA_UNROLL):
                    sub, c = t * (DSA_UNROLL // r) + j // r, j % r
                    rows.append(pl.ds(sub * prev + c, Q_BLOCK, stride=r))
                    qs.append(q_ref[rows[j], :])
                    if first and sub == 0:
                        half = pl.ds(c, Q_BLOCK, stride=r)
                        ks.append(jnp.concatenate([kp_ref[half, :], k_ref[half, :]], axis=0))
                        vs.append(jnp.concatenate([vp_ref[half, :], v_ref[half, :]], axis=0))
                        biases.append(bias_ref[seq_start])
                    else:
                        both = pl.ds((sub - 1) * prev + c, 2 * Q_BLOCK, stride=r)
                        ks.append(k_ref[both, :])
                        vs.append(v_ref[both, :])
                        biases.append(bias_ref[0])
                for j, (o, lse) in enumerate(attend(qs, ks, vs, biases)):
                    og_ref[rows[j], :] = o
                    lg_ref[rows[j], :] = lse

            trip(0, True)

            def later_trip(t, carry, trip=trip):
                trip(t, False)
                return carry

            lax.fori_loop(1, n_trips, later_trip, 0)
        else:
            inner = r // DSA_STAGE
            assert inner <= DSA_STAGE and prev == DSA_UNIT and DSA_UNROLL == DSA_STAGE
            qs_ref, ks_ref, vs_ref, os_ref, ls_ref = stage_scratch[5 * staged:5 * staged + 5]
            staged += 1
            half = Q_BLOCK * inner
            for c in range(DSA_STAGE):
                coarse = pl.ds(c, half, stride=DSA_STAGE)
                qs_ref[c] = q_ref[coarse, :]
                ks_ref[c, :half] = kp_ref[coarse, :]
                ks_ref[c, half:] = k_ref[coarse, :]
                vs_ref[c, :half] = vp_ref[coarse, :]
                vs_ref[c, half:] = v_ref[coarse, :]

            def staged_trip(t, carry, inner=inner, qs_ref=qs_ref, ks_ref=ks_ref, vs_ref=vs_ref,
                            os_ref=os_ref, ls_ref=ls_ref):
                fine_q = pl.ds(t, Q_BLOCK, stride=inner)
                fine_k = pl.ds(t, 2 * Q_BLOCK, stride=inner)
                bias = bias_ref[seq_start]
                outs = attend([qs_ref.at[c][fine_q, :] for c in range(DSA_STAGE)],
                              [ks_ref.at[c][fine_k, :] for c in range(DSA_STAGE)],
                              [vs_ref.at[c][fine_k, :] for c in range(DSA_STAGE)],
                              [bias] * DSA_STAGE)
                for c, (o, lse) in enumerate(outs):
                    os_ref.at[c][fine_q, :] = o
                    ls_ref.at[c][fine_q, :] = lse
                return carry

            lax.fori_loop(0, n_trips, staged_trip, 0)
            for c in range(DSA_STAGE):
                coarse = pl.ds(c, half, stride=DSA_STAGE)
                og_ref[coarse, :] = os_ref[c]
                lg_ref[coarse, :] = ls_ref[c]

    lse = [out_scratch[2 * g + 1][...] for g in range(n_g)]
    top = functools.reduce(jnp.maximum, lse)
    e = [jnp.exp(l - top) for l in lse]
    num = sum(e[g] * out_scratch[2 * g][...] for g in range(n_g))
    o_ref[...] = (num / sum(e)).astype(o_ref.dtype)


def _dilated(q, k, v):
    s = q.shape[0]
    n_pairs = DSA_OUT_W // LANES
    in_specs, stage_scratch = [], []
    for g, (_, r) in enumerate(DSA_GROUPS):
        prev = Q_BLOCK * r
        per_unit = DSA_UNIT // prev
        cur = pl.BlockSpec((DSA_UNIT, LANES), lambda i, p, g=g: (i, n_pairs * g + p))
        prv = pl.BlockSpec((prev, LANES),
                           lambda i, p, g=g, n=per_unit: (jnp.maximum(i * n - 1, 0), n_pairs * g + p))
        in_specs += [cur, cur, prv, cur, prv]
        if r > DSA_STAGE:
            rows = DSA_UNIT // DSA_STAGE
            stage_scratch += [pltpu.VMEM((DSA_STAGE, rows, LANES), F32),
                              pltpu.VMEM((DSA_STAGE, 2 * rows, LANES), F32),
                              pltpu.VMEM((DSA_STAGE, 2 * rows, LANES), F32),
                              pltpu.VMEM((DSA_STAGE, rows, LANES), F32),
                              pltpu.VMEM((DSA_STAGE, rows, LANES), F32)]
    scratch = ([pltpu.VMEM((2, 2 * Q_BLOCK, 2 * Q_BLOCK), F32)]
               + [pltpu.VMEM((DSA_UNIT, LANES), F32)] * (2 * len(DSA_GROUPS)) + stage_scratch)
    args = []
    for _ in DSA_GROUPS:
        args += [q, k, k, v, v]
    return pl.pallas_call(
        _dsa_kernel,
        out_shape=jax.ShapeDtypeStruct((s, DSA_OUT_W), BF16),
        grid=(s // DSA_UNIT, n_pairs),
        in_specs=in_specs,
        out_specs=pl.BlockSpec((DSA_UNIT, LANES), lambda i, p: (i, p)),
        scratch_shapes=scratch,
        compiler_params=_params(2),
        name="dilated",
    )(*args)


def _merge_kernel(x_ref, g_ref, wg_ref, bg_ref, ya_ref, yb_ref, yc_ref, wa_ref, wb_ref, wc_ref,
                  wo_ref, o_ref):
    x = x_ref[...]
    d = x.shape[1]
    h = _rms(x, g_ref[...]).astype(BF16)
    merged = None
    for i, (y_ref, w_ref) in enumerate(((ya_ref, wa_ref), (yb_ref, wb_ref), (yc_ref, wc_ref))):
        logits = _dot(h, wg_ref[:, i * d:(i + 1) * d]) + bg_ref[:, i * d:(i + 1) * d]
        term = _dot(y_ref[...], w_ref[...]) / (1.0 + jnp.exp(-logits))
        merged = term if merged is None else merged + term
    o_ref[...] = x + _dot(merged.astype(BF16), wo_ref[...])


def _merge(x, g, wg, bg, ya, yb, yc, wa, wb, wc, wo):
    s, d = x.shape
    row = lambda i: (i, 0)
    tile = lambda w: pl.BlockSpec((ROW_TILE, w), row)
    return pl.pallas_call(
        _merge_kernel,
        out_shape=jax.ShapeDtypeStruct((s, d), F32),
        grid=(s // ROW_TILE,),
        in_specs=[tile(d), _resident((1, d)), _resident(wg.shape), _resident(bg.shape),
                  tile(SB_W), tile(DSA_OUT_W), tile(MEM_W),
                  _resident(wa.shape), _resident(wb.shape), _resident(wc.shape), _resident(wo.shape)],
        out_specs=tile(d),
        compiler_params=_params(1),
        name="merge",
    )(x, g, wg, bg, ya, yb, yc, wa, wb, wc, wo)


def _rope_tables(s):
    half = HEAD_DIM // 2
    inv_freq = jnp.power(ROPE_THETA, -jnp.arange(half, dtype=F32) / half)
    ang = jnp.arange(s).astype(F32)[:, None] * inv_freq[None, :]
    cos, sin = jnp.cos(ang), jnp.sin(ang)
    reps = LANES // HEAD_DIM
    return (jnp.concatenate([cos, cos] * reps, axis=1), jnp.concatenate([-sin, sin] * reps, axis=1))


def _layer(x, mem, p):
    s = x.shape[0]
    bf = lambda w: w.astype(BF16)
    vec = lambda v: v.reshape(1, -1)
    heads = lambda v, n: jnp.tile(v, n).reshape(1, -1)
    cos, sin_signed = _rope_tables(s)

    x = _ffn(x, vec(p["ffn1_norm"]), bf(p["ffn1_w1"]), bf(p["ffn1_w3"]), bf(p["ffn1_w2"]))
    km, vm = _memkv(mem, vec(p["mem_norm"]), bf(p["w_mem_kv"]), heads(p["kn_mem"], MEM_HEADS))
    n_dsa = DSA_W // HEAD_DIM
    qa, ka, va, qb, kb, vb, yc = _proj(
        x, vec(p["mix_norm"]), bf(p["w_in"]), cos, sin_signed,
        heads(p["qn_dsa"], n_dsa), heads(p["kn_dsa"], n_dsa), heads(p["qn_mem"], MEM_HEADS), km, vm)
    ya = _stickbreak(qa, ka, va)
    yb = _dilated(qb, kb, vb)
    x = _merge(x, vec(p["mix_norm"]), bf(p["w_gate"]), vec(p["b_gate"]), ya, yb, yc,
               bf(p["w_branch_sb"]), bf(p["w_branch_dsa"]), bf(p["w_branch_mem"]), bf(p["w_out"]))
    return _ffn(x, vec(p["ffn2_norm"]), bf(p["ffn2_w1"]), bf(p["ffn2_w3"]), bf(p["ffn2_w2"]))


_PARAM_NAMES = ("ffn1_norm", "ffn1_w1", "ffn1_w3", "ffn1_w2", "mix_norm", "mem_norm", "w_in", "w_mem_kv",
                "qn_dsa", "kn_dsa", "qn_mem", "kn_mem", "w_branch_sb", "w_branch_dsa", "w_branch_mem",
                "w_gate", "b_gate", "w_out", "ffn2_norm", "ffn2_w1", "ffn2_w3", "ffn2_w2")


def kernel(x, mem, ffn1_norm, ffn1_w1, ffn1_w3, ffn1_w2, mix_norm, mem_norm, w_in, w_mem_kv, qn_dsa, kn_dsa, qn_mem, kn_mem, w_branch_sb, w_branch_dsa, w_branch_mem, w_gate, b_gate, w_out, ffn2_norm, ffn2_w1, ffn2_w3, ffn2_w2):
    stacked = dict(zip(_PARAM_NAMES, (ffn1_norm, ffn1_w1, ffn1_w3, ffn1_w2, mix_norm, mem_norm, w_in,
                                      w_mem_kv, qn_dsa, kn_dsa, qn_mem, kn_mem, w_branch_sb, w_branch_dsa,
                                      w_branch_mem, w_gate, b_gate, w_out, ffn2_norm, ffn2_w1, ffn2_w3,
                                      ffn2_w2)))
    depth = ffn1_norm.shape[0]
    outs = []
    for b in range(x.shape[0]):
        xb = x[b]
        for l in range(depth):
            xb = _layer(xb, mem[b], {k: v[l] for k, v in stacked.items()})
        outs.append(xb)
    return jnp.stack(outs)
```

```python
import functools

import jax
import jax.numpy as jnp
from jax import lax
from jax.experimental import pallas as pl
from jax.experimental.pallas import tpu as pltpu

F32 = jnp.float32
BF16 = jnp.bfloat16

HEAD_DIM = 64
SB_HEADS = 8
DSA_GROUPS = ((128, 1), (512, 4), (2048, 16))
DSA_HEADS_PER_GROUP = 4
MEM_HEADS = 4
ROPE_THETA = 10000.0
NORM_EPS = 1e-6
Q_BLOCK = 128
SB_W = SB_HEADS * HEAD_DIM
DSA_W = DSA_HEADS_PER_GROUP * len(DSA_GROUPS) * HEAD_DIM
DSA_OUT_W = DSA_HEADS_PER_GROUP * HEAD_DIM
MEM_W = MEM_HEADS * HEAD_DIM
QK_SCALE = HEAD_DIM ** -0.5

LANES = 128
MXU_WIDTH = 256
DSA_UNIT = Q_BLOCK * max(r for _, r in DSA_GROUPS)
DSA_UNROLL = 4
DSA_STAGE = 4
ROW_TILE = 512
VMEM_LIMIT = 56 * 1024 * 1024
CAST_CHUNK_ROWS = 128
SB_LOG_CUTOFF = -104.0
SB_MASKED_LOGIT = -1e30
SB_FUSED_BLOCKS = 3
SB_LAST_BLOCK_ROWS = 48


def _resident(shape):
    zeros = (0,) * len(shape)
    return pl.BlockSpec(shape, lambda *_: zeros, pipeline_mode=pl.Buffered(1))


def _params(n_axes):
    return pltpu.CompilerParams(dimension_semantics=("arbitrary",) * n_axes,
                                vmem_limit_bytes=VMEM_LIMIT)


def _rms(x, g):
    return x * lax.rsqrt(jnp.mean(x * x, axis=-1, keepdims=True) + NORM_EPS) * g


def _dot(a, b):
    return jnp.dot(a, b, preferred_element_type=F32)


def _dot_nt(a, b):
    return lax.dot_general(a, b, (((1,), (1,)), ((), ())), preferred_element_type=F32)


def _split_bf16(x):
    hi = x.astype(BF16)
    lo = (x - hi.astype(F32)).astype(BF16)
    return hi, lo


def _head_norm(x, g):
    n = x.shape[-1]
    w = min(n, MXU_WIDTH)
    r = lax.broadcasted_iota(jnp.int32, (w, w), 0) // HEAD_DIM
    c = lax.broadcasted_iota(jnp.int32, (w, w), 1) // HEAD_DIM
    bd = jnp.where(r == c, 1.0, 0.0).astype(BF16)
    hi, lo = _split_bf16(x * x)
    ms = jnp.concatenate([_dot(hi[:, j:j + w], bd) + _dot(lo[:, j:j + w], bd) for j in range(0, n, w)],
                         axis=1) * (1.0 / HEAD_DIM)
    return x * lax.rsqrt(ms + NORM_EPS) * g


def _ffn_kernel(x_ref, g_ref, w1_ref, w3_ref, w2_ref, o_ref):
    x = x_ref[...]
    h = _rms(x, g_ref[...]).astype(BF16)
    a = _dot(h, w1_ref[...])
    b = _dot(h, w3_ref[...])
    act = (a * b / (1.0 + jnp.exp(-a))).astype(BF16)
    o_ref[...] = x + 0.5 * _dot(act, w2_ref[...])


def _ffn(x, g, w1, w3, w2):
    s, d = x.shape
    f = w1.shape[1]
    row = lambda i: (i, 0)
    return pl.pallas_call(
        _ffn_kernel,
        out_shape=jax.ShapeDtypeStruct((s, d), F32),
        grid=(s // ROW_TILE,),
        in_specs=[pl.BlockSpec((ROW_TILE, d), row), _resident((1, d)),
                  _resident((d, f)), _resident((d, f)), _resident((f, d))],
        out_specs=pl.BlockSpec((ROW_TILE, d), row),
        compiler_params=_params(1),
        name="ffn",
    )(x, g, w1, w3, w2)


def _memkv_kernel(mem_ref, g_ref, w_ref, kn_ref, k_ref, v_ref):
    h = _rms(mem_ref[...], g_ref[...]).astype(BF16)
    kv = _dot(h, w_ref[...])
    k_ref[...] = _head_norm(kv[:, :MEM_W], kn_ref[...]).astype(BF16)
    v_ref[...] = kv[:, MEM_W:].astype(BF16)


def _memkv(mem, g, w, kn):
    m = mem.shape[0]
    out = jax.ShapeDtypeStruct((m, MEM_W), BF16)
    return pl.pallas_call(_memkv_kernel, out_shape=(out, out), name="memkv",
                          compiler_params=pltpu.CompilerParams(vmem_limit_bytes=VMEM_LIMIT),
                          )(mem, g, w, kn)


def _rope(x, cos, sin_signed):
    lane = lax.broadcasted_iota(jnp.int32, (x.shape[0], LANES), 1)
    first_half = (lane % HEAD_DIM) < (HEAD_DIM // 2)
    out = []
    for j in range(x.shape[1] // LANES):
        xs = x[:, j * LANES:(j + 1) * LANES]
        partner = jnp.where(first_half, pltpu.roll(xs, LANES - HEAD_DIM // 2, 1),
                            pltpu.roll(xs, HEAD_DIM // 2, 1))
        out.append(xs * cos + partner * sin_signed)
    return jnp.concatenate(out, axis=1)


def _rope_lanes(cos_half, sin_half):
    half = HEAD_DIM // 2
    f = lax.broadcasted_iota(jnp.int32, (half, LANES), 0)
    lane = lax.broadcasted_iota(jnp.int32, (half, LANES), 1)
    hit = (lane % half) == f
    spread = jnp.where(hit, 1.0, 0.0).astype(BF16)
    signed = jnp.where(hit, jnp.where((lane % HEAD_DIM) < half, -1.0, 1.0), 0.0).astype(BF16)
    c_hi, c_lo = _split_bf16(cos_half)
    s_hi, s_lo = _split_bf16(sin_half)
    return _dot(c_hi, spread) + _dot(c_lo, spread), _dot(s_hi, signed) + _dot(s_lo, signed)


def _cast_once(src_ref, dst_ref):
    rows = src_ref.shape[0]
    chunk = min(rows, CAST_CHUNK_ROWS)

    @pl.when(pl.program_id(0) == 0)
    def _():
        def body(i, carry):
            r = pl.ds(pl.multiple_of(i * chunk, chunk), chunk)
            dst_ref[r, :] = src_ref[r, :].astype(BF16)
            return carry

        lax.fori_loop(0, rows // chunk, body, 0)


def _proj_kernel(x_ref, g_ref, w32_ref, cos_ref, sin_ref, qn_d_ref, kn_d_ref, qn_m_ref, km_ref, vm_ref,
                 qa_ref, ka_ref, va_ref, qb_ref, kb_ref, vb_ref, yc_ref, w_ref):
    _cast_once(w32_ref, w_ref)
    h = _rms(x_ref[...], g_ref[...]).astype(BF16)
    cos, sin_signed = _rope_lanes(cos_ref[...], sin_ref[...])

    def cols(lo, width):
        return _dot(h, w_ref[:, lo:lo + width])

    qa_ref[...] = (cols(0, SB_W) * QK_SCALE).astype(BF16)
    ka_ref[...] = cols(SB_W, SB_W).astype(BF16)
    va_ref[...] = cols(2 * SB_W, SB_W).astype(BF16)
    base = 3 * SB_W
    qb_ref[...] = _rope(_head_norm(cols(base, DSA_W), qn_d_ref[...]), cos, sin_signed) * QK_SCALE
    kb_ref[...] = _rope(_head_norm(cols(base + DSA_W, DSA_W), kn_d_ref[...]), cos, sin_signed)
    vb_ref[...] = cols(base + 2 * DSA_W, DSA_W)

    qc = _head_norm(cols(base + 3 * DSA_W, MEM_W), qn_m_ref[...]) * QK_SCALE
    lane = lax.broadcasted_iota(jnp.int32, (qc.shape[0], LANES), 1)
    for j in range(MEM_W // LANES):
        sl = slice(j * LANES, (j + 1) * LANES)
        qs, km, vm = qc[:, sl], km_ref[:, sl], vm_ref[:, sl]
        outs = []
        for half in range(2):
            in_head = (lane < HEAD_DIM) == (half == 0)
            sc = _dot_nt(jnp.where(in_head, qs, 0.0).astype(BF16), km)
            p = jnp.exp(sc - jnp.max(sc, axis=-1, keepdims=True))
            outs.append(_dot(p.astype(BF16), vm) / jnp.sum(p, axis=-1, keepdims=True))
        yc_ref[:, sl] = jnp.where(lane < HEAD_DIM, outs[0], outs[1]).astype(BF16)


def _proj(x, g, w_in, cos, sin_signed, qn_d, kn_d, qn_m, km, vm):
    s, d = x.shape
    row = lambda i: (i, 0)
    tile = lambda w: pl.BlockSpec((ROW_TILE, w), row)
    sb = jax.ShapeDtypeStruct((s, SB_W), BF16)
    dsa = jax.ShapeDtypeStruct((s, DSA_W), F32)
    return pl.pallas_call(
        _proj_kernel,
        out_shape=(sb, sb, sb, dsa, dsa, dsa, jax.ShapeDtypeStruct((s, MEM_W), BF16)),
        grid=(s // ROW_TILE,),
        in_specs=[tile(d), _resident((1, d)), _resident(w_in.shape), tile(HEAD_DIM // 2), tile(HEAD_DIM // 2),
                  _resident((1, DSA_W)), _resident((1, DSA_W)), _resident((1, MEM_W)),
                  _resident(km.shape), _resident(vm.shape)],
        out_specs=(tile(SB_W), tile(SB_W), tile(SB_W), tile(DSA_W), tile(DSA_W), tile(DSA_W),
                   tile(MEM_W)),
        scratch_shapes=[pltpu.VMEM(w_in.shape, BF16)],
        compiler_params=_params(1),
        name="proj",
    )(x, g, w_in, cos, sin_signed, qn_d, kn_d, qn_m, km, vm)


def _sb_kernel(q_ref, k_ref, v_ref, o_ref, q2_ref, carry_ref, acc_ref):
    qb = pl.program_id(0)
    n_pairs = SB_W // LANES
    lane = lax.broadcasted_iota(jnp.int32, (2 * Q_BLOCK, LANES), 1)
    row = lax.broadcasted_iota(jnp.int32, (2 * Q_BLOCK, LANES), 0)
    own_lanes = (lane < HEAD_DIM) == (row < Q_BLOCK)
    before = lane < (row % Q_BLOCK)
    j = lax.broadcasted_iota(jnp.int32, (2 * Q_BLOCK, 2 * Q_BLOCK), 0) % Q_BLOCK
    s = lax.broadcasted_iota(jnp.int32, (2 * Q_BLOCK, 2 * Q_BLOCK), 1)
    suffix = jnp.where((s >= Q_BLOCK) | (j >= s), 1.0, 0.0).astype(BF16)

    def stack(x):
        return jnp.where(own_lanes, jnp.concatenate([x, x], axis=0), jnp.zeros((), x.dtype))

    for pair in range(n_pairs):
        q2_ref[pair] = stack(q_ref[:, pair * LANES:(pair + 1) * LANES])

    pairs = range(n_pairs)
    sls = [slice(p * LANES, (p + 1) * LANES) for p in pairs]

    def walk(kbs, from_diagonal, last_rows=Q_BLOCK):
        n_b = len(kbs)
        blocks = range(n_b)
        partial = last_rows < Q_BLOCK
        assert not partial or (from_diagonal and n_b > 1)

        def visiting(x, b):
            if partial and b == n_b - 1:
                return jnp.concatenate([x[:last_rows], x[Q_BLOCK:Q_BLOCK + last_rows]], axis=0)
            return x

        rows = [pl.ds(pl.multiple_of(kb * Q_BLOCK, Q_BLOCK), Q_BLOCK) for kb in kbs]
        z = [[_dot_nt(visiting(q2_ref[p], b), k_ref[rows[b], sls[p]]) for p in pairs] for b in blocks]
        split = [[None] * n_pairs for _ in blocks]
        for b in blocks:
            for p in pairs:
                if from_diagonal and b == 0:
                    z[b][p] = jnp.where(before, z[b][p], SB_MASKED_LOGIT)
                nz = -z[b][p]
                lf = jnp.minimum(nz, 0.0) - jnp.log(1.0 + jnp.exp(jnp.minimum(z[b][p], nz)))
                split[b][p] = jnp.concatenate(_split_bf16(lf), axis=1)
        sums = [[_dot(split[b][p], suffix) for p in pairs] for b in blocks]
        w2, w_last, top = [], [], None
        for p in pairs:
            carry = None if from_diagonal else carry_ref[p]
            ws = []
            for b in blocks:
                logw = z[b][p] + sums[b][p][:, :Q_BLOCK]
                total = sums[b][p][:, Q_BLOCK:]
                if carry is not None:
                    logw = logw + visiting(carry, b)
                    total = total + visiting(carry, b)
                w = jnp.exp(logw).astype(BF16)
                half = w.shape[0] // 2
                if partial and b == n_b - 1:
                    w_last.append(jnp.concatenate([w[:half], w[half:]], axis=1))
                    carry = jnp.concatenate([total[:half], carry[last_rows:Q_BLOCK],
                                             total[half:], carry[Q_BLOCK + last_rows:]], axis=0)
                else:
                    ws += [w[:half], w[half:]]
                    carry = total
            w2.append(jnp.concatenate(ws, axis=1))
            if not partial:
                carry_ref[p] = carry
            top = carry if top is None else jnp.maximum(top, carry)
        n_full = n_b - 1 if partial else n_b
        for p in pairs:
            v2 = jnp.concatenate([stack(v_ref[rows[b], sls[p]]) for b in range(n_full)], axis=0)
            pv = _dot(w2[p], v2)
            if partial:
                extra = _dot(w_last[p], stack(v_ref[rows[n_b - 1], sls[p]]))
                pv = jnp.concatenate([pv[:last_rows] + extra, pv[last_rows:]], axis=0)
            acc_ref[p] = pv if from_diagonal else acc_ref[p] + pv
        return jnp.max(top)

    def cond(state):
        kb, top = state
        return jnp.logical_and(kb >= 0, top > SB_LOG_CUTOFF)

    def body(state):
        kb, _ = state
        return kb - 1, walk([kb], False)

    def general():
        lax.while_loop(cond, body, (qb - 1, walk([qb], True)))

    n_fused = SB_FUSED_BLOCKS
    fused_top = lax.cond(qb >= n_fused - 1,
                         lambda: walk([qb - b for b in range(n_fused)], True, SB_LAST_BLOCK_ROWS),
                         lambda: jnp.float32(jnp.inf))
    lax.cond(fused_top > SB_LOG_CUTOFF, general, lambda: None)
    for pair in range(n_pairs):
        o_ref[:, pair * LANES:(pair + 1) * LANES] = acc_ref[pair].astype(BF16)


def _stickbreak(q, k, v):
    s, w = q.shape
    n_pairs = w // LANES
    return pl.pallas_call(
        _sb_kernel,
        out_shape=jax.ShapeDtypeStruct((s, w), BF16),
        grid=(s // Q_BLOCK,),
        in_specs=[pl.BlockSpec((Q_BLOCK, w), lambda i: (i, 0)), _resident((s, w)), _resident((s, w))],
        out_specs=pl.BlockSpec((Q_BLOCK, w), lambda i: (i, 0)),
        scratch_shapes=[pltpu.VMEM((n_pairs, 2 * Q_BLOCK, LANES), BF16),
                        pltpu.VMEM((n_pairs, 2 * Q_BLOCK, LANES), F32),
                        pltpu.VMEM((n_pairs, Q_BLOCK, LANES), F32)],
        compiler_params=_params(1),
        name="stickbreak",
    )(q, k, v)


def _dsa_kernel(*refs):
    n_g = len(DSA_GROUPS)
    ins = [refs[5 * g:5 * g + 5] for g in range(n_g)]
    o_ref = refs[5 * n_g]
    bias_ref = refs[5 * n_g + 1]
    out_scratch = refs[5 * n_g + 2:5 * n_g + 2 + 2 * n_g]
    stage_scratch = refs[5 * n_g + 2 + 2 * n_g:]
    step = pl.program_id(0)
    lane = lax.broadcasted_iota(jnp.int32, (Q_BLOCK, LANES), 1)
    head0 = lane < HEAD_DIM

    def own_lanes(rows):
        ln = lax.broadcasted_iota(jnp.int32, (rows, LANES), 1)
        rw = lax.broadcasted_iota(jnp.int32, (rows, LANES), 0)
        return (ln < HEAD_DIM) == (rw < rows // 2)

    own_q, own_v = own_lanes(2 * Q_BLOCK), own_lanes(4 * Q_BLOCK)
    qi = lax.broadcasted_iota(jnp.int32, (2 * Q_BLOCK, 2 * Q_BLOCK), 0) % Q_BLOCK
    kj = lax.broadcasted_iota(jnp.int32, (2 * Q_BLOCK, 2 * Q_BLOCK), 1)
    dist = Q_BLOCK + qi - kj
    in_band = (dist >= 0) & (dist <= Q_BLOCK)
    bias_ref[0] = jnp.where(in_band, 0.0, -jnp.inf)
    bias_ref[1] = jnp.where(in_band & (kj >= Q_BLOCK), 0.0, -jnp.inf)
    seq_start = jnp.where(step == 0, 1, 0)

    def attend(qs, ks, vs, biases):
        units = range(len(qs))
        sc, v2 = [], []
        for j in units:
            q2 = jnp.where(own_q, jnp.concatenate([qs[j], qs[j]], axis=0), 0.0).astype(BF16)
            v = vs[j].astype(BF16)
            v2.append(jnp.where(own_v, jnp.concatenate([v, v], axis=0), jnp.zeros((), BF16)))
            sc.append(_dot_nt(q2, ks[j].astype(BF16)))
        p2, m, den = [], [], []
        for j in units:
            s_j = sc[j] + biases[j]
            m.append(jnp.max(s_j, axis=-1, keepdims=True))
            p = jnp.exp(s_j - m[j])
            den.append(jnp.sum(p, axis=-1, keepdims=True))
            p = p.astype(BF16)
            p2.append(jnp.concatenate([p[:Q_BLOCK], p[Q_BLOCK:]], axis=1))
        pv = [_dot(p2[j], v2[j]) for j in units]
        outs = []
        for j in units:
            inv = 1.0 / den[j]
            lse = m[j] + jnp.log(den[j])
            outs.append((pv[j] * jnp.where(head0, inv[:Q_BLOCK], inv[Q_BLOCK:]),
                         jnp.where(head0, lse[:Q_BLOCK], lse[Q_BLOCK:])))
        return outs

    n_trips = DSA_UNIT // Q_BLOCK // DSA_UNROLL
    staged = 0
    for g, (window, r) in enumerate(DSA_GROUPS):
        assert window // r == Q_BLOCK
        q_ref, k_ref, kp_ref, v_ref, vp_ref = ins[g]
        og_ref, lg_ref = out_scratch[2 * g:2 * g + 2]
        prev = Q_BLOCK * r

        if r <= DSA_STAGE:
            assert DSA_UNROLL % r == 0

            def trip(t, first, r=r, prev=prev, q_ref=q_ref, k_ref=k_ref, kp_ref=kp_ref, v_ref=v_ref,
                     vp_ref=vp_ref, og_ref=og_ref, lg_ref=lg_ref):
                qs, ks, vs, biases, rows = [], [], [], [], []
                for j in range(DSA_UNROLL):
                    sub, c = t * (DSA_UNROLL // r) + j // r, j % r
                    rows.append(pl.ds(sub * prev + c, Q_BLOCK, stride=r))
                    qs.append(q_ref[rows[j], :])
                    if first and sub == 0:
                        half = pl.ds(c, Q_BLOCK, stride=r)
                        ks.append(jnp.concatenate([kp_ref[half, :], k_ref[half, :]], axis=0))
                        vs.append(jnp.concatenate([vp_ref[half, :], v_ref[half, :]], axis=0))
                        biases.append(bias_ref[seq_start])
                    else:
                        both = pl.ds((sub - 1) * prev + c, 2 * Q_BLOCK, stride=r)
                        ks.append(k_ref[both, :])
                        vs.append(v_ref[both, :])
                        biases.append(bias_ref[0])
                for j, (o, lse) in enumerate(attend(qs, ks, vs, biases)):
                    og_ref[rows[j], :] = o
                    lg_ref[rows[j], :] = lse

            trip(0, True)

            def later_trip(t, carry, trip=trip):
                trip(t, False)
                return carry

            lax.fori_loop(1, n_trips, later_trip, 0)
        else:
            inner = r // DSA_STAGE
            assert inner <= DSA_STAGE and prev == DSA_UNIT and DSA_UNROLL == DSA_STAGE
            qs_ref, ks_ref, vs_ref, os_ref, ls_ref = stage_scratch[5 * staged:5 * staged + 5]
            staged += 1
            half = Q_BLOCK * inner
            for c in range(DSA_STAGE):
                coarse = pl.ds(c, half, stride=DSA_STAGE)
                qs_ref[c] = q_ref[coarse, :]
                ks_ref[c, :half] = kp_ref[coarse, :]
                ks_ref[c, half:] = k_ref[coarse, :]
                vs_ref[c, :half] = vp_ref[coarse, :]
                vs_ref[c, half:] = v_ref[coarse, :]

            def staged_trip(t, carry, inner=inner, qs_ref=qs_ref, ks_ref=ks_ref, vs_ref=vs_ref,
                            os_ref=os_ref, ls_ref=ls_ref):
                fine_q = pl.ds(t, Q_BLOCK, stride=inner)
                fine_k = pl.ds(t, 2 * Q_BLOCK, stride=inner)
                bias = bias_ref[seq_start]
                outs = attend([qs_ref.at[c][fine_q, :] for c in range(DSA_STAGE)],
                              [ks_ref.at[c][fine_k, :] for c in range(DSA_STAGE)],
                              [vs_ref.at[c][fine_k, :] for c in range(DSA_STAGE)],
                              [bias] * DSA_STAGE)
                for c, (o, lse) in enumerate(outs):
                    os_ref.at[c][fine_q, :] = o
                    ls_ref.at[c][fine_q, :] = lse
                return carry

            lax.fori_loop(0, n_trips, staged_trip, 0)
            for c in range(DSA_STAGE):
                coarse = pl.ds(c, half, stride=DSA_STAGE)
                og_ref[coarse, :] = os_ref[c]
                lg_ref[coarse, :] = ls_ref[c]

    lse = [out_scratch[2 * g + 1][...] for g in range(n_g)]
    top = functools.reduce(jnp.maximum, lse)
    e = [jnp.exp(l - top) for l in lse]
    num = sum(e[g] * out_scratch[2 * g][...] for g in range(n_g))
    o_ref[...] = (num / sum(e)).astype(o_ref.dtype)


def _dilated(q, k, v):
    s = q.shape[0]
    n_pairs = DSA_OUT_W // LANES
    in_specs, stage_scratch = [], []
    for g, (_, r) in enumerate(DSA_GROUPS):
        prev = Q_BLOCK * r
        per_unit = DSA_UNIT // prev
        cur = pl.BlockSpec((DSA_UNIT, LANES), lambda i, p, g=g: (i, n_pairs * g + p))
        prv = pl.BlockSpec((prev, LANES),
                           lambda i, p, g=g, n=per_unit: (jnp.maximum(i * n - 1, 0), n_pairs * g + p))
        in_specs += [cur, cur, prv, cur, prv]
        if r > DSA_STAGE:
            rows = DSA_UNIT // DSA_STAGE
            stage_scratch += [pltpu.VMEM((DSA_STAGE, rows, LANES), F32),
                              pltpu.VMEM((DSA_STAGE, 2 * rows, LANES), F32),
                              pltpu.VMEM((DSA_STAGE, 2 * rows, LANES), F32),
                              pltpu.VMEM((DSA_STAGE, rows, LANES), F32),
                              pltpu.VMEM((DSA_STAGE, rows, LANES), F32)]
    scratch = ([pltpu.VMEM((2, 2 * Q_BLOCK, 2 * Q_BLOCK), F32)]
               + [pltpu.VMEM((DSA_UNIT, LANES), F32)] * (2 * len(DSA_GROUPS)) + stage_scratch)
    args = []
    for _ in DSA_GROUPS:
        args += [q, k, k, v, v]
    return pl.pallas_call(
        _dsa_kernel,
        out_shape=jax.ShapeDtypeStruct((s, DSA_OUT_W), BF16),
        grid=(s // DSA_UNIT, n_pairs),
        in_specs=in_specs,
        out_specs=pl.BlockSpec((DSA_UNIT, LANES), lambda i, p: (i, p)),
        scratch_shapes=scratch,
        compiler_params=_params(2),
        name="dilated",
    )(*args)


def _merge_kernel(x_ref, g_ref, wg32_ref, bg_ref, ya_ref, yb_ref, yc_ref, wa32_ref, wb32_ref, wc32_ref,
                  wo32_ref, o_ref, wg_ref, wa_ref, wb_ref, wc_ref, wo_ref):
    for src, dst in ((wg32_ref, wg_ref), (wa32_ref, wa_ref), (wb32_ref, wb_ref), (wc32_ref, wc_ref),
                     (wo32_ref, wo_ref)):
        _cast_once(src, dst)
    x = x_ref[...]
    d = x.shape[1]
    h = _rms(x, g_ref[...]).astype(BF16)
    merged = None
    for i, (y_ref, w_ref) in enumerate(((ya_ref, wa_ref), (yb_ref, wb_ref), (yc_ref, wc_ref))):
        logits = _dot(h, wg_ref[:, i * d:(i + 1) * d]) + bg_ref[:, i * d:(i + 1) * d]
        term = _dot(y_ref[...], w_ref[...]) / (1.0 + jnp.exp(-logits))
        merged = term if merged is None else merged + term
    o_ref[...] = x + _dot(merged.astype(BF16), wo_ref[...])


def _merge(x, g, wg, bg, ya, yb, yc, wa, wb, wc, wo):
    s, d = x.shape
    row = lambda i: (i, 0)
    tile = lambda w: pl.BlockSpec((ROW_TILE, w), row)
    return pl.pallas_call(
        _merge_kernel,
        out_shape=jax.ShapeDtypeStruct((s, d), F32),
        grid=(s // ROW_TILE,),
        in_specs=[tile(d), _resident((1, d)), _resident(wg.shape), _resident(bg.shape),
                  tile(SB_W), tile(DSA_OUT_W), tile(MEM_W),
                  _resident(wa.shape), _resident(wb.shape), _resident(wc.shape), _resident(wo.shape)],
        out_specs=tile(d),
        scratch_shapes=[pltpu.VMEM(w.shape, BF16) for w in (wg, wa, wb, wc, wo)],
        compiler_params=_params(1),
        name="merge",
    )(x, g, wg, bg, ya, yb, yc, wa, wb, wc, wo)


def _rope_tables(s):
    half = HEAD_DIM // 2
    inv_freq = jnp.power(ROPE_THETA, -jnp.arange(half, dtype=F32) / half)
    ang = jnp.arange(s).astype(F32)[:, None] * inv_freq[None, :]
    return jnp.cos(ang), jnp.sin(ang)


def _layer(x, mem, p):
    s = x.shape[0]
    bf = lambda w: w.astype(BF16)
    vec = lambda v: v.reshape(1, -1)
    heads = lambda v, n: jnp.tile(v, n).reshape(1, -1)
    cos, sin = _rope_tables(s)

    x = _ffn(x, vec(p["ffn1_norm"]), bf(p["ffn1_w1"]), bf(p["ffn1_w3"]), bf(p["ffn1_w2"]))
    km, vm = _memkv(mem, vec(p["mem_norm"]), bf(p["w_mem_kv"]), heads(p["kn_mem"], MEM_HEADS))
    n_dsa = DSA_W // HEAD_DIM
    qa, ka, va, qb, kb, vb, yc = _proj(
        x, vec(p["mix_norm"]), p["w_in"], cos, sin,
        heads(p["qn_dsa"], n_dsa), heads(p["kn_dsa"], n_dsa), heads(p["qn_mem"], MEM_HEADS), km, vm)
    ya = _stickbreak(qa, ka, va)
    yb = _dilated(qb, kb, vb)
    x = _merge(x, vec(p["mix_norm"]), p["w_gate"], vec(p["b_gate"]), ya, yb, yc,
               p["w_branch_sb"], p["w_branch_dsa"], p["w_branch_mem"], p["w_out"])
    return _ffn(x, vec(p["ffn2_norm"]), bf(p["ffn2_w1"]), bf(p["ffn2_w3"]), bf(p["ffn2_w2"]))


_PARAM_NAMES = ("ffn1_norm", "ffn1_w1", "ffn1_w3", "ffn1_w2", "mix_norm", "mem_norm", "w_in", "w_mem_kv",
                "qn_dsa", "kn_dsa", "qn_mem", "kn_mem", "w_branch_sb", "w_branch_dsa", "w_branch_mem",
                "w_gate", "b_gate", "w_out", "ffn2_norm", "ffn2_w1", "ffn2_w3", "ffn2_w2")


def kernel(x, mem, ffn1_norm, ffn1_w1, ffn1_w3, ffn1_w2, mix_norm, mem_norm, w_in, w_mem_kv, qn_dsa, kn_dsa, qn_mem, kn_mem, w_branch_sb, w_branch_dsa, w_branch_mem, w_gate, b_gate, w_out, ffn2_norm, ffn2_w1, ffn2_w3, ffn2_w2):
    stacked = dict(zip(_PARAM_NAMES, (ffn1_norm, ffn1_w1, ffn1_w3, ffn1_w2, mix_norm, mem_norm, w_in,
                                      w_mem_kv, qn_dsa, kn_dsa, qn_mem, kn_mem, w_branch_sb, w_branch_dsa,
                                      w_branch_mem, w_gate, b_gate, w_out, ffn2_norm, ffn2_w1, ffn2_w3,
                                      ffn2_w2)))
    depth = ffn1_norm.shape[0]
    outs = []
    for b in range(x.shape[0]):
        xb = x[b]
        for l in range(depth):
            xb = _layer(xb, mem[b], {k: v[l] for k, v in stacked.items()})
        outs.append(xb)
    return jnp.stack(outs)
```

```python
import functools

import jax
import jax.numpy as jnp
from jax import lax
from jax.experimental import pallas as pl
from jax.experimental.pallas import tpu as pltpu

F32 = jnp.float32
BF16 = jnp.bfloat16

HEAD_DIM = 64
SB_HEADS = 8
DSA_GROUPS = ((128, 1), (512, 4), (2048, 16))
DSA_HEADS_PER_GROUP = 4
MEM_HEADS = 4
ROPE_THETA = 10000.0
NORM_EPS = 1e-6
Q_BLOCK = 128
SB_W = SB_HEADS * HEAD_DIM
DSA_W = DSA_HEADS_PER_GROUP * len(DSA_GROUPS) * HEAD_DIM
DSA_OUT_W = DSA_HEADS_PER_GROUP * HEAD_DIM
MEM_W = MEM_HEADS * HEAD_DIM
QK_SCALE = HEAD_DIM ** -0.5

LANES = 128
MXU_WIDTH = 256
DSA_UNIT = Q_BLOCK * max(r for _, r in DSA_GROUPS)
DSA_UNROLL = 4
DSA_STAGE = 4
ROW_TILE = 512
VMEM_LIMIT = 56 * 1024 * 1024
CAST_CHUNK_ROWS = 128
SB_LOG_CUTOFF = -104.0
SB_MASKED_LOGIT = -1e30
SB_FUSED_BLOCKS = 3
SB_QBLOCKS_PER_STEP = 2
SB_LAST_BLOCK_ROWS = 48


def _resident(shape):
    zeros = (0,) * len(shape)
    return pl.BlockSpec(shape, lambda *_: zeros, pipeline_mode=pl.Buffered(1))


def _params(n_axes):
    return pltpu.CompilerParams(dimension_semantics=("arbitrary",) * n_axes,
                                vmem_limit_bytes=VMEM_LIMIT)


def _rms(x, g):
    return x * lax.rsqrt(jnp.mean(x * x, axis=-1, keepdims=True) + NORM_EPS) * g


def _dot(a, b):
    return jnp.dot(a, b, preferred_element_type=F32)


def _dot_nt(a, b):
    return lax.dot_general(a, b, (((1,), (1,)), ((), ())), preferred_element_type=F32)


def _split_bf16(x):
    hi = x.astype(BF16)
    lo = (x - hi.astype(F32)).astype(BF16)
    return hi, lo


def _head_norm(x, g):
    n = x.shape[-1]
    w = min(n, MXU_WIDTH)
    r = lax.broadcasted_iota(jnp.int32, (w, w), 0) // HEAD_DIM
    c = lax.broadcasted_iota(jnp.int32, (w, w), 1) // HEAD_DIM
    bd = jnp.where(r == c, 1.0, 0.0).astype(BF16)
    hi, lo = _split_bf16(x * x)
    ms = jnp.concatenate([_dot(hi[:, j:j + w], bd) + _dot(lo[:, j:j + w], bd) for j in range(0, n, w)],
                         axis=1) * (1.0 / HEAD_DIM)
    return x * lax.rsqrt(ms + NORM_EPS) * g


def _ffn_kernel(x_ref, g_ref, w1_ref, w3_ref, w2_ref, o_ref):
    x = x_ref[...]
    h = _rms(x, g_ref[...]).astype(BF16)
    a = _dot(h, w1_ref[...])
    b = _dot(h, w3_ref[...])
    act = (a * b / (1.0 + jnp.exp(-a))).astype(BF16)
    o_ref[...] = x + 0.5 * _dot(act, w2_ref[...])


def _ffn(x, g, w1, w3, w2):
    s, d = x.shape
    f = w1.shape[1]
    row = lambda i: (i, 0)
    return pl.pallas_call(
        _ffn_kernel,
        out_shape=jax.ShapeDtypeStruct((s, d), F32),
        grid=(s // ROW_TILE,),
        in_specs=[pl.BlockSpec((ROW_TILE, d), row), _resident((1, d)),
                  _resident((d, f)), _resident((d, f)), _resident((f, d))],
        out_specs=pl.BlockSpec((ROW_TILE, d), row),
        compiler_params=_params(1),
        name="ffn",
    )(x, g, w1, w3, w2)


def _memkv_kernel(mem_ref, g_ref, w_ref, kn_ref, k_ref, v_ref):
    h = _rms(mem_ref[...], g_ref[...]).astype(BF16)
    kv = _dot(h, w_ref[...])
    k_ref[...] = _head_norm(kv[:, :MEM_W], kn_ref[...]).astype(BF16)
    v_ref[...] = kv[:, MEM_W:].astype(BF16)


def _memkv(mem, g, w, kn):
    m = mem.shape[0]
    out = jax.ShapeDtypeStruct((m, MEM_W), BF16)
    return pl.pallas_call(_memkv_kernel, out_shape=(out, out), name="memkv",
                          compiler_params=pltpu.CompilerParams(vmem_limit_bytes=VMEM_LIMIT),
                          )(mem, g, w, kn)


def _rope(x, cos, sin_signed):
    lane = lax.broadcasted_iota(jnp.int32, (x.shape[0], LANES), 1)
    first_half = (lane % HEAD_DIM) < (HEAD_DIM // 2)
    out = []
    for j in range(x.shape[1] // LANES):
        xs = x[:, j * LANES:(j + 1) * LANES]
        partner = jnp.where(first_half, pltpu.roll(xs, LANES - HEAD_DIM // 2, 1),
                            pltpu.roll(xs, HEAD_DIM // 2, 1))
        out.append(xs * cos + partner * sin_signed)
    return jnp.concatenate(out, axis=1)


def _rope_lanes(cos_half, sin_half):
    half = HEAD_DIM // 2
    f = lax.broadcasted_iota(jnp.int32, (half, LANES), 0)
    lane = lax.broadcasted_iota(jnp.int32, (half, LANES), 1)
    hit = (lane % half) == f
    spread = jnp.where(hit, 1.0, 0.0).astype(BF16)
    signed = jnp.where(hit, jnp.where((lane % HEAD_DIM) < half, -1.0, 1.0), 0.0).astype(BF16)
    c_hi, c_lo = _split_bf16(cos_half)
    s_hi, s_lo = _split_bf16(sin_half)
    return _dot(c_hi, spread) + _dot(c_lo, spread), _dot(s_hi, signed) + _dot(s_lo, signed)


def _cast_once(src_ref, dst_ref):
    rows = src_ref.shape[0]
    chunk = min(rows, CAST_CHUNK_ROWS)

    @pl.when(pl.program_id(0) == 0)
    def _():
        def body(i, carry):
            r = pl.ds(pl.multiple_of(i * chunk, chunk), chunk)
            dst_ref[r, :] = src_ref[r, :].astype(BF16)
            return carry

        lax.fori_loop(0, rows // chunk, body, 0)


def _proj_kernel(x_ref, g_ref, w32_ref, cos_ref, sin_ref, qn_d_ref, kn_d_ref, qn_m_ref, km_ref, vm_ref,
                 qa_ref, ka_ref, va_ref, qb_ref, kb_ref, vb_ref, yc_ref, w_ref):
    _cast_once(w32_ref, w_ref)
    h = _rms(x_ref[...], g_ref[...]).astype(BF16)
    cos, sin_signed = _rope_lanes(cos_ref[...].T, sin_ref[...].T)

    def cols(lo, width):
        return _dot(h, w_ref[:, lo:lo + width])

    qa_ref[...] = (cols(0, SB_W) * QK_SCALE).astype(BF16)
    ka_ref[...] = cols(SB_W, SB_W).astype(BF16)
    va_ref[...] = cols(2 * SB_W, SB_W).astype(BF16)
    base = 3 * SB_W
    qb_ref[...] = _rope(_head_norm(cols(base, DSA_W), qn_d_ref[...]), cos, sin_signed) * QK_SCALE
    kb_ref[...] = _rope(_head_norm(cols(base + DSA_W, DSA_W), kn_d_ref[...]), cos, sin_signed)
    vb_ref[...] = cols(base + 2 * DSA_W, DSA_W)

    qc = _head_norm(cols(base + 3 * DSA_W, MEM_W), qn_m_ref[...]) * QK_SCALE
    lane = lax.broadcasted_iota(jnp.int32, (qc.shape[0], LANES), 1)
    for j in range(MEM_W // LANES):
        sl = slice(j * LANES, (j + 1) * LANES)
        qs, km, vm = qc[:, sl], km_ref[:, sl], vm_ref[:, sl]
        outs = []
        for half in range(2):
            in_head = (lane < HEAD_DIM) == (half == 0)
            sc = _dot_nt(jnp.where(in_head, qs, 0.0).astype(BF16), km)
            p = jnp.exp(sc - jnp.max(sc, axis=-1, keepdims=True))
            outs.append(_dot(p.astype(BF16), vm) / jnp.sum(p, axis=-1, keepdims=True))
        yc_ref[:, sl] = jnp.where(lane < HEAD_DIM, outs[0], outs[1]).astype(BF16)


def _proj(x, g, w_in, cos, sin_signed, qn_d, kn_d, qn_m, km, vm):
    s, d = x.shape
    row = lambda i: (i, 0)
    tile = lambda w: pl.BlockSpec((ROW_TILE, w), row)
    sb = jax.ShapeDtypeStruct((s, SB_W), BF16)
    dsa = jax.ShapeDtypeStruct((s, DSA_W), F32)
    table = pl.BlockSpec((HEAD_DIM // 2, ROW_TILE), lambda i: (0, i))
    return pl.pallas_call(
        _proj_kernel,
        out_shape=(sb, sb, sb, dsa, dsa, dsa, jax.ShapeDtypeStruct((s, MEM_W), BF16)),
        grid=(s // ROW_TILE,),
        in_specs=[tile(d), _resident((1, d)), _resident(w_in.shape), table, table,
                  _resident((1, DSA_W)), _resident((1, DSA_W)), _resident((1, MEM_W)),
                  _resident(km.shape), _resident(vm.shape)],
        out_specs=(tile(SB_W), tile(SB_W), tile(SB_W), tile(DSA_W), tile(DSA_W), tile(DSA_W),
                   tile(MEM_W)),
        scratch_shapes=[pltpu.VMEM(w_in.shape, BF16)],
        compiler_params=_params(1),
        name="proj",
    )(x, g, w_in, cos, sin_signed, qn_d, kn_d, qn_m, km, vm)


def _sb_kernel(q_ref, k_ref, v_ref, o_ref, q2_ref, carry_ref, acc_ref):
    n_slots = SB_QBLOCKS_PER_STEP
    first_qb = pl.program_id(0) * n_slots
    n_pairs = SB_W // LANES
    lane = lax.broadcasted_iota(jnp.int32, (2 * Q_BLOCK, LANES), 1)
    row = lax.broadcasted_iota(jnp.int32, (2 * Q_BLOCK, LANES), 0)
    own_lanes = (lane < HEAD_DIM) == (row < Q_BLOCK)
    before = lane < (row % Q_BLOCK)
    j = lax.broadcasted_iota(jnp.int32, (2 * Q_BLOCK, 2 * Q_BLOCK), 0) % Q_BLOCK
    s = lax.broadcasted_iota(jnp.int32, (2 * Q_BLOCK, 2 * Q_BLOCK), 1)
    suffix = jnp.where((s >= Q_BLOCK) | (j >= s), 1.0, 0.0).astype(BF16)

    def stack(x):
        return jnp.where(own_lanes, jnp.concatenate([x, x], axis=0), jnp.zeros((), x.dtype))

    pairs = range(n_pairs)
    sls = [slice(p * LANES, (p + 1) * LANES) for p in pairs]
    for slot in range(n_slots):
        for p in pairs:
            q2_ref[slot, p] = stack(q_ref[slot * Q_BLOCK:(slot + 1) * Q_BLOCK, sls[p]])

    def walk(jobs, from_diagonal, last_rows=Q_BLOCK):
        partial = last_rows < Q_BLOCK
        assert not partial or (from_diagonal and all(len(kbs) > 1 for _, kbs in jobs))
        chains = [(i, b) for i, (_, kbs) in enumerate(jobs) for b in range(len(kbs))]

        def is_partial(i, b):
            return partial and b == len(jobs[i][1]) - 1

        def visiting(x, i, b):
            if is_partial(i, b):
                return jnp.concatenate([x[:last_rows], x[Q_BLOCK:Q_BLOCK + last_rows]], axis=0)
            return x

        rows = {(i, b): pl.ds(pl.multiple_of(jobs[i][1][b] * Q_BLOCK, Q_BLOCK), Q_BLOCK) for i, b in chains}
        z = {(i, b, p): _dot_nt(visiting(q2_ref[jobs[i][0], p], i, b), k_ref[rows[i, b], sls[p]])
             for i, b in chains for p in pairs}
        split = {}
        for i, b in chains:
            for p in pairs:
                if from_diagonal and b == 0:
                    z[i, b, p] = jnp.where(before, z[i, b, p], SB_MASKED_LOGIT)
                nz = -z[i, b, p]
                lf = jnp.minimum(nz, 0.0) - jnp.log(1.0 + jnp.exp(jnp.minimum(z[i, b, p], nz)))
                split[i, b, p] = jnp.concatenate(_split_bf16(lf), axis=1)
        sums = {(i, b, p): _dot(split[i, b, p], suffix) for i, b in chains for p in pairs}
        w2, w_last, tops = {}, {}, []
        for i, (slot, kbs) in enumerate(jobs):
            top = None
            for p in pairs:
                carry = None if from_diagonal else carry_ref[slot, p]
                ws = []
                for b in range(len(kbs)):
                    logw = z[i, b, p] + sums[i, b, p][:, :Q_BLOCK]
                    total = sums[i, b, p][:, Q_BLOCK:]
                    if carry is not None:
                        logw = logw + visiting(carry, i, b)
                        total = total + visiting(carry, i, b)
                    w = jnp.exp(logw).astype(BF16)
                    half = w.shape[0] // 2
                    if is_partial(i, b):
                        w_last[i, p] = jnp.concatenate([w[:half], w[half:]], axis=1)
                        carry = jnp.concatenate([total[:half], carry[last_rows:Q_BLOCK],
                                                 total[half:], carry[Q_BLOCK + last_rows:]], axis=0)
                    else:
                        ws += [w[:half], w[half:]]
                        carry = total
                w2[i, p] = jnp.concatenate(ws, axis=1)
                if not partial:
                    carry_ref[slot, p] = carry
                top = carry if top is None else jnp.maximum(top, carry)
            tops.append(top)
        for i, (slot, kbs) in enumerate(jobs):
            n_full = len(kbs) - 1 if partial else len(kbs)
            for p in pairs:
                v2 = jnp.concatenate([stack(v_ref[rows[i, b], sls[p]]) for b in range(n_full)], axis=0)
                pv = _dot(w2[i, p], v2)
                if partial:
                    extra = _dot(w_last[i, p], stack(v_ref[rows[i, n_full], sls[p]]))
                    pv = jnp.concatenate([pv[:last_rows] + extra, pv[last_rows:]], axis=0)
                acc_ref[slot, p] = pv if from_diagonal else acc_ref[slot, p] + pv
        return [jnp.max(top) for top in tops]

    def cond(state):
        kb, top = state
        return jnp.logical_and(kb >= 0, top > SB_LOG_CUTOFF)

    def general(slot):
        def run():
            def body(state):
                kb, _ = state
                return kb - 1, walk([(slot, [kb])], False)[0]

            qb = first_qb + slot
            lax.while_loop(cond, body, (qb - 1, walk([(slot, [qb])], True)[0]))
        return run

    n_fused = SB_FUSED_BLOCKS
    fused_tops = lax.cond(
        first_qb >= n_fused - 1,
        lambda: tuple(walk([(slot, [first_qb + slot - b for b in range(n_fused)]) for slot in range(n_slots)],
                           True, SB_LAST_BLOCK_ROWS)),
        lambda: (jnp.float32(jnp.inf),) * n_slots)
    for slot in range(n_slots):
        lax.cond(fused_tops[slot] > SB_LOG_CUTOFF, general(slot), lambda: None)
        for p in pairs:
            o_ref[slot * Q_BLOCK:(slot + 1) * Q_BLOCK, sls[p]] = acc_ref[slot, p].astype(BF16)


def _stickbreak(q, k, v):
    s, w = q.shape
    n_pairs = w // LANES
    n_slots = SB_QBLOCKS_PER_STEP
    tile = pl.BlockSpec((n_slots * Q_BLOCK, w), lambda i: (i, 0))
    return pl.pallas_call(
        _sb_kernel,
        out_shape=jax.ShapeDtypeStruct((s, w), BF16),
        grid=(s // (n_slots * Q_BLOCK),),
        in_specs=[tile, _resident((s, w)), _resident((s, w))],
        out_specs=tile,
        scratch_shapes=[pltpu.VMEM((n_slots, n_pairs, 2 * Q_BLOCK, LANES), BF16),
                        pltpu.VMEM((n_slots, n_pairs, 2 * Q_BLOCK, LANES), F32),
                        pltpu.VMEM((n_slots, n_pairs, Q_BLOCK, LANES), F32)],
        compiler_params=_params(1),
        name="stickbreak",
    )(q, k, v)


def _dsa_kernel(*refs):
    n_g = len(DSA_GROUPS)
    ins = [refs[5 * g:5 * g + 5] for g in range(n_g)]
    o_ref = refs[5 * n_g]
    bias_ref = refs[5 * n_g + 1]
    out_scratch = refs[5 * n_g + 2:5 * n_g + 2 + 2 * n_g]
    stage_scratch = refs[5 * n_g + 2 + 2 * n_g:]
    step = pl.program_id(0)
    lane = lax.broadcasted_iota(jnp.int32, (Q_BLOCK, LANES), 1)
    head0 = lane < HEAD_DIM

    def own_lanes(rows):
        ln = lax.broadcasted_iota(jnp.int32, (rows, LANES), 1)
        rw = lax.broadcasted_iota(jnp.int32, (rows, LANES), 0)
        return (ln < HEAD_DIM) == (rw < rows // 2)

    own_q, own_v = own_lanes(2 * Q_BLOCK), own_lanes(4 * Q_BLOCK)
    qi = lax.broadcasted_iota(jnp.int32, (2 * Q_BLOCK, 2 * Q_BLOCK), 0) % Q_BLOCK
    kj = lax.broadcasted_iota(jnp.int32, (2 * Q_BLOCK, 2 * Q_BLOCK), 1)
    dist = Q_BLOCK + qi - kj
    in_band = (dist >= 0) & (dist <= Q_BLOCK)
    bias_ref[0] = jnp.where(in_band, 0.0, -jnp.inf)
    bias_ref[1] = jnp.where(in_band & (kj >= Q_BLOCK), 0.0, -jnp.inf)
    seq_start = jnp.where(step == 0, 1, 0)

    def attend(qs, ks, vs, biases):
        units = range(len(qs))
        sc, v2 = [], []
        for j in units:
            q2 = jnp.where(own_q, jnp.concatenate([qs[j], qs[j]], axis=0), 0.0).astype(BF16)
            v = vs[j].astype(BF16)
            v2.append(jnp.where(own_v, jnp.concatenate([v, v], axis=0), jnp.zeros((), BF16)))
            sc.append(_dot_nt(q2, ks[j].astype(BF16)))
        p2, m, den = [], [], []
        for j in units:
            s_j = sc[j] + biases[j]
            m.append(jnp.max(s_j, axis=-1, keepdims=True))
            p = jnp.exp(s_j - m[j])
            den.append(jnp.sum(p, axis=-1, keepdims=True))
            p = p.astype(BF16)
            p2.append(jnp.concatenate([p[:Q_BLOCK], p[Q_BLOCK:]], axis=1))
        pv = [_dot(p2[j], v2[j]) for j in units]
        outs = []
        for j in units:
            inv = 1.0 / den[j]
            lse = m[j] + jnp.log(den[j])
            outs.append((pv[j] * jnp.where(head0, inv[:Q_BLOCK], inv[Q_BLOCK:]),
                         jnp.where(head0, lse[:Q_BLOCK], lse[Q_BLOCK:])))
        return outs

    n_trips = DSA_UNIT // Q_BLOCK // DSA_UNROLL
    staged = 0
    for g, (window, r) in enumerate(DSA_GROUPS):
        assert window // r == Q_BLOCK
        q_ref, k_ref, kp_ref, v_ref, vp_ref = ins[g]
        og_ref, lg_ref = out_scratch[2 * g:2 * g + 2]
        prev = Q_BLOCK * r

        if r <= DSA_STAGE:
            assert DSA_UNROLL % r == 0

            def trip(t, first, r=r, prev=prev, q_ref=q_ref, k_ref=k_ref, kp_ref=kp_ref, v_ref=v_ref,
                     vp_ref=vp_ref, og_ref=og_ref, lg_ref=lg_ref):
                qs, ks, vs, biases, rows = [], [], [], [], []
                for j in range(DSA_UNROLL):
                    sub, c = t * (DSA_UNROLL // r) + j // r, j % r
                    rows.append(pl.ds(sub * prev + c, Q_BLOCK, stride=r))
                    qs.append(q_ref[rows[j], :])
                    if first and sub == 0:
                        half = pl.ds(c, Q_BLOCK, stride=r)
                        ks.append(jnp.concatenate([kp_ref[half, :], k_ref[half, :]], axis=0))
                        vs.append(jnp.concatenate([vp_ref[half, :], v_ref[half, :]], axis=0))
                        biases.append(bias_ref[seq_start])
                    else:
                        both = pl.ds((sub - 1) * prev + c, 2 * Q_BLOCK, stride=r)
                        ks.append(k_ref[both, :])
                        vs.append(v_ref[both, :])
                        biases.append(bias_ref[0])
                for j, (o, lse) in enumerate(attend(qs, ks, vs, biases)):
                    og_ref[rows[j], :] = o
                    lg_ref[rows[j], :] = lse

            trip(0, True)

            def later_trip(t, carry, trip=trip):
                trip(t, False)
                return carry

            lax.fori_loop(1, n_trips, later_trip, 0)
        else:
            inner = r // DSA_STAGE
            assert inner <= DSA_STAGE and prev == DSA_UNIT and DSA_UNROLL == DSA_STAGE
            qs_ref, ks_ref, vs_ref, os_ref, ls_ref = stage_scratch[5 * staged:5 * staged + 5]
            staged += 1
            half = Q_BLOCK * inner
            for c in range(DSA_STAGE):
                coarse = pl.ds(c, half, stride=DSA_STAGE)
                qs_ref[c] = q_ref[coarse, :]
                ks_ref[c, :half] = kp_ref[coarse, :]
                ks_ref[c, half:] = k_ref[coarse, :]
                vs_ref[c, :half] = vp_ref[coarse, :]
                vs_ref[c, half:] = v_ref[coarse, :]

            def staged_trip(t, carry, inner=inner, qs_ref=qs_ref, ks_ref=ks_ref, vs_ref=vs_ref,
                            os_ref=os_ref, ls_ref=ls_ref):
                fine_q = pl.ds(t, Q_BLOCK, stride=inner)
                fine_k = pl.ds(t, 2 * Q_BLOCK, stride=inner)
                bias = bias_ref[seq_start]
                outs = attend([qs_ref.at[c][fine_q, :] for c in range(DSA_STAGE)],
                              [ks_ref.at[c][fine_k, :] for c in range(DSA_STAGE)],
                              [vs_ref.at[c][fine_k, :] for c in range(DSA_STAGE)],
                              [bias] * DSA_STAGE)
                for c, (o, lse) in enumerate(outs):
                    os_ref.at[c][fine_q, :] = o
                    ls_ref.at[c][fine_q, :] = lse
                return carry

            lax.fori_loop(0, n_trips, staged_trip, 0)
            for c in range(DSA_STAGE):
                coarse = pl.ds(c, half, stride=DSA_STAGE)
                og_ref[coarse, :] = os_ref[c]
                lg_ref[coarse, :] = ls_ref[c]

    lse = [out_scratch[2 * g + 1][...] for g in range(n_g)]
    top = functools.reduce(jnp.maximum, lse)
    e = [jnp.exp(l - top) for l in lse]
    num = sum(e[g] * out_scratch[2 * g][...] for g in range(n_g))
    o_ref[...] = (num / sum(e)).astype(o_ref.dtype)


def _dilated(q, k, v):
    s = q.shape[0]
    n_pairs = DSA_OUT_W // LANES
    in_specs, stage_scratch = [], []
    for g, (_, r) in enumerate(DSA_GROUPS):
        prev = Q_BLOCK * r
        per_unit = DSA_UNIT // prev
        cur = pl.BlockSpec((DSA_UNIT, LANES), lambda i, p, g=g: (i, n_pairs * g + p))
        prv = pl.BlockSpec((prev, LANES),
                           lambda i, p, g=g, n=per_unit: (jnp.maximum(i * n - 1, 0), n_pairs * g + p))
        in_specs += [cur, cur, prv, cur, prv]
        if r > DSA_STAGE:
            rows = DSA_UNIT // DSA_STAGE
            stage_scratch += [pltpu.VMEM((DSA_STAGE, rows, LANES), F32),
                              pltpu.VMEM((DSA_STAGE, 2 * rows, LANES), F32),
                              pltpu.VMEM((DSA_STAGE, 2 * rows, LANES), F32),
                              pltpu.VMEM((DSA_STAGE, rows, LANES), F32),
                              pltpu.VMEM((DSA_STAGE, rows, LANES), F32)]
    scratch = ([pltpu.VMEM((2, 2 * Q_BLOCK, 2 * Q_BLOCK), F32)]
               + [pltpu.VMEM((DSA_UNIT, LANES), F32)] * (2 * len(DSA_GROUPS)) + stage_scratch)
    args = []
    for _ in DSA_GROUPS:
        args += [q, k, k, v, v]
    return pl.pallas_call(
        _dsa_kernel,
        out_shape=jax.ShapeDtypeStruct((s, DSA_OUT_W), BF16),
        grid=(s // DSA_UNIT, n_pairs),
        in_specs=in_specs,
        out_specs=pl.BlockSpec((DSA_UNIT, LANES), lambda i, p: (i, p)),
        scratch_shapes=scratch,
        compiler_params=_params(2),
        name="dilated",
    )(*args)


def _merge_kernel(x_ref, g_ref, wg32_ref, bg_ref, ya_ref, yb_ref, yc_ref, wa32_ref, wb32_ref, wc32_ref,
                  wo32_ref, o_ref, wg_ref, wa_ref, wb_ref, wc_ref, wo_ref):
    for src, dst in ((wg32_ref, wg_ref), (wa32_ref, wa_ref), (wb32_ref, wb_ref), (wc32_ref, wc_ref),
                     (wo32_ref, wo_ref)):
        _cast_once(src, dst)
    x = x_ref[...]
    d = x.shape[1]
    h = _rms(x, g_ref[...]).astype(BF16)
    merged = None
    for i, (y_ref, w_ref) in enumerate(((ya_ref, wa_ref), (yb_ref, wb_ref), (yc_ref, wc_ref))):
        logits = _dot(h, wg_ref[:, i * d:(i + 1) * d]) + bg_ref[:, i * d:(i + 1) * d]
        term = _dot(y_ref[...], w_ref[...]) / (1.0 + jnp.exp(-logits))
        merged = term if merged is None else merged + term
    o_ref[...] = x + _dot(merged.astype(BF16), wo_ref[...])


def _merge(x, g, wg, bg, ya, yb, yc, wa, wb, wc, wo):
    s, d = x.shape
    row = lambda i: (i, 0)
    tile = lambda w: pl.BlockSpec((ROW_TILE, w), row)
    return pl.pallas_call(
        _merge_kernel,
        out_shape=jax.ShapeDtypeStruct((s, d), F32),
        grid=(s // ROW_TILE,),
        in_specs=[tile(d), _resident((1, d)), _resident(wg.shape), _resident(bg.shape),
                  tile(SB_W), tile(DSA_OUT_W), tile(MEM_W),
                  _resident(wa.shape), _resident(wb.shape), _resident(wc.shape), _resident(wo.shape)],
        out_specs=tile(d),
        scratch_shapes=[pltpu.VMEM(w.shape, BF16) for w in (wg, wa, wb, wc, wo)],
        compiler_params=_params(1),
        name="merge",
    )(x, g, wg, bg, ya, yb, yc, wa, wb, wc, wo)


def _rope_tables(s):
    half = HEAD_DIM // 2
    inv_freq = jnp.power(ROPE_THETA, -jnp.arange(half, dtype=F32) / half)
    ang = inv_freq[:, None] * jnp.arange(s).astype(F32)[None, :]
    return jnp.cos(ang), jnp.sin(ang)


def _layer(x, mem, p):
    s = x.shape[0]
    bf = lambda w: w.astype(BF16)
    vec = lambda v: v.reshape(1, -1)
    heads = lambda v, n: jnp.tile(v, n).reshape(1, -1)
    cos, sin = _rope_tables(s)

    x = _ffn(x, vec(p["ffn1_norm"]), bf(p["ffn1_w1"]), bf(p["ffn1_w3"]), bf(p["ffn1_w2"]))
    km, vm = _memkv(mem, vec(p["mem_norm"]), bf(p["w_mem_kv"]), heads(p["kn_mem"], MEM_HEADS))
    n_dsa = DSA_W // HEAD_DIM
    qa, ka, va, qb, kb, vb, yc = _proj(
        x, vec(p["mix_norm"]), p["w_in"], cos, sin,
        heads(p["qn_dsa"], n_dsa), heads(p["kn_dsa"], n_dsa), heads(p["qn_mem"], MEM_HEADS), km, vm)
    ya = _stickbreak(qa, ka, va)
    yb = _dilated(qb, kb, vb)
    x = _merge(x, vec(p["mix_norm"]), p["w_gate"], vec(p["b_gate"]), ya, yb, yc,
               p["w_branch_sb"], p["w_branch_dsa"], p["w_branch_mem"], p["w_out"])
    return _ffn(x, vec(p["ffn2_norm"]), bf(p["ffn2_w1"]), bf(p["ffn2_w3"]), bf(p["ffn2_w2"]))


_PARAM_NAMES = ("ffn1_norm", "ffn1_w1", "ffn1_w3", "ffn1_w2", "mix_norm", "mem_norm", "w_in", "w_mem_kv",
                "qn_dsa", "kn_dsa", "qn_mem", "kn_mem", "w_branch_sb", "w_branch_dsa", "w_branch_mem",
                "w_gate", "b_gate", "w_out", "ffn2_norm", "ffn2_w1", "ffn2_w3", "ffn2_w2")


def kernel(x, mem, ffn1_norm, ffn1_w1, ffn1_w3, ffn1_w2, mix_norm, mem_norm, w_in, w_mem_kv, qn_dsa, kn_dsa, qn_mem, kn_mem, w_branch_sb, w_branch_dsa, w_branch_mem, w_gate, b_gate, w_out, ffn2_norm, ffn2_w1, ffn2_w3, ffn2_w2):
    stacked = dict(zip(_PARAM_NAMES, (ffn1_norm, ffn1_w1, ffn1_w3, ffn1_w2, mix_norm, mem_norm, w_in,
                                      w_mem_kv, qn_dsa, kn_dsa, qn_mem, kn_mem, w_branch_sb, w_branch_dsa,
                                      w_branch_mem, w_gate, b_gate, w_out, ffn2_norm, ffn2_w1, ffn2_w3,
                                      ffn2_w2)))
    depth = ffn1_norm.shape[0]
    outs = []
    for b in range(x.shape[0]):
        xb = x[b]
        for l in range(depth):
            xb = _layer(xb, mem[b], {k: v[l] for k, v in stacked.items()})
        outs.append(xb)
    return jnp.stack(outs)
```

```python
import functools

import jax
import jax.numpy as jnp
from jax import lax
from jax.experimental import pallas as pl
from jax.experimental.pallas import tpu as pltpu

F32 = jnp.float32
BF16 = jnp.bfloat16

HEAD_DIM = 64
SB_HEADS = 8
DSA_GROUPS = ((128, 1), (512, 4), (2048, 16))
DSA_HEADS_PER_GROUP = 4
MEM_HEADS = 4
ROPE_THETA = 10000.0
NORM_EPS = 1e-6
Q_BLOCK = 128
SB_W = SB_HEADS * HEAD_DIM
DSA_W = DSA_HEADS_PER_GROUP * len(DSA_GROUPS) * HEAD_DIM
DSA_OUT_W = DSA_HEADS_PER_GROUP * HEAD_DIM
MEM_W = MEM_HEADS * HEAD_DIM
QK_SCALE = HEAD_DIM ** -0.5

LANES = 128
MXU_WIDTH = 256
DSA_UNIT = Q_BLOCK * max(r for _, r in DSA_GROUPS)
DSA_UNROLL = 4
DSA_STAGE = 4
ROW_TILE = 512
VMEM_LIMIT = 56 * 1024 * 1024
CAST_CHUNK_ROWS = 128
SB_LOG_CUTOFF = -104.0
SB_MASKED_LOGIT = -1e30
SB_FUSED_BLOCKS = 3
SB_QBLOCKS_PER_STEP = 2
SB_LAST_BLOCK_ROWS = 48


def _resident(shape):
    zeros = (0,) * len(shape)
    return pl.BlockSpec(shape, lambda *_: zeros, pipeline_mode=pl.Buffered(1))


def _params(n_axes):
    return pltpu.CompilerParams(dimension_semantics=("arbitrary",) * n_axes,
                                vmem_limit_bytes=VMEM_LIMIT)


def _rms(x, g):
    return x * lax.rsqrt(jnp.mean(x * x, axis=-1, keepdims=True) + NORM_EPS) * g


def _dot(a, b):
    return jnp.dot(a, b, preferred_element_type=F32)


def _dot_nt(a, b):
    return lax.dot_general(a, b, (((1,), (1,)), ((), ())), preferred_element_type=F32)


def _split_bf16(x):
    hi = x.astype(BF16)
    lo = (x - hi.astype(F32)).astype(BF16)
    return hi, lo


def _head_norm(x, g):
    n = x.shape[-1]
    w = min(n, MXU_WIDTH)
    r = lax.broadcasted_iota(jnp.int32, (w, w), 0) // HEAD_DIM
    c = lax.broadcasted_iota(jnp.int32, (w, w), 1) // HEAD_DIM
    bd = jnp.where(r == c, 1.0, 0.0).astype(BF16)
    hi, lo = _split_bf16(x * x)
    ms = jnp.concatenate([_dot(hi[:, j:j + w], bd) + _dot(lo[:, j:j + w], bd) for j in range(0, n, w)],
                         axis=1) * (1.0 / HEAD_DIM)
    return x * lax.rsqrt(ms + NORM_EPS) * g


def _ffn_kernel(x_ref, g_ref, w1_ref, w3_ref, w2_ref, o_ref):
    x = x_ref[...]
    h = _rms(x, g_ref[...]).astype(BF16)
    a = _dot(h, w1_ref[...])
    b = _dot(h, w3_ref[...])
    act = (a * b / (1.0 + jnp.exp(-a))).astype(BF16)
    o_ref[...] = x + 0.5 * _dot(act, w2_ref[...])


def _ffn(x, g, w1, w3, w2):
    s, d = x.shape
    f = w1.shape[1]
    row = lambda i: (i, 0)
    return pl.pallas_call(
        _ffn_kernel,
        out_shape=jax.ShapeDtypeStruct((s, d), F32),
        grid=(s // ROW_TILE,),
        in_specs=[pl.BlockSpec((ROW_TILE, d), row), _resident((1, d)),
                  _resident((d, f)), _resident((d, f)), _resident((f, d))],
        out_specs=pl.BlockSpec((ROW_TILE, d), row),
        compiler_params=_params(1),
        name="ffn",
    )(x, g, w1, w3, w2)


def _memkv_kernel(mem_ref, g_ref, w_ref, kn_ref, k_ref, v_ref):
    h = _rms(mem_ref[...], g_ref[...]).astype(BF16)
    kv = _dot(h, w_ref[...])
    k_ref[...] = _head_norm(kv[:, :MEM_W], kn_ref[...]).astype(BF16)
    v_ref[...] = kv[:, MEM_W:].astype(BF16)


def _memkv(mem, g, w, kn):
    m = mem.shape[0]
    out = jax.ShapeDtypeStruct((m, MEM_W), BF16)
    return pl.pallas_call(_memkv_kernel, out_shape=(out, out), name="memkv",
                          compiler_params=pltpu.CompilerParams(vmem_limit_bytes=VMEM_LIMIT),
                          )(mem, g, w, kn)


def _rope(x, cos, sin_signed):
    lane = lax.broadcasted_iota(jnp.int32, (x.shape[0], LANES), 1)
    first_half = (lane % HEAD_DIM) < (HEAD_DIM // 2)
    out = []
    for j in range(x.shape[1] // LANES):
        xs = x[:, j * LANES:(j + 1) * LANES]
        partner = jnp.where(first_half, pltpu.roll(xs, LANES - HEAD_DIM // 2, 1),
                            pltpu.roll(xs, HEAD_DIM // 2, 1))
        out.append(xs * cos + partner * sin_signed)
    return jnp.concatenate(out, axis=1)


def _rope_lanes(cos_half, sin_half):
    half = HEAD_DIM // 2
    f = lax.broadcasted_iota(jnp.int32, (half, LANES), 0)
    lane = lax.broadcasted_iota(jnp.int32, (half, LANES), 1)
    hit = (lane % half) == f
    spread = jnp.where(hit, 1.0, 0.0).astype(BF16)
    signed = jnp.where(hit, jnp.where((lane % HEAD_DIM) < half, -1.0, 1.0), 0.0).astype(BF16)
    c_hi, c_lo = _split_bf16(cos_half)
    s_hi, s_lo = _split_bf16(sin_half)
    return _dot(c_hi, spread) + _dot(c_lo, spread), _dot(s_hi, signed) + _dot(s_lo, signed)


def _cast_once(src_ref, dst_ref):
    rows = src_ref.shape[0]
    chunk = min(rows, CAST_CHUNK_ROWS)

    @pl.when(pl.program_id(0) == 0)
    def _():
        def body(i, carry):
            r = pl.ds(pl.multiple_of(i * chunk, chunk), chunk)
            dst_ref[r, :] = src_ref[r, :].astype(BF16)
            return carry

        lax.fori_loop(0, rows // chunk, body, 0)


def _proj_kernel(x_ref, g_ref, w32_ref, cos_ref, sin_ref, qn_d_ref, kn_d_ref, qn_m_ref, km_ref, vm_ref,
                 qa_ref, ka_ref, va_ref, qb_ref, kb_ref, vb_ref, yc_ref, w_ref):
    _cast_once(w32_ref, w_ref)
    h = _rms(x_ref[...], g_ref[...]).astype(BF16)
    cos, sin_signed = _rope_lanes(cos_ref[...].T, sin_ref[...].T)

    def cols(lo, width):
        return _dot(h, w_ref[:, lo:lo + width])

    qa_ref[...] = (cols(0, SB_W) * QK_SCALE).astype(BF16)
    ka_ref[...] = cols(SB_W, SB_W).astype(BF16)
    va_ref[...] = cols(2 * SB_W, SB_W).astype(BF16)
    base = 3 * SB_W
    qb_ref[...] = _rope(_head_norm(cols(base, DSA_W), qn_d_ref[...]), cos, sin_signed) * QK_SCALE
    kb_ref[...] = _rope(_head_norm(cols(base + DSA_W, DSA_W), kn_d_ref[...]), cos, sin_signed)
    vb_ref[...] = cols(base + 2 * DSA_W, DSA_W)

    qc = _head_norm(cols(base + 3 * DSA_W, MEM_W), qn_m_ref[...]) * QK_SCALE
    lane = lax.broadcasted_iota(jnp.int32, (qc.shape[0], LANES), 1)
    for j in range(MEM_W // LANES):
        sl = slice(j * LANES, (j + 1) * LANES)
        qs, km, vm = qc[:, sl], km_ref[:, sl], vm_ref[:, sl]
        outs = []
        for half in range(2):
            in_head = (lane < HEAD_DIM) == (half == 0)
            sc = _dot_nt(jnp.where(in_head, qs, 0.0).astype(BF16), km)
            p = jnp.exp(sc - jnp.max(sc, axis=-1, keepdims=True))
            outs.append(_dot(p.astype(BF16), vm) / jnp.sum(p, axis=-1, keepdims=True))
        yc_ref[:, sl] = jnp.where(lane < HEAD_DIM, outs[0], outs[1]).astype(BF16)


def _proj(x, g, w_in, cos, sin_signed, qn_d, kn_d, qn_m, km, vm):
    s, d = x.shape
    row = lambda i: (i, 0)
    tile = lambda w: pl.BlockSpec((ROW_TILE, w), row)
    sb = jax.ShapeDtypeStruct((s, SB_W), BF16)
    dsa = jax.ShapeDtypeStruct((s, DSA_W), F32)
    table = pl.BlockSpec((HEAD_DIM // 2, ROW_TILE), lambda i: (0, i))
    return pl.pallas_call(
        _proj_kernel,
        out_shape=(sb, sb, sb, dsa, dsa, dsa, jax.ShapeDtypeStruct((s, MEM_W), BF16)),
        grid=(s // ROW_TILE,),
        in_specs=[tile(d), _resident((1, d)), _resident(w_in.shape), table, table,
                  _resident((1, DSA_W)), _resident((1, DSA_W)), _resident((1, MEM_W)),
                  _resident(km.shape), _resident(vm.shape)],
        out_specs=(tile(SB_W), tile(SB_W), tile(SB_W), tile(DSA_W), tile(DSA_W), tile(DSA_W),
                   tile(MEM_W)),
        scratch_shapes=[pltpu.VMEM(w_in.shape, BF16)],
        compiler_params=_params(1),
        name="proj",
    )(x, g, w_in, cos, sin_signed, qn_d, kn_d, qn_m, km, vm)


def _sb_kernel(q_ref, k_ref, v_ref, o_ref, q2_ref, carry_ref, acc_ref):
    n_slots = SB_QBLOCKS_PER_STEP
    first_qb = pl.program_id(0) * n_slots
    n_pairs = SB_W // LANES
    lane = lax.broadcasted_iota(jnp.int32, (2 * Q_BLOCK, LANES), 1)
    row = lax.broadcasted_iota(jnp.int32, (2 * Q_BLOCK, LANES), 0)
    own_lanes = (lane < HEAD_DIM) == (row < Q_BLOCK)
    before = lane < (row % Q_BLOCK)
    j = lax.broadcasted_iota(jnp.int32, (2 * Q_BLOCK, 2 * Q_BLOCK), 0) % Q_BLOCK
    s = lax.broadcasted_iota(jnp.int32, (2 * Q_BLOCK, 2 * Q_BLOCK), 1)
    suffix = jnp.where((s >= Q_BLOCK) | (j >= s), 1.0, 0.0).astype(BF16)

    def stack(x):
        return jnp.where(own_lanes, jnp.concatenate([x, x], axis=0), jnp.zeros((), x.dtype))

    pairs = range(n_pairs)
    sls = [slice(p * LANES, (p + 1) * LANES) for p in pairs]
    for slot in range(n_slots):
        for p in pairs:
            q2_ref[slot, p] = stack(q_ref[slot * Q_BLOCK:(slot + 1) * Q_BLOCK, sls[p]])

    def walk(jobs, from_diagonal, last_rows=Q_BLOCK):
        partial = last_rows < Q_BLOCK
        assert not partial or (from_diagonal and all(len(kbs) > 1 for _, kbs in jobs))
        chains = [(i, b) for i, (_, kbs) in enumerate(jobs) for b in range(len(kbs))]

        def is_partial(i, b):
            return partial and b == len(jobs[i][1]) - 1

        def visiting(x, i, b):
            if is_partial(i, b):
                return jnp.concatenate([x[:last_rows], x[Q_BLOCK:Q_BLOCK + last_rows]], axis=0)
            return x

        rows = {(i, b): pl.ds(pl.multiple_of(jobs[i][1][b] * Q_BLOCK, Q_BLOCK), Q_BLOCK) for i, b in chains}
        z = {(i, b, p): _dot_nt(visiting(q2_ref[jobs[i][0], p], i, b), k_ref[rows[i, b], sls[p]])
             for i, b in chains for p in pairs}
        split = {}
        for i, b in chains:
            for p in pairs:
                if from_diagonal and b == 0:
                    z[i, b, p] = jnp.where(before, z[i, b, p], SB_MASKED_LOGIT)
                nz = -z[i, b, p]
                lf = jnp.minimum(nz, 0.0) - jnp.log(1.0 + jnp.exp(jnp.minimum(z[i, b, p], nz)))
                split[i, b, p] = jnp.concatenate(_split_bf16(lf), axis=1)
        sums = {(i, b, p): _dot(split[i, b, p], suffix) for i, b in chains for p in pairs}
        w2, w_last, tops = {}, {}, []
        for i, (slot, kbs) in enumerate(jobs):
            top = None
            for p in pairs:
                carry = None if from_diagonal else carry_ref[slot, p]
                ws = []
                for b in range(len(kbs)):
                    logw = z[i, b, p] + sums[i, b, p][:, :Q_BLOCK]
                    total = sums[i, b, p][:, Q_BLOCK:]
                    if carry is not None:
                        logw = logw + visiting(carry, i, b)
                        total = total + visiting(carry, i, b)
                    w = jnp.exp(logw).astype(BF16)
                    half = w.shape[0] // 2
                    if is_partial(i, b):
                        w_last[i, p] = jnp.concatenate([w[:half], w[half:]], axis=1)
                        carry = jnp.concatenate([total[:half], carry[last_rows:Q_BLOCK],
                                                 total[half:], carry[Q_BLOCK + last_rows:]], axis=0)
                    else:
                        ws += [w[:half], w[half:]]
                        carry = total
                w2[i, p] = jnp.concatenate(ws, axis=1)
                if not partial:
                    carry_ref[slot, p] = carry
                top = carry if top is None else jnp.maximum(top, carry)
            tops.append(top)
        for i, (slot, kbs) in enumerate(jobs):
            n_full = len(kbs) - 1 if partial else len(kbs)
            for p in pairs:
                v2 = jnp.concatenate([stack(v_ref[rows[i, b], sls[p]]) for b in range(n_full)], axis=0)
                pv = _dot(w2[i, p], v2)
                if partial:
                    extra = _dot(w_last[i, p], stack(v_ref[rows[i, n_full], sls[p]]))
                    pv = jnp.concatenate([pv[:last_rows] + extra, pv[last_rows:]], axis=0)
                acc_ref[slot, p] = pv if from_diagonal else acc_ref[slot, p] + pv
        return [jnp.max(top) for top in tops]

    def cond(state):
        kb, top = state
        return jnp.logical_and(kb >= 0, top > SB_LOG_CUTOFF)

    def general(slot):
        def run():
            def body(state):
                kb, _ = state
                return kb - 1, walk([(slot, [kb])], False)[0]

            qb = first_qb + slot
            lax.while_loop(cond, body, (qb - 1, walk([(slot, [qb])], True)[0]))
        return run

    n_fused = SB_FUSED_BLOCKS
    fused_tops = lax.cond(
        first_qb >= n_fused - 1,
        lambda: tuple(walk([(slot, [first_qb + slot - b for b in range(n_fused)]) for slot in range(n_slots)],
                           True, SB_LAST_BLOCK_ROWS)),
        lambda: (jnp.float32(jnp.inf),) * n_slots)
    for slot in range(n_slots):
        lax.cond(fused_tops[slot] > SB_LOG_CUTOFF, general(slot), lambda: None)
        for p in pairs:
            o_ref[slot * Q_BLOCK:(slot + 1) * Q_BLOCK, sls[p]] = acc_ref[slot, p].astype(BF16)


def _stickbreak(q, k, v):
    s, w = q.shape
    n_pairs = w // LANES
    n_slots = SB_QBLOCKS_PER_STEP
    tile = pl.BlockSpec((n_slots * Q_BLOCK, w), lambda i: (i, 0))
    return pl.pallas_call(
        _sb_kernel,
        out_shape=jax.ShapeDtypeStruct((s, w), BF16),
        grid=(s // (n_slots * Q_BLOCK),),
        in_specs=[tile, _resident((s, w)), _resident((s, w))],
        out_specs=tile,
        scratch_shapes=[pltpu.VMEM((n_slots, n_pairs, 2 * Q_BLOCK, LANES), BF16),
                        pltpu.VMEM((n_slots, n_pairs, 2 * Q_BLOCK, LANES), F32),
                        pltpu.VMEM((n_slots, n_pairs, Q_BLOCK, LANES), F32)],
        compiler_params=_params(1),
        name="stickbreak",
    )(q, k, v)


def _dsa_kernel(*refs):
    n_g = len(DSA_GROUPS)
    ins = [refs[5 * g:5 * g + 5] for g in range(n_g)]
    o_ref = refs[5 * n_g]
    bias_ref = refs[5 * n_g + 1]
    out_scratch = refs[5 * n_g + 2:5 * n_g + 2 + 2 * n_g]
    stage_scratch = refs[5 * n_g + 2 + 2 * n_g:]
    step = pl.program_id(0)
    lane = lax.broadcasted_iota(jnp.int32, (Q_BLOCK, LANES), 1)
    head0 = lane < HEAD_DIM

    def own_lanes(rows):
        ln = lax.broadcasted_iota(jnp.int32, (rows, LANES), 1)
        rw = lax.broadcasted_iota(jnp.int32, (rows, LANES), 0)
        return (ln < HEAD_DIM) == (rw < rows // 2)

    own_q, own_v = own_lanes(2 * Q_BLOCK), own_lanes(4 * Q_BLOCK)
    qi = lax.broadcasted_iota(jnp.int32, (2 * Q_BLOCK, 2 * Q_BLOCK), 0) % Q_BLOCK
    kj = lax.broadcasted_iota(jnp.int32, (2 * Q_BLOCK, 2 * Q_BLOCK), 1)
    dist = Q_BLOCK + qi - kj
    in_band = (dist >= 0) & (dist <= Q_BLOCK)
    bias_ref[0] = jnp.where(in_band, 0.0, -jnp.inf)
    bias_ref[1] = jnp.where(in_band & (kj >= Q_BLOCK), 0.0, -jnp.inf)
    seq_start = jnp.where(step == 0, 1, 0)

    def attend(qs, ks, vs, biases):
        units = range(len(qs))
        sc, v2 = [], []
        for j in units:
            q2 = jnp.where(own_q, jnp.concatenate([qs[j], qs[j]], axis=0), 0.0).astype(BF16)
            v = vs[j].astype(BF16)
            v2.append(jnp.where(own_v, jnp.concatenate([v, v], axis=0), jnp.zeros((), BF16)))
            sc.append(_dot_nt(q2, ks[j].astype(BF16)))
        p2, m, den = [], [], []
        for j in units:
            s_j = sc[j] + biases[j]
            m.append(jnp.max(s_j, axis=-1, keepdims=True))
            p = jnp.exp(s_j - m[j])
            den.append(jnp.sum(p, axis=-1, keepdims=True))
            p = p.astype(BF16)
            p2.append(jnp.concatenate([p[:Q_BLOCK], p[Q_BLOCK:]], axis=1))
        pv = [_dot(p2[j], v2[j]) for j in units]
        outs = []
        for j in units:
            inv = 1.0 / den[j]
            lse = m[j] + jnp.log(den[j])
            outs.append((pv[j] * jnp.where(head0, inv[:Q_BLOCK], inv[Q_BLOCK:]),
                         jnp.where(head0, lse[:Q_BLOCK], lse[Q_BLOCK:])))
        return outs

    n_trips = DSA_UNIT // Q_BLOCK // DSA_UNROLL
    staged = 0
    for g, (window, r) in enumerate(DSA_GROUPS):
        assert window // r == Q_BLOCK
        q_ref, k_ref, kp_ref, v_ref, vp_ref = ins[g]
        og_ref, lg_ref = out_scratch[2 * g:2 * g + 2]
        prev = Q_BLOCK * r

        if r <= DSA_STAGE:
            assert DSA_UNROLL % r == 0

            def trip(t, first, r=r, prev=prev, q_ref=q_ref, k_ref=k_ref, kp_ref=kp_ref, v_ref=v_ref,
                     vp_ref=vp_ref, og_ref=og_ref, lg_ref=lg_ref):
                qs, ks, vs, biases, rows = [], [], [], [], []
                for j in range(DSA_UNROLL):
                    sub, c = t * (DSA_UNROLL // r) + j // r, j % r
                    rows.append(pl.ds(sub * prev + c, Q_BLOCK, stride=r))
                    qs.append(q_ref[rows[j], :])
                    if first and sub == 0:
                        half = pl.ds(c, Q_BLOCK, stride=r)
                        ks.append(jnp.concatenate([kp_ref[half, :], k_ref[half, :]], axis=0))
                        vs.append(jnp.concatenate([vp_ref[half, :], v_ref[half, :]], axis=0))
                        biases.append(bias_ref[seq_start])
                    else:
                        both = pl.ds((sub - 1) * prev + c, 2 * Q_BLOCK, stride=r)
                        ks.append(k_ref[both, :])
                        vs.append(v_ref[both, :])
                        biases.append(bias_ref[0])
                for j, (o, lse) in enumerate(attend(qs, ks, vs, biases)):
                    og_ref[rows[j], :] = o
                    lg_ref[rows[j], :] = lse

            trip(0, True)

            def later_trip(t, carry, trip=trip):
                trip(t, False)
                return carry

            lax.fori_loop(1, n_trips, later_trip, 0)
        else:
            inner = r // DSA_STAGE
            assert inner <= DSA_STAGE and prev == DSA_UNIT and DSA_UNROLL == DSA_STAGE
            qs_ref, ks_ref, vs_ref, os_ref, ls_ref = stage_scratch[5 * staged:5 * staged + 5]
            staged += 1
            half = Q_BLOCK * inner
            for c in range(DSA_STAGE):
                coarse = pl.ds(c, half, stride=DSA_STAGE)
                qs_ref[c] = q_ref[coarse, :]
                ks_ref[c, :half] = kp_ref[coarse, :]
                ks_ref[c, half:] = k_ref[coarse, :]
                vs_ref[c, :half] = vp_ref[coarse, :]
                vs_ref[c, half:] = v_ref[coarse, :]

            def staged_trip(t, carry, inner=inner, qs_ref=qs_ref, ks_ref=ks_ref, vs_ref=vs_ref,
                            os_ref=os_ref, ls_ref=ls_ref):
                fine_q = pl.ds(t, Q_BLOCK, stride=inner)
                fine_k = pl.ds(t, 2 * Q_BLOCK, stride=inner)
                bias = bias_ref[seq_start]
                outs = attend([qs_ref.at[c][fine_q, :] for c in range(DSA_STAGE)],
                              [ks_ref.at[c][fine_k, :] for c in range(DSA_STAGE)],
                              [vs_ref.at[c][fine_k, :] for c in range(DSA_STAGE)],
                              [bias] * DSA_STAGE)
                for c, (o, lse) in enumerate(outs):
                    os_ref.at[c][fine_q, :] = o
                    ls_ref.at[c][fine_q, :] = lse
                return carry

            lax.fori_loop(0, n_trips, staged_trip, 0)
            for c in range(DSA_STAGE):
                coarse = pl.ds(c, half, stride=DSA_STAGE)
                og_ref[coarse, :] = os_ref[c]
                lg_ref[coarse, :] = ls_ref[c]

    lse = [out_scratch[2 * g + 1][...] for g in range(n_g)]
    top = functools.reduce(jnp.maximum, lse)
    e = [jnp.exp(l - top) for l in lse]
    num = sum(e[g] * out_scratch[2 * g][...] for g in range(n_g))
    o_ref[...] = (num / sum(e)).astype(o_ref.dtype)


def _dilated(q, k, v):
    s = q.shape[0]
    n_pairs = DSA_OUT_W // LANES
    in_specs, stage_scratch = [], []
    for g, (_, r) in enumerate(DSA_GROUPS):
        prev = Q_BLOCK * r
        per_unit = DSA_UNIT // prev
        cur = pl.BlockSpec((DSA_UNIT, LANES), lambda i, p, g=g: (i, n_pairs * g + p))
        prv = pl.BlockSpec((prev, LANES),
                           lambda i, p, g=g, n=per_unit: (jnp.maximum(i * n - 1, 0), n_pairs * g + p))
        in_specs += [cur, cur, prv, cur, prv]
        if r > DSA_STAGE:
            rows = DSA_UNIT // DSA_STAGE
            stage_scratch += [pltpu.VMEM((DSA_STAGE, rows, LANES), F32),
                              pltpu.VMEM((DSA_STAGE, 2 * rows, LANES), F32),
                              pltpu.VMEM((DSA_STAGE, 2 * rows, LANES), F32),
                              pltpu.VMEM((DSA_STAGE, rows, LANES), F32),
                              pltpu.VMEM((DSA_STAGE, rows, LANES), F32)]
    scratch = ([pltpu.VMEM((2, 2 * Q_BLOCK, 2 * Q_BLOCK), F32)]
               + [pltpu.VMEM((DSA_UNIT, LANES), F32)] * (2 * len(DSA_GROUPS)) + stage_scratch)
    args = []
    for _ in DSA_GROUPS:
        args += [q, k, k, v, v]
    return pl.pallas_call(
        _dsa_kernel,
        out_shape=jax.ShapeDtypeStruct((s, DSA_OUT_W), BF16),
        grid=(s // DSA_UNIT, n_pairs),
        in_specs=in_specs,
        out_specs=pl.BlockSpec((DSA_UNIT, LANES), lambda i, p: (i, p)),
        scratch_shapes=scratch,
        compiler_params=_params(2),
        name="dilated",
    )(*args)


def _merge_kernel(x_ref, g_ref, wg32_ref, bg_ref, ya_ref, yb_ref, yc_ref, wa32_ref, wb32_ref, wc32_ref,
                  wo32_ref, o_ref, wg_ref, wa_ref, wb_ref, wc_ref, wo_ref):
    for src, dst in ((wg32_ref, wg_ref), (wa32_ref, wa_ref), (wb32_ref, wb_ref), (wc32_ref, wc_ref),
                     (wo32_ref, wo_ref)):
        _cast_once(src, dst)
    x = x_ref[...]
    d = x.shape[1]
    h = _rms(x, g_ref[...]).astype(BF16)
    merged = None
    for i, (y_ref, w_ref) in enumerate(((ya_ref, wa_ref), (yb_ref, wb_ref), (yc_ref, wc_ref))):
        logits = _dot(h, wg_ref[:, i * d:(i + 1) * d]) + bg_ref[:, i * d:(i + 1) * d]
        term = _dot(y_ref[...], w_ref[...]) / (1.0 + jnp.exp(-logits))
        merged = term if merged is None else merged + term
    o_ref[...] = x + _dot(merged.astype(BF16), wo_ref[...])


def _merge(x, g, wg, bg, ya, yb, yc, wa, wb, wc, wo):
    s, d = x.shape
    row = lambda i: (i, 0)
    tile = lambda w: pl.BlockSpec((ROW_TILE, w), row)
    return pl.pallas_call(
        _merge_kernel,
        out_shape=jax.ShapeDtypeStruct((s, d), F32),
        grid=(s // ROW_TILE,),
        in_specs=[tile(d), _resident((1, d)), _resident(wg.shape), _resident(bg.shape),
                  tile(SB_W), tile(DSA_OUT_W), tile(MEM_W),
                  _resident(wa.shape), _resident(wb.shape), _resident(wc.shape), _resident(wo.shape)],
        out_specs=tile(d),
        scratch_shapes=[pltpu.VMEM(w.shape, BF16) for w in (wg, wa, wb, wc, wo)],
        compiler_params=_params(1),
        name="merge",
    )(x, g, wg, bg, ya, yb, yc, wa, wb, wc, wo)


def _tail_kernel(q_ref, kprev_ref, kcur_ref, vprev_ref, vcur_ref, k_hbm, v_hbm,
                 x_ref, gm_ref, wg_ref, bg_ref, yb_ref, yc_ref, wa_ref, wb_ref, wc_ref, wo_ref,
                 gf_ref, w1_ref, w3_ref, w2_ref, o_ref,
                 q2_ref, carry_ref, acc_ref, ya_ref, kblk_ref, vblk_ref, sem):
    n_slots = SB_QBLOCKS_PER_STEP
    step = pl.program_id(0)
    n_tiles = pl.num_programs(0) - 1
    first_qb = step * n_slots
    n_pairs = SB_W // LANES
    lane = lax.broadcasted_iota(jnp.int32, (2 * Q_BLOCK, LANES), 1)
    row = lax.broadcasted_iota(jnp.int32, (2 * Q_BLOCK, LANES), 0)
    own_lanes = (lane < HEAD_DIM) == (row < Q_BLOCK)
    before = lane < (row % Q_BLOCK)
    j = lax.broadcasted_iota(jnp.int32, (2 * Q_BLOCK, 2 * Q_BLOCK), 0) % Q_BLOCK
    s = lax.broadcasted_iota(jnp.int32, (2 * Q_BLOCK, 2 * Q_BLOCK), 1)
    suffix = jnp.where((s >= Q_BLOCK) | (j >= s), 1.0, 0.0).astype(BF16)

    def stack(x):
        return jnp.where(own_lanes, jnp.concatenate([x, x], axis=0), jnp.zeros((), x.dtype))

    pairs = range(n_pairs)
    sls = [slice(p * LANES, (p + 1) * LANES) for p in pairs]
    for slot in range(n_slots):
        for p in pairs:
            q2_ref[slot, p] = stack(q_ref[slot * Q_BLOCK:(slot + 1) * Q_BLOCK, sls[p]])

    def kv_block(handle, p):
        if handle is None:
            return kblk_ref[:, sls[p]], vblk_ref[:, sls[p]]
        k_win, v_win = (kprev_ref, vprev_ref) if handle < n_slots else (kcur_ref, vcur_ref)
        rows = slice((handle % n_slots) * Q_BLOCK, (handle % n_slots + 1) * Q_BLOCK)
        return k_win[rows, sls[p]], v_win[rows, sls[p]]

    def walk(jobs, from_diagonal, last_rows=Q_BLOCK, between=(None, None, None)):
        partial = last_rows < Q_BLOCK
        assert not partial or (from_diagonal and all(len(kbs) > 1 for _, kbs in jobs))
        chains = [(i, b) for i, (_, kbs) in enumerate(jobs) for b in range(len(kbs))]

        def is_partial(i, b):
            return partial and b == len(jobs[i][1]) - 1

        def visiting(x, i, b):
            if is_partial(i, b):
                return jnp.concatenate([x[:last_rows], x[Q_BLOCK:Q_BLOCK + last_rows]], axis=0)
            return x

        z = {(i, b, p): _dot_nt(visiting(q2_ref[jobs[i][0], p], i, b), kv_block(jobs[i][1][b], p)[0])
             for i, b in chains for p in pairs}
        if between[0] is not None:
            between[0]()
        split = {}
        for i, b in chains:
            for p in pairs:
                if from_diagonal and b == 0:
                    z[i, b, p] = jnp.where(before, z[i, b, p], SB_MASKED_LOGIT)
                nz = -z[i, b, p]
                lf = jnp.minimum(nz, 0.0) - jnp.log(1.0 + jnp.exp(jnp.minimum(z[i, b, p], nz)))
                split[i, b, p] = jnp.concatenate(_split_bf16(lf), axis=1)
        sums = {(i, b, p): _dot(split[i, b, p], suffix) for i, b in chains for p in pairs}
        if between[1] is not None:
            between[1]()
        w2, w_last, tops = {}, {}, []
        for i, (slot, kbs) in enumerate(jobs):
            top = None
            for p in pairs:
                carry = None if from_diagonal else carry_ref[slot, p]
                ws = []
                for b in range(len(kbs)):
                    logw = z[i, b, p] + sums[i, b, p][:, :Q_BLOCK]
                    total = sums[i, b, p][:, Q_BLOCK:]
                    if carry is not None:
                        logw = logw + visiting(carry, i, b)
                        total = total + visiting(carry, i, b)
                    w = jnp.exp(logw).astype(BF16)
                    half = w.shape[0] // 2
                    if is_partial(i, b):
                        w_last[i, p] = jnp.concatenate([w[:half], w[half:]], axis=1)
                        carry = jnp.concatenate([total[:half], carry[last_rows:Q_BLOCK],
                                                 total[half:], carry[Q_BLOCK + last_rows:]], axis=0)
                    else:
                        ws += [w[:half], w[half:]]
                        carry = total
                w2[i, p] = jnp.concatenate(ws, axis=1)
                if not partial:
                    carry_ref[slot, p] = carry
                top = carry if top is None else jnp.maximum(top, carry)
            tops.append(top)
        for i, (slot, kbs) in enumerate(jobs):
            n_full = len(kbs) - 1 if partial else len(kbs)
            for p in pairs:
                v2 = jnp.concatenate([stack(kv_block(kbs[b], p)[1]) for b in range(n_full)], axis=0)
                pv = _dot(w2[i, p], v2)
                if partial:
                    extra = _dot(w_last[i, p], stack(kv_block(kbs[n_full], p)[1]))
                    pv = jnp.concatenate([pv[:last_rows] + extra, pv[last_rows:]], axis=0)
                acc_ref[slot, p] = pv if from_diagonal else acc_ref[slot, p] + pv
        if between[2] is not None:
            between[2]()
        return [jnp.max(top) for top in tops]

    dense = {}

    def merge_stage():
        x = x_ref[...]
        d = x.shape[1]
        h = _rms(x, gm_ref[...]).astype(BF16)
        ya = ya_ref[(step + 1) % 2]
        merged = None
        for i, (y, w_ref) in enumerate(((ya, wa_ref), (yb_ref[...], wb_ref), (yc_ref[...], wc_ref))):
            logits = _dot(h, wg_ref[:, i * d:(i + 1) * d]) + bg_ref[:, i * d:(i + 1) * d]
            term = _dot(y, w_ref[...]) / (1.0 + jnp.exp(-logits))
            merged = term if merged is None else merged + term
        dense["x"] = x + _dot(merged.astype(BF16), wo_ref[...])
        dense["h"] = _rms(dense["x"], gf_ref[...]).astype(BF16)

    def up_stage():
        a = _dot(dense["h"], w1_ref[...])
        b = _dot(dense["h"], w3_ref[...])
        dense["act"] = (a * b / (1.0 + jnp.exp(-a))).astype(BF16)

    def down_stage():
        o_ref[...] = dense["x"] + 0.5 * _dot(dense["act"], w2_ref[...])

    def fetch(kb):
        rows = pl.ds(pl.multiple_of(kb * Q_BLOCK, Q_BLOCK), Q_BLOCK)
        copies = [pltpu.make_async_copy(k_hbm.at[rows, :], kblk_ref, sem.at[0]),
                  pltpu.make_async_copy(v_hbm.at[rows, :], vblk_ref, sem.at[1])]
        for c in copies:
            c.start()
        for c in copies:
            c.wait()

    def cond(state):
        kb, top = state
        return jnp.logical_and(kb >= 0, top > SB_LOG_CUTOFF)

    def general(slot):
        def run():
            def visit(kb, from_diagonal):
                fetch(kb)
                return walk([(slot, [None])], from_diagonal)[0]

            def body(state):
                kb, _ = state
                return kb - 1, visit(kb, False)

            qb = first_qb + slot
            lax.while_loop(cond, body, (qb - 1, visit(qb, True)))
        return run

    n_fused = SB_FUSED_BLOCKS
    assert n_fused - 1 <= n_slots

    def usual():
        jobs = [(slot, [n_slots + slot - b for b in range(n_fused)]) for slot in range(n_slots)]
        return tuple(walk(jobs, True, SB_LAST_BLOCK_ROWS, (merge_stage, up_stage, down_stage)))

    def edge():
        @pl.when(step == n_tiles)
        def _():
            merge_stage()
            up_stage()
            down_stage()
        return (jnp.float32(jnp.inf),) * n_slots

    fused_tops = lax.cond(jnp.logical_and(step >= 1, step < n_tiles), usual, edge)
    for slot in range(n_slots):
        lax.cond(jnp.logical_and(fused_tops[slot] > SB_LOG_CUTOFF, step < n_tiles), general(slot), lambda: None)
        for p in pairs:
            ya_ref[step % 2, slot * Q_BLOCK:(slot + 1) * Q_BLOCK, sls[p]] = acc_ref[slot, p].astype(BF16)


def _tail(q, k, v, x, gm, wg, bg, yb, yc, wa, wb, wc, wo, gf, w1, w3, w2):
    s, w = q.shape
    d = x.shape[1]
    n_pairs = w // LANES
    n_slots = SB_QBLOCKS_PER_STEP
    rows = n_slots * Q_BLOCK
    n_tiles = s // rows
    this = lambda i: (jnp.minimum(i, n_tiles - 1), 0)
    last = lambda i: (jnp.maximum(i - 1, 0), 0)
    hbm = pl.BlockSpec(memory_space=pl.ANY)
    return pl.pallas_call(
        _tail_kernel,
        out_shape=jax.ShapeDtypeStruct((s, d), F32),
        grid=(n_tiles + 1,),
        in_specs=[pl.BlockSpec((rows, w), this), pl.BlockSpec((rows, w), last), pl.BlockSpec((rows, w), this),
                  pl.BlockSpec((rows, w), last), pl.BlockSpec((rows, w), this), hbm, hbm,
                  pl.BlockSpec((rows, d), last), _resident((1, d)), _resident(wg.shape), _resident(bg.shape),
                  pl.BlockSpec((rows, DSA_OUT_W), last), pl.BlockSpec((rows, MEM_W), last),
                  _resident(wa.shape), _resident(wb.shape), _resident(wc.shape), _resident(wo.shape),
                  _resident((1, d)), _resident(w1.shape), _resident(w3.shape), _resident(w2.shape)],
        out_specs=pl.BlockSpec((rows, d), last),
        scratch_shapes=[pltpu.VMEM((n_slots, n_pairs, 2 * Q_BLOCK, LANES), BF16),
                        pltpu.VMEM((n_slots, n_pairs, 2 * Q_BLOCK, LANES), F32),
                        pltpu.VMEM((n_slots, n_pairs, Q_BLOCK, LANES), F32),
                        pltpu.VMEM((2, rows, w), BF16),
                        pltpu.VMEM((Q_BLOCK, w), BF16), pltpu.VMEM((Q_BLOCK, w), BF16),
                        pltpu.SemaphoreType.DMA((2,))],
        compiler_params=_params(1),
        name="tail",
    )(q, k, k, v, v, k, v, x, gm, wg, bg, yb, yc, wa, wb, wc, wo, gf, w1, w3, w2)


def _rope_tables(s):
    half = HEAD_DIM // 2
    inv_freq = jnp.power(ROPE_THETA, -jnp.arange(half, dtype=F32) / half)
    ang = inv_freq[:, None] * jnp.arange(s).astype(F32)[None, :]
    return jnp.cos(ang), jnp.sin(ang)


def _layer(x, mem, p):
    s = x.shape[0]
    bf = lambda w: w.astype(BF16)
    vec = lambda v: v.reshape(1, -1)
    heads = lambda v, n: jnp.tile(v, n).reshape(1, -1)
    cos, sin = _rope_tables(s)

    x = _ffn(x, vec(p["ffn1_norm"]), bf(p["ffn1_w1"]), bf(p["ffn1_w3"]), bf(p["ffn1_w2"]))
    km, vm = _memkv(mem, vec(p["mem_norm"]), bf(p["w_mem_kv"]), heads(p["kn_mem"], MEM_HEADS))
    n_dsa = DSA_W // HEAD_DIM
    qa, ka, va, qb, kb, vb, yc = _proj(
        x, vec(p["mix_norm"]), p["w_in"], cos, sin,
        heads(p["qn_dsa"], n_dsa), heads(p["kn_dsa"], n_dsa), heads(p["qn_mem"], MEM_HEADS), km, vm)
    yb = _dilated(qb, kb, vb)
    return _tail(qa, ka, va, x, vec(p["mix_norm"]), bf(p["w_gate"]), vec(p["b_gate"]), yb, yc,
                 bf(p["w_branch_sb"]), bf(p["w_branch_dsa"]), bf(p["w_branch_mem"]), bf(p["w_out"]),
                 vec(p["ffn2_norm"]), bf(p["ffn2_w1"]), bf(p["ffn2_w3"]), bf(p["ffn2_w2"]))


_PARAM_NAMES = ("ffn1_norm", "ffn1_w1", "ffn1_w3", "ffn1_w2", "mix_norm", "mem_norm", "w_in", "w_mem_kv",
                "qn_dsa", "kn_dsa", "qn_mem", "kn_mem", "w_branch_sb", "w_branch_dsa", "w_branch_mem",
                "w_gate", "b_gate", "w_out", "ffn2_norm", "ffn2_w1", "ffn2_w3", "ffn2_w2")


def kernel(x, mem, ffn1_norm, ffn1_w1, ffn1_w3, ffn1_w2, mix_norm, mem_norm, w_in, w_mem_kv, qn_dsa, kn_dsa, qn_mem, kn_mem, w_branch_sb, w_branch_dsa, w_branch_mem, w_gate, b_gate, w_out, ffn2_norm, ffn2_w1, ffn2_w3, ffn2_w2):
    stacked = dict(zip(_PARAM_NAMES, (ffn1_norm, ffn1_w1, ffn1_w3, ffn1_w2, mix_norm, mem_norm, w_in,
                                      w_mem_kv, qn_dsa, kn_dsa, qn_mem, kn_mem, w_branch_sb, w_branch_dsa,
                                      w_branch_mem, w_gate, b_gate, w_out, ffn2_norm, ffn2_w1, ffn2_w3,
                                      ffn2_w2)))
    depth = ffn1_norm.shape[0]
    outs = []
    for b in range(x.shape[0]):
        xb = x[b]
        for l in range(depth):
            xb = _layer(xb, mem[b], {k: v[l] for k, v in stacked.items()})
        outs.append(xb)
    return jnp.stack(outs)
```

```python
import functools

import jax
import jax.numpy as jnp
from jax import lax
from jax.experimental import pallas as pl
from jax.experimental.pallas import tpu as pltpu

F32 = jnp.float32
BF16 = jnp.bfloat16

HEAD_DIM = 64
SB_HEADS = 8
DSA_GROUPS = ((128, 1), (512, 4), (2048, 16))
DSA_HEADS_PER_GROUP = 4
MEM_HEADS = 4
ROPE_THETA = 10000.0
NORM_EPS = 1e-6
Q_BLOCK = 128
SB_W = SB_HEADS * HEAD_DIM
DSA_W = DSA_HEADS_PER_GROUP * len(DSA_GROUPS) * HEAD_DIM
DSA_OUT_W = DSA_HEADS_PER_GROUP * HEAD_DIM
MEM_W = MEM_HEADS * HEAD_DIM
QK_SCALE = HEAD_DIM ** -0.5

LANES = 128
MXU_WIDTH = 256
DSA_UNIT = Q_BLOCK * max(r for _, r in DSA_GROUPS)
DSA_UNROLL = 8
DSA_STAGE = 4
ROW_TILE = 512
VMEM_LIMIT = 56 * 1024 * 1024
CAST_SLABS = 16
SB_LOG_CUTOFF = -104.0
SB_MASKED_LOGIT = -1e30
SB_FUSED_BLOCKS = 3
SB_QBLOCKS_PER_STEP = 2
SB_LAST_BLOCK_ROWS = 48


def _resident(shape):
    zeros = (0,) * len(shape)
    return pl.BlockSpec(shape, lambda *_: zeros, pipeline_mode=pl.Buffered(1))


def _params(n_axes):
    return pltpu.CompilerParams(dimension_semantics=("arbitrary",) * n_axes,
                                vmem_limit_bytes=VMEM_LIMIT)


def _rms(x, g):
    return x * lax.rsqrt(jnp.mean(x * x, axis=-1, keepdims=True) + NORM_EPS) * g


def _dot(a, b):
    return jnp.dot(a, b, preferred_element_type=F32)


def _dot_nt(a, b):
    return lax.dot_general(a, b, (((1,), (1,)), ((), ())), preferred_element_type=F32)


def _split_bf16(x):
    hi = x.astype(BF16)
    lo = (x - hi.astype(F32)).astype(BF16)
    return hi, lo


def _head_norm(x, g):
    n = x.shape[-1]
    w = min(n, MXU_WIDTH)
    r = lax.broadcasted_iota(jnp.int32, (w, w), 0) // HEAD_DIM
    c = lax.broadcasted_iota(jnp.int32, (w, w), 1) // HEAD_DIM
    bd = jnp.where(r == c, 1.0, 0.0).astype(BF16)
    hi, lo = _split_bf16(x * x)
    ms = jnp.concatenate([_dot(hi[:, j:j + w], bd) + _dot(lo[:, j:j + w], bd) for j in range(0, n, w)],
                         axis=1) * (1.0 / HEAD_DIM)
    return x * lax.rsqrt(ms + NORM_EPS) * g


def _ffn_kernel(x_ref, g_ref, w1_ref, w3_ref, w2_ref, *refs):
    n_cast = (len(refs) - 1) // 2
    o_ref = refs[n_cast]
    x = x_ref[...]
    h = _rms(x, g_ref[...]).astype(BF16)
    a = _dot(h, w1_ref[...])
    b = _dot(h, w3_ref[...])
    act = (a * b / (1.0 + jnp.exp(-a))).astype(BF16)
    o_ref[...] = x + 0.5 * _dot(act, w2_ref[...])
    for src, dst in zip(refs[:n_cast], refs[n_cast + 1:]):
        dst[...] = src[...].astype(BF16)


def _ffn(x, g, w1, w3, w2, cast=()):
    s, d = x.shape
    f = w1.shape[1]
    steps = s // ROW_TILE
    row = lambda i: (i, 0)
    n_slabs = min(CAST_SLABS, steps)
    per_slab = steps // n_slabs
    slab = lambda i: (i // per_slab, 0)
    slabs = [pl.BlockSpec((w.shape[0] // n_slabs, w.shape[1]), slab) for w in cast]
    out = pl.pallas_call(
        _ffn_kernel,
        out_shape=(jax.ShapeDtypeStruct((s, d), F32), *[jax.ShapeDtypeStruct(w.shape, BF16) for w in cast]),
        grid=(steps,),
        in_specs=[pl.BlockSpec((ROW_TILE, d), row), _resident((1, d)),
                  _resident((d, f)), _resident((d, f)), _resident((f, d)), *slabs],
        out_specs=(pl.BlockSpec((ROW_TILE, d), row), *slabs),
        compiler_params=_params(1),
        name="ffn",
    )(x, g, w1, w3, w2, *cast)
    return out[0], list(out[1:])


def _memkv_kernel(mem_ref, g_ref, w_ref, kn_ref, k_ref, v_ref):
    h = _rms(mem_ref[...], g_ref[...]).astype(BF16)
    kv = _dot(h, w_ref[...])
    k_ref[...] = _head_norm(kv[:, :MEM_W], kn_ref[...]).astype(BF16)
    v_ref[...] = kv[:, MEM_W:].astype(BF16)


def _memkv(mem, g, w, kn):
    m = mem.shape[0]
    out = jax.ShapeDtypeStruct((m, MEM_W), BF16)
    return pl.pallas_call(_memkv_kernel, out_shape=(out, out), name="memkv",
                          compiler_params=pltpu.CompilerParams(vmem_limit_bytes=VMEM_LIMIT),
                          )(mem, g, w, kn)


def _rope(x, cos, sin_signed):
    lane = lax.broadcasted_iota(jnp.int32, (x.shape[0], LANES), 1)
    first_half = (lane % HEAD_DIM) < (HEAD_DIM // 2)
    out = []
    for j in range(x.shape[1] // LANES):
        xs = x[:, j * LANES:(j + 1) * LANES]
        partner = jnp.where(first_half, pltpu.roll(xs, LANES - HEAD_DIM // 2, 1),
                            pltpu.roll(xs, HEAD_DIM // 2, 1))
        out.append(xs * cos + partner * sin_signed)
    return jnp.concatenate(out, axis=1)


def _rope_lanes(cos_half, sin_half):
    half = HEAD_DIM // 2
    f = lax.broadcasted_iota(jnp.int32, (half, LANES), 0)
    lane = lax.broadcasted_iota(jnp.int32, (half, LANES), 1)
    hit = (lane % half) == f
    spread = jnp.where(hit, 1.0, 0.0).astype(BF16)
    signed = jnp.where(hit, jnp.where((lane % HEAD_DIM) < half, -1.0, 1.0), 0.0).astype(BF16)
    c_hi, c_lo = _split_bf16(cos_half)
    s_hi, s_lo = _split_bf16(sin_half)
    return _dot(c_hi, spread) + _dot(c_lo, spread), _dot(s_hi, signed) + _dot(s_lo, signed)


def _proj_kernel(x_ref, g_ref, w_ref, cos_ref, sin_ref, qn_d_ref, kn_d_ref, qn_m_ref, km_ref, vm_ref,
                 qa_ref, ka_ref, va_ref, qb_ref, kb_ref, vb_ref, yc_ref):
    h = _rms(x_ref[...], g_ref[...]).astype(BF16)
    cos, sin_signed = _rope_lanes(cos_ref[...].T, sin_ref[...].T)

    def cols(lo, width):
        return _dot(h, w_ref[:, lo:lo + width])

    qa_ref[...] = (cols(0, SB_W) * QK_SCALE).astype(BF16)
    ka_ref[...] = cols(SB_W, SB_W).astype(BF16)
    va_ref[...] = cols(2 * SB_W, SB_W).astype(BF16)
    base = 3 * SB_W
    qb_ref[...] = _rope(_head_norm(cols(base, DSA_W), qn_d_ref[...]), cos, sin_signed) * QK_SCALE
    kb_ref[...] = _rope(_head_norm(cols(base + DSA_W, DSA_W), kn_d_ref[...]), cos, sin_signed)
    vb_ref[...] = cols(base + 2 * DSA_W, DSA_W)

    qc = _head_norm(cols(base + 3 * DSA_W, MEM_W), qn_m_ref[...]) * QK_SCALE
    lane = lax.broadcasted_iota(jnp.int32, (qc.shape[0], LANES), 1)
    for j in range(MEM_W // LANES):
        sl = slice(j * LANES, (j + 1) * LANES)
        qs, km, vm = qc[:, sl], km_ref[:, sl], vm_ref[:, sl]
        outs = []
        for half in range(2):
            in_head = (lane < HEAD_DIM) == (half == 0)
            sc = _dot_nt(jnp.where(in_head, qs, 0.0).astype(BF16), km)
            p = jnp.exp(sc - jnp.max(sc, axis=-1, keepdims=True))
            outs.append(_dot(p.astype(BF16), vm) / jnp.sum(p, axis=-1, keepdims=True))
        yc_ref[:, sl] = jnp.where(lane < HEAD_DIM, outs[0], outs[1]).astype(BF16)


def _proj(x, g, w_in, cos, sin_signed, qn_d, kn_d, qn_m, km, vm):
    s, d = x.shape
    row = lambda i: (i, 0)
    tile = lambda w: pl.BlockSpec((ROW_TILE, w), row)
    sb = jax.ShapeDtypeStruct((s, SB_W), BF16)
    dsa = jax.ShapeDtypeStruct((s, DSA_W), F32)
    table = pl.BlockSpec((HEAD_DIM // 2, ROW_TILE), lambda i: (0, i))
    return pl.pallas_call(
        _proj_kernel,
        out_shape=(sb, sb, sb, dsa, dsa, dsa, jax.ShapeDtypeStruct((s, MEM_W), BF16)),
        grid=(s // ROW_TILE,),
        in_specs=[tile(d), _resident((1, d)), _resident(w_in.shape), table, table,
                  _resident((1, DSA_W)), _resident((1, DSA_W)), _resident((1, MEM_W)),
                  _resident(km.shape), _resident(vm.shape)],
        out_specs=(tile(SB_W), tile(SB_W), tile(SB_W), tile(DSA_W), tile(DSA_W), tile(DSA_W),
                   tile(MEM_W)),
        compiler_params=_params(1),
        name="proj",
    )(x, g, w_in, cos, sin_signed, qn_d, kn_d, qn_m, km, vm)


def _dsa_kernel(*refs):
    n_g = len(DSA_GROUPS)
    ins = [refs[5 * g:5 * g + 5] for g in range(n_g)]
    o_ref = refs[5 * n_g]
    bias_ref = refs[5 * n_g + 1]
    out_scratch = refs[5 * n_g + 2:5 * n_g + 2 + 2 * n_g]
    stage_scratch = refs[5 * n_g + 2 + 2 * n_g:]
    step = pl.program_id(0)
    lane = lax.broadcasted_iota(jnp.int32, (Q_BLOCK, LANES), 1)
    head0 = lane < HEAD_DIM

    def own_lanes(rows):
        ln = lax.broadcasted_iota(jnp.int32, (rows, LANES), 1)
        rw = lax.broadcasted_iota(jnp.int32, (rows, LANES), 0)
        return (ln < HEAD_DIM) == (rw < rows // 2)

    own_q, own_v = own_lanes(2 * Q_BLOCK), own_lanes(4 * Q_BLOCK)
    qi = lax.broadcasted_iota(jnp.int32, (2 * Q_BLOCK, 2 * Q_BLOCK), 0) % Q_BLOCK
    kj = lax.broadcasted_iota(jnp.int32, (2 * Q_BLOCK, 2 * Q_BLOCK), 1)
    dist = Q_BLOCK + qi - kj
    in_band = (dist >= 0) & (dist <= Q_BLOCK)
    bias_ref[0] = jnp.where(in_band, 0.0, -jnp.inf)
    bias_ref[1] = jnp.where(in_band & (kj >= Q_BLOCK), 0.0, -jnp.inf)
    seq_start = jnp.where(step == 0, 1, 0)

    def attend(qs, ks, vs, biases):
        units = range(len(qs))
        sc, v2 = [], []
        for j in units:
            q2 = jnp.where(own_q, jnp.concatenate([qs[j], qs[j]], axis=0), 0.0).astype(BF16)
            v = vs[j].astype(BF16)
            v2.append(jnp.where(own_v, jnp.concatenate([v, v], axis=0), jnp.zeros((), BF16)))
            sc.append(_dot_nt(q2, ks[j].astype(BF16)))
        p2, m, den = [], [], []
        for j in units:
            s_j = sc[j] + biases[j]
            m.append(jnp.max(s_j, axis=-1, keepdims=True))
            p = jnp.exp(s_j - m[j])
            den.append(jnp.sum(p, axis=-1, keepdims=True))
            p = p.astype(BF16)
            p2.append(jnp.concatenate([p[:Q_BLOCK], p[Q_BLOCK:]], axis=1))
        pv = [_dot(p2[j], v2[j]) for j in units]
        outs = []
        for j in units:
            inv = 1.0 / den[j]
            lse = m[j] + jnp.log(den[j])
            outs.append((pv[j] * jnp.where(head0, inv[:Q_BLOCK], inv[Q_BLOCK:]),
                         jnp.where(head0, lse[:Q_BLOCK], lse[Q_BLOCK:])))
        return outs

    n_trips = DSA_UNIT // Q_BLOCK // DSA_UNROLL
    staged = 0
    for g, (window, r) in enumerate(DSA_GROUPS):
        assert window // r == Q_BLOCK
        q_ref, k_ref, kp_ref, v_ref, vp_ref = ins[g]
        og_ref, lg_ref = out_scratch[2 * g:2 * g + 2]
        prev = Q_BLOCK * r

        if r <= DSA_STAGE:
            assert DSA_UNROLL % r == 0

            def trip(t, first, r=r, prev=prev, q_ref=q_ref, k_ref=k_ref, kp_ref=kp_ref, v_ref=v_ref,
                     vp_ref=vp_ref, og_ref=og_ref, lg_ref=lg_ref):
                qs, ks, vs, biases, rows = [], [], [], [], []
                for j in range(DSA_UNROLL):
                    sub, c = t * (DSA_UNROLL // r) + j // r, j % r
                    rows.append(pl.ds(sub * prev + c, Q_BLOCK, stride=r))
                    qs.append(q_ref[rows[j], :])
                    if first and sub == 0:
                        half = pl.ds(c, Q_BLOCK, stride=r)
                        ks.append(jnp.concatenate([kp_ref[half, :], k_ref[half, :]], axis=0))
                        vs.append(jnp.concatenate([vp_ref[half, :], v_ref[half, :]], axis=0))
                        biases.append(bias_ref[seq_start])
                    else:
                        both = pl.ds((sub - 1) * prev + c, 2 * Q_BLOCK, stride=r)
                        ks.append(k_ref[both, :])
                        vs.append(v_ref[both, :])
                        biases.append(bias_ref[0])
                for j, (o, lse) in enumerate(attend(qs, ks, vs, biases)):
                    og_ref[rows[j], :] = o
                    lg_ref[rows[j], :] = lse

            trip(0, True)

            def later_trip(t, carry, trip=trip):
                trip(t, False)
                return carry

            lax.fori_loop(1, n_trips, later_trip, 0)
        else:
            inner = r // DSA_STAGE
            per_trip = DSA_UNROLL // DSA_STAGE
            assert inner <= DSA_STAGE and prev == DSA_UNIT and DSA_UNROLL % DSA_STAGE == 0 and inner % per_trip == 0
            qs_ref, ks_ref, vs_ref, os_ref, ls_ref = stage_scratch[5 * staged:5 * staged + 5]
            staged += 1
            half = Q_BLOCK * inner
            for c in range(DSA_STAGE):
                coarse = pl.ds(c, half, stride=DSA_STAGE)
                qs_ref[c] = q_ref[coarse, :]
                ks_ref[c, :half] = kp_ref[coarse, :]
                ks_ref[c, half:] = k_ref[coarse, :]
                vs_ref[c, :half] = vp_ref[coarse, :]
                vs_ref[c, half:] = v_ref[coarse, :]

            def staged_trip(t, carry, inner=inner, per_trip=per_trip, qs_ref=qs_ref, ks_ref=ks_ref,
                            vs_ref=vs_ref, os_ref=os_ref, ls_ref=ls_ref):
                bias = bias_ref[seq_start]
                spots = [(c, t * per_trip + f) for f in range(per_trip) for c in range(DSA_STAGE)]
                fine_q = lambda fine: pl.ds(fine, Q_BLOCK, stride=inner)
                fine_k = lambda fine: pl.ds(fine, 2 * Q_BLOCK, stride=inner)
                outs = attend([qs_ref.at[c][fine_q(fine), :] for c, fine in spots],
                              [ks_ref.at[c][fine_k(fine), :] for c, fine in spots],
                              [vs_ref.at[c][fine_k(fine), :] for c, fine in spots],
                              [bias] * len(spots))
                for (c, fine), (o, lse) in zip(spots, outs):
                    os_ref.at[c][fine_q(fine), :] = o
                    ls_ref.at[c][fine_q(fine), :] = lse
                return carry

            lax.fori_loop(0, n_trips, staged_trip, 0)
            for c in range(DSA_STAGE):
                coarse = pl.ds(c, half, stride=DSA_STAGE)
                og_ref[coarse, :] = os_ref[c]
                lg_ref[coarse, :] = ls_ref[c]

    lse = [out_scratch[2 * g + 1][...] for g in range(n_g)]
    top = functools.reduce(jnp.maximum, lse)
    e = [jnp.exp(l - top) for l in lse]
    num = sum(e[g] * out_scratch[2 * g][...] for g in range(n_g))
    o_ref[...] = (num / sum(e)).astype(o_ref.dtype)


def _dilated(q, k, v):
    s = q.shape[0]
    n_pairs = DSA_OUT_W // LANES
    in_specs, stage_scratch = [], []
    for g, (_, r) in enumerate(DSA_GROUPS):
        prev = Q_BLOCK * r
        per_unit = DSA_UNIT // prev
        cur = pl.BlockSpec((DSA_UNIT, LANES), lambda i, p, g=g: (i, n_pairs * g + p))
        prv = pl.BlockSpec((prev, LANES),
                           lambda i, p, g=g, n=per_unit: (jnp.maximum(i * n - 1, 0), n_pairs * g + p))
        in_specs += [cur, cur, prv, cur, prv]
        if r > DSA_STAGE:
            rows = DSA_UNIT // DSA_STAGE
            stage_scratch += [pltpu.VMEM((DSA_STAGE, rows, LANES), F32),
                              pltpu.VMEM((DSA_STAGE, 2 * rows, LANES), F32),
                              pltpu.VMEM((DSA_STAGE, 2 * rows, LANES), F32),
                              pltpu.VMEM((DSA_STAGE, rows, LANES), F32),
                              pltpu.VMEM((DSA_STAGE, rows, LANES), F32)]
    scratch = ([pltpu.VMEM((2, 2 * Q_BLOCK, 2 * Q_BLOCK), F32)]
               + [pltpu.VMEM((DSA_UNIT, LANES), F32)] * (2 * len(DSA_GROUPS)) + stage_scratch)
    args = []
    for _ in DSA_GROUPS:
        args += [q, k, k, v, v]
    return pl.pallas_call(
        _dsa_kernel,
        out_shape=jax.ShapeDtypeStruct((s, DSA_OUT_W), BF16),
        grid=(s // DSA_UNIT, n_pairs),
        in_specs=in_specs,
        out_specs=pl.BlockSpec((DSA_UNIT, LANES), lambda i, p: (i, p)),
        scratch_shapes=scratch,
        compiler_params=_params(2),
        name="dilated",
    )(*args)


def _tail_kernel(q_ref, kprev_ref, kcur_ref, vprev_ref, vcur_ref, k_hbm, v_hbm,
                 x_ref, gm_ref, wg_ref, bg_ref, yb_ref, yc_ref, wa_ref, wb_ref, wc_ref, wo_ref,
                 gf_ref, w1_ref, w3_ref, w2_ref, o_ref,
                 q2_ref, carry_ref, acc_ref, ya_ref, kblk_ref, vblk_ref, sem):
    n_slots = SB_QBLOCKS_PER_STEP
    step = pl.program_id(0)
    n_tiles = pl.num_programs(0) - 1
    first_qb = step * n_slots
    n_pairs = SB_W // LANES
    lane = lax.broadcasted_iota(jnp.int32, (2 * Q_BLOCK, LANES), 1)
    row = lax.broadcasted_iota(jnp.int32, (2 * Q_BLOCK, LANES), 0)
    own_lanes = (lane < HEAD_DIM) == (row < Q_BLOCK)
    before = lane < (row % Q_BLOCK)
    j = lax.broadcasted_iota(jnp.int32, (2 * Q_BLOCK, 2 * Q_BLOCK), 0) % Q_BLOCK
    s = lax.broadcasted_iota(jnp.int32, (2 * Q_BLOCK, 2 * Q_BLOCK), 1)
    suffix = jnp.where((s >= Q_BLOCK) | (j >= s), 1.0, 0.0).astype(BF16)

    def stack(x):
        return jnp.where(own_lanes, jnp.concatenate([x, x], axis=0), jnp.zeros((), x.dtype))

    pairs = range(n_pairs)
    sls = [slice(p * LANES, (p + 1) * LANES) for p in pairs]
    for slot in range(n_slots):
        for p in pairs:
            q2_ref[slot, p] = stack(q_ref[slot * Q_BLOCK:(slot + 1) * Q_BLOCK, sls[p]])

    def kv_block(handle, p):
        if handle is None:
            return kblk_ref[:, sls[p]], vblk_ref[:, sls[p]]
        k_win, v_win = (kprev_ref, vprev_ref) if handle < n_slots else (kcur_ref, vcur_ref)
        rows = slice((handle % n_slots) * Q_BLOCK, (handle % n_slots + 1) * Q_BLOCK)
        return k_win[rows, sls[p]], v_win[rows, sls[p]]

    def walk(jobs, from_diagonal, last_rows=Q_BLOCK, between=(None, None, None)):
        partial = last_rows < Q_BLOCK
        assert not partial or (from_diagonal and all(len(kbs) > 1 for _, kbs in jobs))
        chains = [(i, b) for i, (_, kbs) in enumerate(jobs) for b in range(len(kbs))]

        def is_partial(i, b):
            return partial and b == len(jobs[i][1]) - 1

        def visiting(x, i, b):
            if is_partial(i, b):
                return jnp.concatenate([x[:last_rows], x[Q_BLOCK:Q_BLOCK + last_rows]], axis=0)
            return x

        z = {}
        for i, (slot, kbs) in enumerate(jobs):
            n_full = len(kbs) - 1 if partial else len(kbs)
            for p in pairs:
                for b in range(0, n_full - 1, 2):
                    keys = jnp.concatenate([kv_block(kbs[b], p)[0], kv_block(kbs[b + 1], p)[0]], axis=0)
                    both = _dot_nt(q2_ref[slot, p], keys)
                    z[i, b, p], z[i, b + 1, p] = both[:, :Q_BLOCK], both[:, Q_BLOCK:]
                for b in list(range(n_full - n_full % 2, n_full)) + list(range(n_full, len(kbs))):
                    z[i, b, p] = _dot_nt(visiting(q2_ref[slot, p], i, b), kv_block(kbs[b], p)[0])
        if between[0] is not None:
            between[0]()
        split = {}
        for i, b in chains:
            for p in pairs:
                if from_diagonal and b == 0:
                    z[i, b, p] = jnp.where(before, z[i, b, p], SB_MASKED_LOGIT)
                nz = -z[i, b, p]
                lf = jnp.minimum(nz, 0.0) - jnp.log(1.0 + jnp.exp(jnp.minimum(z[i, b, p], nz)))
                split[i, b, p] = jnp.concatenate(_split_bf16(lf), axis=1)
        sums = {(i, b, p): _dot(split[i, b, p], suffix) for i, b in chains for p in pairs}
        if between[1] is not None:
            between[1]()
        w2, w_last, tops = {}, {}, []
        for i, (slot, kbs) in enumerate(jobs):
            top = None
            for p in pairs:
                carry = None if from_diagonal else carry_ref[slot, p]
                ws = []
                for b in range(len(kbs)):
                    logw = z[i, b, p] + sums[i, b, p][:, :Q_BLOCK]
                    total = sums[i, b, p][:, Q_BLOCK:]
                    if carry is not None:
                        logw = logw + visiting(carry, i, b)
                        total = total + visiting(carry, i, b)
                    w = jnp.exp(logw).astype(BF16)
                    half = w.shape[0] // 2
                    if is_partial(i, b):
                        w_last[i, p] = jnp.concatenate([w[:half], w[half:]], axis=1)
                        carry = jnp.concatenate([total[:half], carry[last_rows:Q_BLOCK],
                                                 total[half:], carry[Q_BLOCK + last_rows:]], axis=0)
                    else:
                        ws += [w[:half], w[half:]]
                        carry = total
                w2[i, p] = jnp.concatenate(ws, axis=1)
                if not partial:
                    carry_ref[slot, p] = carry
                top = carry if top is None else jnp.maximum(top, carry)
            tops.append(top)
        for i, (slot, kbs) in enumerate(jobs):
            n_full = len(kbs) - 1 if partial else len(kbs)
            for p in pairs:
                v2 = jnp.concatenate([stack(kv_block(kbs[b], p)[1]) for b in range(n_full)], axis=0)
                pv = _dot(w2[i, p], v2)
                if partial:
                    extra = _dot(w_last[i, p], stack(kv_block(kbs[n_full], p)[1]))
                    pv = jnp.concatenate([pv[:last_rows] + extra, pv[last_rows:]], axis=0)
                acc_ref[slot, p] = pv if from_diagonal else acc_ref[slot, p] + pv
        if between[2] is not None:
            between[2]()
        return [jnp.max(top) for top in tops]

    dense = {}

    def merge_stage():
        x = x_ref[...]
        d = x.shape[1]
        h = _rms(x, gm_ref[...]).astype(BF16)
        ya = ya_ref[(step + 1) % 2]
        merged = None
        for i, (y, w_ref) in enumerate(((ya, wa_ref), (yb_ref[...], wb_ref), (yc_ref[...], wc_ref))):
            logits = _dot(h, wg_ref[:, i * d:(i + 1) * d]) + bg_ref[:, i * d:(i + 1) * d]
            term = _dot(y, w_ref[...]) / (1.0 + jnp.exp(-logits))
            merged = term if merged is None else merged + term
        dense["x"] = x + _dot(merged.astype(BF16), wo_ref[...])
        dense["h"] = _rms(dense["x"], gf_ref[...]).astype(BF16)

    def up_stage():
        a = _dot(dense["h"], w1_ref[...])
        b = _dot(dense["h"], w3_ref[...])
        dense["act"] = (a * b / (1.0 + jnp.exp(-a))).astype(BF16)

    def down_stage():
        o_ref[...] = dense["x"] + 0.5 * _dot(dense["act"], w2_ref[...])

    def fetch(kb):
        rows = pl.ds(pl.multiple_of(kb * Q_BLOCK, Q_BLOCK), Q_BLOCK)
        copies = [pltpu.make_async_copy(k_hbm.at[rows, :], kblk_ref, sem.at[0]),
                  pltpu.make_async_copy(v_hbm.at[rows, :], vblk_ref, sem.at[1])]
        for c in copies:
            c.start()
        for c in copies:
            c.wait()

    def cond(state):
        kb, top = state
        return jnp.logical_and(kb >= 0, top > SB_LOG_CUTOFF)

    def general(slot):
        def run():
            def visit(kb, from_diagonal):
                fetch(kb)
                return walk([(slot, [None])], from_diagonal)[0]

            def body(state):
                kb, _ = state
                return kb - 1, visit(kb, False)

            qb = first_qb + slot
            lax.while_loop(cond, body, (qb - 1, visit(qb, True)))
        return run

    n_fused = SB_FUSED_BLOCKS
    assert n_fused - 1 <= n_slots

    def usual():
        jobs = [(slot, [n_slots + slot - b for b in range(n_fused)]) for slot in range(n_slots)]
        return tuple(walk(jobs, True, SB_LAST_BLOCK_ROWS, (merge_stage, up_stage, down_stage)))

    def edge():
        @pl.when(step == n_tiles)
        def _():
            merge_stage()
            up_stage()
            down_stage()
        return (jnp.float32(jnp.inf),) * n_slots

    fused_tops = lax.cond(jnp.logical_and(step >= 1, step < n_tiles), usual, edge)
    for slot in range(n_slots):
        lax.cond(jnp.logical_and(fused_tops[slot] > SB_LOG_CUTOFF, step < n_tiles), general(slot), lambda: None)
        for p in pairs:
            ya_ref[step % 2, slot * Q_BLOCK:(slot + 1) * Q_BLOCK, sls[p]] = acc_ref[slot, p].astype(BF16)


def _tail(q, k, v, x, gm, wg, bg, yb, yc, wa, wb, wc, wo, gf, w1, w3, w2):
    s, w = q.shape
    d = x.shape[1]
    n_pairs = w // LANES
    n_slots = SB_QBLOCKS_PER_STEP
    rows = n_slots * Q_BLOCK
    n_tiles = s // rows
    this = lambda i: (jnp.minimum(i, n_tiles - 1), 0)
    last = lambda i: (jnp.maximum(i - 1, 0), 0)
    hbm = pl.BlockSpec(memory_space=pl.ANY)
    return pl.pallas_call(
        _tail_kernel,
        out_shape=jax.ShapeDtypeStruct((s, d), F32),
        grid=(n_tiles + 1,),
        in_specs=[pl.BlockSpec((rows, w), this), pl.BlockSpec((rows, w), last), pl.BlockSpec((rows, w), this),
                  pl.BlockSpec((rows, w), last), pl.BlockSpec((rows, w), this), hbm, hbm,
                  pl.BlockSpec((rows, d), last), _resident((1, d)), _resident(wg.shape), _resident(bg.shape),
                  pl.BlockSpec((rows, DSA_OUT_W), last), pl.BlockSpec((rows, MEM_W), last),
                  _resident(wa.shape), _resident(wb.shape), _resident(wc.shape), _resident(wo.shape),
                  _resident((1, d)), _resident(w1.shape), _resident(w3.shape), _resident(w2.shape)],
        out_specs=pl.BlockSpec((rows, d), last),
        scratch_shapes=[pltpu.VMEM((n_slots, n_pairs, 2 * Q_BLOCK, LANES), BF16),
                        pltpu.VMEM((n_slots, n_pairs, 2 * Q_BLOCK, LANES), F32),
                        pltpu.VMEM((n_slots, n_pairs, Q_BLOCK, LANES), F32),
                        pltpu.VMEM((2, rows, w), BF16),
                        pltpu.VMEM((Q_BLOCK, w), BF16), pltpu.VMEM((Q_BLOCK, w), BF16),
                        pltpu.SemaphoreType.DMA((2,))],
        compiler_params=_params(1),
        name="tail",
    )(q, k, k, v, v, k, v, x, gm, wg, bg, yb, yc, wa, wb, wc, wo, gf, w1, w3, w2)


def _rope_tables(s):
    half = HEAD_DIM // 2
    inv_freq = jnp.power(ROPE_THETA, -jnp.arange(half, dtype=F32) / half)
    ang = inv_freq[:, None] * jnp.arange(s).astype(F32)[None, :]
    return jnp.cos(ang), jnp.sin(ang)


def _layer(x, mem, p):
    s = x.shape[0]
    bf = lambda w: w.astype(BF16)
    vec = lambda v: v.reshape(1, -1)
    heads = lambda v, n: jnp.tile(v, n).reshape(1, -1)
    cos, sin = _rope_tables(s)

    later = ("w_in", "w_gate", "w_branch_sb", "w_branch_dsa", "w_branch_mem", "w_out",
             "ffn2_w1", "ffn2_w3", "ffn2_w2")
    x, casted = _ffn(x, vec(p["ffn1_norm"]), bf(p["ffn1_w1"]), bf(p["ffn1_w3"]), bf(p["ffn1_w2"]),
                     cast=[p[name] for name in later])
    w = dict(zip(later, casted))
    km, vm = _memkv(mem, vec(p["mem_norm"]), bf(p["w_mem_kv"]), heads(p["kn_mem"], MEM_HEADS))
    n_dsa = DSA_W // HEAD_DIM
    qa, ka, va, qb, kb, vb, yc = _proj(
        x, vec(p["mix_norm"]), w["w_in"], cos, sin,
        heads(p["qn_dsa"], n_dsa), heads(p["kn_dsa"], n_dsa), heads(p["qn_mem"], MEM_HEADS), km, vm)
    yb = _dilated(qb, kb, vb)
    return _tail(qa, ka, va, x, vec(p["mix_norm"]), w["w_gate"], vec(p["b_gate"]), yb, yc,
                 w["w_branch_sb"], w["w_branch_dsa"], w["w_branch_mem"], w["w_out"],
                 vec(p["ffn2_norm"]), w["ffn2_w1"], w["ffn2_w3"], w["ffn2_w2"])


_PARAM_NAMES = ("ffn1_norm", "ffn1_w1", "ffn1_w3", "ffn1_w2", "mix_norm", "mem_norm", "w_in", "w_mem_kv",
                "qn_dsa", "kn_dsa", "qn_mem", "kn_mem", "w_branch_sb", "w_branch_dsa", "w_branch_mem",
                "w_gate", "b_gate", "w_out", "ffn2_norm", "ffn2_w1", "ffn2_w3", "ffn2_w2")


def kernel(x, mem, ffn1_norm, ffn1_w1, ffn1_w3, ffn1_w2, mix_norm, mem_norm, w_in, w_mem_kv, qn_dsa, kn_dsa, qn_mem, kn_mem, w_branch_sb, w_branch_dsa, w_branch_mem, w_gate, b_gate, w_out, ffn2_norm, ffn2_w1, ffn2_w3, ffn2_w2):
    stacked = dict(zip(_PARAM_NAMES, (ffn1_norm, ffn1_w1, ffn1_w3, ffn1_w2, mix_norm, mem_norm, w_in,
                                      w_mem_kv, qn_dsa, kn_dsa, qn_mem, kn_mem, w_branch_sb, w_branch_dsa,
                                      w_branch_mem, w_gate, b_gate, w_out, ffn2_norm, ffn2_w1, ffn2_w3,
                                      ffn2_w2)))
    depth = ffn1_norm.shape[0]
    outs = []
    for b in range(x.shape[0]):
        xb = x[b]
        for l in range(depth):
            xb = _layer(xb, mem[b], {k: v[l] for k, v in stacked.items()})
        outs.append(xb)
    return jnp.stack(outs)
```

```python
import functools

import jax
import jax.numpy as jnp
from jax import lax
from jax.experimental import pallas as pl
from jax.experimental.pallas import tpu as pltpu

F32 = jnp.float32
BF16 = jnp.bfloat16

HEAD_DIM = 64
SB_HEADS = 8
DSA_GROUPS = ((128, 1), (512, 4), (2048, 16))
DSA_HEADS_PER_GROUP = 4
MEM_HEADS = 4
ROPE_THETA = 10000.0
NORM_EPS = 1e-6
Q_BLOCK = 128
SB_W = SB_HEADS * HEAD_DIM
DSA_W = DSA_HEADS_PER_GROUP * len(DSA_GROUPS) * HEAD_DIM
DSA_OUT_W = DSA_HEADS_PER_GROUP * HEAD_DIM
MEM_W = MEM_HEADS * HEAD_DIM
QK_SCALE = HEAD_DIM ** -0.5

LANES = 128
MXU_WIDTH = 256
DSA_UNIT = Q_BLOCK * max(r for _, r in DSA_GROUPS)
DSA_UNROLL = 16
DSA_STAGE = 4
ROW_TILE = 512
VMEM_LIMIT = 56 * 1024 * 1024
CAST_SLABS = 16
SB_LOG_CUTOFF = -104.0
SB_MASKED_LOGIT = -1e30
SB_FUSED_BLOCKS = 3
SB_QBLOCKS_PER_STEP = 2
SB_LAST_BLOCK_ROWS = 48


def _resident(shape):
    zeros = (0,) * len(shape)
    return pl.BlockSpec(shape, lambda *_: zeros, pipeline_mode=pl.Buffered(1))


def _params(n_axes):
    return pltpu.CompilerParams(dimension_semantics=("arbitrary",) * n_axes,
                                vmem_limit_bytes=VMEM_LIMIT)


def _rms(x, g):
    return x * lax.rsqrt(jnp.mean(x * x, axis=-1, keepdims=True) + NORM_EPS) * g


def _dot(a, b):
    return jnp.dot(a, b, preferred_element_type=F32)


def _dot_nt(a, b):
    return lax.dot_general(a, b, (((1,), (1,)), ((), ())), preferred_element_type=F32)


def _split_bf16(x):
    hi = x.astype(BF16)
    lo = (x - hi.astype(F32)).astype(BF16)
    return hi, lo


def _head_norm(x, g):
    n = x.shape[-1]
    w = min(n, MXU_WIDTH)
    r = lax.broadcasted_iota(jnp.int32, (w, w), 0) // HEAD_DIM
    c = lax.broadcasted_iota(jnp.int32, (w, w), 1) // HEAD_DIM
    bd = jnp.where(r == c, 1.0, 0.0).astype(BF16)
    hi, lo = _split_bf16(x * x)
    ms = jnp.concatenate([_dot(hi[:, j:j + w], bd) + _dot(lo[:, j:j + w], bd) for j in range(0, n, w)],
                         axis=1) * (1.0 / HEAD_DIM)
    return x * lax.rsqrt(ms + NORM_EPS) * g


def _ffn_kernel(x_ref, g_ref, w1_ref, w3_ref, w2_ref, *refs):
    n_cast = (len(refs) - 1) // 2
    o_ref = refs[n_cast]
    x = x_ref[...]
    h = _rms(x, g_ref[...]).astype(BF16)
    a = _dot(h, w1_ref[...])
    b = _dot(h, w3_ref[...])
    act = (a * b / (1.0 + jnp.exp(-a))).astype(BF16)
    o_ref[...] = x + 0.5 * _dot(act, w2_ref[...])
    for src, dst in zip(refs[:n_cast], refs[n_cast + 1:]):
        dst[...] = src[...].astype(BF16)


def _ffn(x, g, w1, w3, w2, cast=()):
    s, d = x.shape
    f = w1.shape[1]
    steps = s // ROW_TILE
    row = lambda i: (i, 0)
    n_slabs = min(CAST_SLABS, steps)
    per_slab = steps // n_slabs
    slab = lambda i: (i // per_slab, 0)
    slabs = [pl.BlockSpec((w.shape[0] // n_slabs, w.shape[1]), slab) for w in cast]
    out = pl.pallas_call(
        _ffn_kernel,
        out_shape=(jax.ShapeDtypeStruct((s, d), F32), *[jax.ShapeDtypeStruct(w.shape, BF16) for w in cast]),
        grid=(steps,),
        in_specs=[pl.BlockSpec((ROW_TILE, d), row), _resident((1, d)),
                  _resident((d, f)), _resident((d, f)), _resident((f, d)), *slabs],
        out_specs=(pl.BlockSpec((ROW_TILE, d), row), *slabs),
        compiler_params=_params(1),
        name="ffn",
    )(x, g, w1, w3, w2, *cast)
    return out[0], list(out[1:])


def _memkv_kernel(mem_ref, g_ref, w_ref, kn_ref, k_ref, v_ref):
    h = _rms(mem_ref[...], g_ref[...]).astype(BF16)
    kv = _dot(h, w_ref[...])
    k_ref[...] = _head_norm(kv[:, :MEM_W], kn_ref[...]).astype(BF16)
    v_ref[...] = kv[:, MEM_W:].astype(BF16)


def _memkv(mem, g, w, kn):
    m = mem.shape[0]
    out = jax.ShapeDtypeStruct((m, MEM_W), BF16)
    return pl.pallas_call(_memkv_kernel, out_shape=(out, out), name="memkv",
                          compiler_params=pltpu.CompilerParams(vmem_limit_bytes=VMEM_LIMIT),
                          )(mem, g, w, kn)


def _rope(x, cos, sin_signed):
    lane = lax.broadcasted_iota(jnp.int32, (x.shape[0], LANES), 1)
    first_half = (lane % HEAD_DIM) < (HEAD_DIM // 2)
    out = []
    for j in range(x.shape[1] // LANES):
        xs = x[:, j * LANES:(j + 1) * LANES]
        partner = jnp.where(first_half, pltpu.roll(xs, LANES - HEAD_DIM // 2, 1),
                            pltpu.roll(xs, HEAD_DIM // 2, 1))
        out.append(xs * cos + partner * sin_signed)
    return jnp.concatenate(out, axis=1)


def _rope_lanes(cos_half, sin_half):
    half = HEAD_DIM // 2
    f = lax.broadcasted_iota(jnp.int32, (half, LANES), 0)
    lane = lax.broadcasted_iota(jnp.int32, (half, LANES), 1)
    hit = (lane % half) == f
    spread = jnp.where(hit, 1.0, 0.0).astype(BF16)
    signed = jnp.where(hit, jnp.where((lane % HEAD_DIM) < half, -1.0, 1.0), 0.0).astype(BF16)
    c_hi, c_lo = _split_bf16(cos_half)
    s_hi, s_lo = _split_bf16(sin_half)
    return _dot(c_hi, spread) + _dot(c_lo, spread), _dot(s_hi, signed) + _dot(s_lo, signed)


def _proj_kernel(x_ref, g_ref, w_ref, cos_ref, sin_ref, qn_d_ref, kn_d_ref, qn_m_ref, km_ref, vm_ref,
                 qa_ref, ka_ref, va_ref, qb_ref, kb_ref, vb_ref, yc_ref):
    h = _rms(x_ref[...], g_ref[...]).astype(BF16)
    cos, sin_signed = _rope_lanes(cos_ref[...].T, sin_ref[...].T)

    def cols(lo, width):
        return _dot(h, w_ref[:, lo:lo + width])

    qa_ref[...] = (cols(0, SB_W) * QK_SCALE).astype(BF16)
    ka_ref[...] = cols(SB_W, SB_W).astype(BF16)
    va_ref[...] = cols(2 * SB_W, SB_W).astype(BF16)
    base = 3 * SB_W
    qb_ref[...] = _rope(_head_norm(cols(base, DSA_W), qn_d_ref[...]), cos, sin_signed) * QK_SCALE
    kb_ref[...] = _rope(_head_norm(cols(base + DSA_W, DSA_W), kn_d_ref[...]), cos, sin_signed)
    vb_ref[...] = cols(base + 2 * DSA_W, DSA_W)

    qc = _head_norm(cols(base + 3 * DSA_W, MEM_W), qn_m_ref[...]) * QK_SCALE
    lane = lax.broadcasted_iota(jnp.int32, (qc.shape[0], LANES), 1)
    for j in range(MEM_W // LANES):
        sl = slice(j * LANES, (j + 1) * LANES)
        qs, km, vm = qc[:, sl], km_ref[:, sl], vm_ref[:, sl]
        outs = []
        for half in range(2):
            in_head = (lane < HEAD_DIM) == (half == 0)
            sc = _dot_nt(jnp.where(in_head, qs, 0.0).astype(BF16), km)
            p = jnp.exp(sc - jnp.max(sc, axis=-1, keepdims=True))
            outs.append(_dot(p.astype(BF16), vm) / jnp.sum(p, axis=-1, keepdims=True))
        yc_ref[:, sl] = jnp.where(lane < HEAD_DIM, outs[0], outs[1]).astype(BF16)


def _proj(x, g, w_in, cos, sin_signed, qn_d, kn_d, qn_m, km, vm):
    s, d = x.shape
    row = lambda i: (i, 0)
    tile = lambda w: pl.BlockSpec((ROW_TILE, w), row)
    sb = jax.ShapeDtypeStruct((s, SB_W), BF16)
    dsa = jax.ShapeDtypeStruct((s, DSA_W), F32)
    table = pl.BlockSpec((HEAD_DIM // 2, ROW_TILE), lambda i: (0, i))
    return pl.pallas_call(
        _proj_kernel,
        out_shape=(sb, sb, sb, dsa, dsa, dsa, jax.ShapeDtypeStruct((s, MEM_W), BF16)),
        grid=(s // ROW_TILE,),
        in_specs=[tile(d), _resident((1, d)), _resident(w_in.shape), table, table,
                  _resident((1, DSA_W)), _resident((1, DSA_W)), _resident((1, MEM_W)),
                  _resident(km.shape), _resident(vm.shape)],
        out_specs=(tile(SB_W), tile(SB_W), tile(SB_W), tile(DSA_W), tile(DSA_W), tile(DSA_W),
                   tile(MEM_W)),
        compiler_params=_params(1),
        name="proj",
    )(x, g, w_in, cos, sin_signed, qn_d, kn_d, qn_m, km, vm)


def _dsa_kernel(*refs):
    n_g = len(DSA_GROUPS)
    ins = [refs[5 * g:5 * g + 5] for g in range(n_g)]
    o_ref = refs[5 * n_g]
    bias_ref = refs[5 * n_g + 1]
    out_scratch = refs[5 * n_g + 2:5 * n_g + 2 + 2 * n_g]
    stage_scratch = refs[5 * n_g + 2 + 2 * n_g:]
    step = pl.program_id(0)
    lane = lax.broadcasted_iota(jnp.int32, (Q_BLOCK, LANES), 1)
    head0 = lane < HEAD_DIM

    def own_lanes(rows):
        ln = lax.broadcasted_iota(jnp.int32, (rows, LANES), 1)
        rw = lax.broadcasted_iota(jnp.int32, (rows, LANES), 0)
        return (ln < HEAD_DIM) == (rw < rows // 2)

    own_q, own_v = own_lanes(2 * Q_BLOCK), own_lanes(4 * Q_BLOCK)
    qi = lax.broadcasted_iota(jnp.int32, (2 * Q_BLOCK, 2 * Q_BLOCK), 0) % Q_BLOCK
    kj = lax.broadcasted_iota(jnp.int32, (2 * Q_BLOCK, 2 * Q_BLOCK), 1)
    dist = Q_BLOCK + qi - kj
    in_band = (dist >= 0) & (dist <= Q_BLOCK)
    bias_ref[0] = jnp.where(in_band, 0.0, -jnp.inf)
    bias_ref[1] = jnp.where(in_band & (kj >= Q_BLOCK), 0.0, -jnp.inf)
    seq_start = jnp.where(step == 0, 1, 0)

    def attend(qs, ks, vs, biases):
        units = range(len(qs))
        sc, v2 = [], []
        for j in units:
            q2 = jnp.where(own_q, jnp.concatenate([qs[j], qs[j]], axis=0), 0.0).astype(BF16)
            v = vs[j].astype(BF16)
            v2.append(jnp.where(own_v, jnp.concatenate([v, v], axis=0), jnp.zeros((), BF16)))
            sc.append(_dot_nt(q2, ks[j].astype(BF16)))
        p2, m, den = [], [], []
        for j in units:
            s_j = sc[j] + biases[j]
            m.append(jnp.max(s_j, axis=-1, keepdims=True))
            p = jnp.exp(s_j - m[j])
            den.append(jnp.sum(p, axis=-1, keepdims=True))
            p = p.astype(BF16)
            p2.append(jnp.concatenate([p[:Q_BLOCK], p[Q_BLOCK:]], axis=1))
        pv = [_dot(p2[j], v2[j]) for j in units]
        outs = []
        for j in units:
            inv = 1.0 / den[j]
            lse = m[j] + jnp.log(den[j])
            outs.append((pv[j] * jnp.where(head0, inv[:Q_BLOCK], inv[Q_BLOCK:]),
                         jnp.where(head0, lse[:Q_BLOCK], lse[Q_BLOCK:])))
        return outs

    n_trips = DSA_UNIT // Q_BLOCK // DSA_UNROLL
    staged = 0
    for g, (window, r) in enumerate(DSA_GROUPS):
        assert window // r == Q_BLOCK
        q_ref, k_ref, kp_ref, v_ref, vp_ref = ins[g]
        og_ref, lg_ref = out_scratch[2 * g:2 * g + 2]
        prev = Q_BLOCK * r

        if r <= DSA_STAGE:
            assert DSA_UNROLL % r == 0

            def trip(t, first, r=r, prev=prev, q_ref=q_ref, k_ref=k_ref, kp_ref=kp_ref, v_ref=v_ref,
                     vp_ref=vp_ref, og_ref=og_ref, lg_ref=lg_ref):
                qs, ks, vs, biases, rows = [], [], [], [], []
                for j in range(DSA_UNROLL):
                    sub, c = t * (DSA_UNROLL // r) + j // r, j % r
                    rows.append(pl.ds(sub * prev + c, Q_BLOCK, stride=r))
                    qs.append(q_ref[rows[j], :])
                    if first and sub == 0:
                        half = pl.ds(c, Q_BLOCK, stride=r)
                        ks.append(jnp.concatenate([kp_ref[half, :], k_ref[half, :]], axis=0))
                        vs.append(jnp.concatenate([vp_ref[half, :], v_ref[half, :]], axis=0))
                        biases.append(bias_ref[seq_start])
                    else:
                        both = pl.ds((sub - 1) * prev + c, 2 * Q_BLOCK, stride=r)
                        ks.append(k_ref[both, :])
                        vs.append(v_ref[both, :])
                        biases.append(bias_ref[0])
                for j, (o, lse) in enumerate(attend(qs, ks, vs, biases)):
                    og_ref[rows[j], :] = o
                    lg_ref[rows[j], :] = lse

            trip(0, True)

            def later_trip(t, carry, trip=trip):
                trip(t, False)
                return carry

            lax.fori_loop(1, n_trips, later_trip, 0)
        else:
            inner = r // DSA_STAGE
            per_trip = DSA_UNROLL // DSA_STAGE
            assert inner <= DSA_STAGE and prev == DSA_UNIT and DSA_UNROLL % DSA_STAGE == 0 and inner % per_trip == 0
            qs_ref, ks_ref, vs_ref, os_ref, ls_ref = stage_scratch[5 * staged:5 * staged + 5]
            staged += 1
            half = Q_BLOCK * inner
            for c in range(DSA_STAGE):
                coarse = pl.ds(c, half, stride=DSA_STAGE)
                qs_ref[c] = q_ref[coarse, :]
                ks_ref[c, :half] = kp_ref[coarse, :]
                ks_ref[c, half:] = k_ref[coarse, :]
                vs_ref[c, :half] = vp_ref[coarse, :]
                vs_ref[c, half:] = v_ref[coarse, :]

            def staged_trip(t, carry, inner=inner, per_trip=per_trip, qs_ref=qs_ref, ks_ref=ks_ref,
                            vs_ref=vs_ref, os_ref=os_ref, ls_ref=ls_ref):
                bias = bias_ref[seq_start]
                spots = [(c, t * per_trip + f) for f in range(per_trip) for c in range(DSA_STAGE)]
                fine_q = lambda fine: pl.ds(fine, Q_BLOCK, stride=inner)
                fine_k = lambda fine: pl.ds(fine, 2 * Q_BLOCK, stride=inner)
                outs = attend([qs_ref.at[c][fine_q(fine), :] for c, fine in spots],
                              [ks_ref.at[c][fine_k(fine), :] for c, fine in spots],
                              [vs_ref.at[c][fine_k(fine), :] for c, fine in spots],
                              [bias] * len(spots))
                for (c, fine), (o, lse) in zip(spots, outs):
                    os_ref.at[c][fine_q(fine), :] = o
                    ls_ref.at[c][fine_q(fine), :] = lse
                return carry

            lax.fori_loop(0, n_trips, staged_trip, 0)
            for c in range(DSA_STAGE):
                coarse = pl.ds(c, half, stride=DSA_STAGE)
                og_ref[coarse, :] = os_ref[c]
                lg_ref[coarse, :] = ls_ref[c]

    lse = [out_scratch[2 * g + 1][...] for g in range(n_g)]
    top = functools.reduce(jnp.maximum, lse)
    e = [jnp.exp(l - top) for l in lse]
    num = sum(e[g] * out_scratch[2 * g][...] for g in range(n_g))
    o_ref[...] = (num / sum(e)).astype(o_ref.dtype)


def _dilated(q, k, v):
    s = q.shape[0]
    n_pairs = DSA_OUT_W // LANES
    in_specs, stage_scratch = [], []
    for g, (_, r) in enumerate(DSA_GROUPS):
        prev = Q_BLOCK * r
        per_unit = DSA_UNIT // prev
        cur = pl.BlockSpec((DSA_UNIT, LANES), lambda i, p, g=g: (i, n_pairs * g + p))
        prv = pl.BlockSpec((prev, LANES),
                           lambda i, p, g=g, n=per_unit: (jnp.maximum(i * n - 1, 0), n_pairs * g + p))
        in_specs += [cur, cur, prv, cur, prv]
        if r > DSA_STAGE:
            rows = DSA_UNIT // DSA_STAGE
            stage_scratch += [pltpu.VMEM((DSA_STAGE, rows, LANES), F32),
                              pltpu.VMEM((DSA_STAGE, 2 * rows, LANES), F32),
                              pltpu.VMEM((DSA_STAGE, 2 * rows, LANES), F32),
                              pltpu.VMEM((DSA_STAGE, rows, LANES), F32),
                              pltpu.VMEM((DSA_STAGE, rows, LANES), F32)]
    scratch = ([pltpu.VMEM((2, 2 * Q_BLOCK, 2 * Q_BLOCK), F32)]
               + [pltpu.VMEM((DSA_UNIT, LANES), F32)] * (2 * len(DSA_GROUPS)) + stage_scratch)
    args = []
    for _ in DSA_GROUPS:
        args += [q, k, k, v, v]
    return pl.pallas_call(
        _dsa_kernel,
        out_shape=jax.ShapeDtypeStruct((s, DSA_OUT_W), BF16),
        grid=(s // DSA_UNIT, n_pairs),
        in_specs=in_specs,
        out_specs=pl.BlockSpec((DSA_UNIT, LANES), lambda i, p: (i, p)),
        scratch_shapes=scratch,
        compiler_params=_params(2),
        name="dilated",
    )(*args)


def _tail_kernel(q_ref, kprev_ref, kcur_ref, vprev_ref, vcur_ref, k_hbm, v_hbm,
                 x_ref, gm_ref, wg_ref, bg_ref, yb_ref, yc_ref, wa_ref, wb_ref, wc_ref, wo_ref,
                 gf_ref, w1_ref, w3_ref, w2_ref, o_ref,
                 q2_ref, carry_ref, acc_ref, ya_ref, kblk_ref, vblk_ref, sem):
    n_slots = SB_QBLOCKS_PER_STEP
    step = pl.program_id(0)
    n_tiles = pl.num_programs(0) - 1
    first_qb = step * n_slots
    n_pairs = SB_W // LANES
    lane = lax.broadcasted_iota(jnp.int32, (2 * Q_BLOCK, LANES), 1)
    row = lax.broadcasted_iota(jnp.int32, (2 * Q_BLOCK, LANES), 0)
    own_lanes = (lane < HEAD_DIM) == (row < Q_BLOCK)
    before = lane < (row % Q_BLOCK)
    j = lax.broadcasted_iota(jnp.int32, (2 * Q_BLOCK, 2 * Q_BLOCK), 0) % Q_BLOCK
    s = lax.broadcasted_iota(jnp.int32, (2 * Q_BLOCK, 2 * Q_BLOCK), 1)
    suffix = jnp.where((s >= Q_BLOCK) | (j >= s), 1.0, 0.0).astype(BF16)

    def stack(x):
        return jnp.where(own_lanes, jnp.concatenate([x, x], axis=0), jnp.zeros((), x.dtype))

    pairs = range(n_pairs)
    sls = [slice(p * LANES, (p + 1) * LANES) for p in pairs]
    for slot in range(n_slots):
        for p in pairs:
            q2_ref[slot, p] = stack(q_ref[slot * Q_BLOCK:(slot + 1) * Q_BLOCK, sls[p]])

    def kv_block(handle, p):
        if handle is None:
            return kblk_ref[:, sls[p]], vblk_ref[:, sls[p]]
        k_win, v_win = (kprev_ref, vprev_ref) if handle < n_slots else (kcur_ref, vcur_ref)
        rows = slice((handle % n_slots) * Q_BLOCK, (handle % n_slots + 1) * Q_BLOCK)
        return k_win[rows, sls[p]], v_win[rows, sls[p]]

    def walk(jobs, from_diagonal, last_rows=Q_BLOCK, between=(None, None, None)):
        partial = last_rows < Q_BLOCK
        assert not partial or (from_diagonal and all(len(kbs) > 1 for _, kbs in jobs))
        chains = [(i, b) for i, (_, kbs) in enumerate(jobs) for b in range(len(kbs))]

        def is_partial(i, b):
            return partial and b == len(jobs[i][1]) - 1

        def visiting(x, i, b):
            if is_partial(i, b):
                return jnp.concatenate([x[:last_rows], x[Q_BLOCK:Q_BLOCK + last_rows]], axis=0)
            return x

        z = {}
        for i, (slot, kbs) in enumerate(jobs):
            n_full = len(kbs) - 1 if partial else len(kbs)
            for p in pairs:
                for b in range(0, n_full - 1, 2):
                    keys = jnp.concatenate([kv_block(kbs[b], p)[0], kv_block(kbs[b + 1], p)[0]], axis=0)
                    both = _dot_nt(q2_ref[slot, p], keys)
                    z[i, b, p], z[i, b + 1, p] = both[:, :Q_BLOCK], both[:, Q_BLOCK:]
                for b in list(range(n_full - n_full % 2, n_full)) + list(range(n_full, len(kbs))):
                    z[i, b, p] = _dot_nt(visiting(q2_ref[slot, p], i, b), kv_block(kbs[b], p)[0])
        if between[0] is not None:
            between[0]()
        split = {}
        for i, b in chains:
            for p in pairs:
                if from_diagonal and b == 0:
                    z[i, b, p] = jnp.where(before, z[i, b, p], SB_MASKED_LOGIT)
                nz = -z[i, b, p]
                lf = jnp.minimum(nz, 0.0) - jnp.log(1.0 + jnp.exp(jnp.minimum(z[i, b, p], nz)))
                split[i, b, p] = jnp.concatenate(_split_bf16(lf), axis=1)
        sums = {(i, b, p): _dot(split[i, b, p], suffix) for i, b in chains for p in pairs}
        if between[1] is not None:
            between[1]()
        w2, w_last, tops = {}, {}, []
        for i, (slot, kbs) in enumerate(jobs):
            top = None
            for p in pairs:
                carry = None if from_diagonal else carry_ref[slot, p]
                ws = []
                for b in range(len(kbs)):
                    logw = z[i, b, p] + sums[i, b, p][:, :Q_BLOCK]
                    total = sums[i, b, p][:, Q_BLOCK:]
                    if carry is not None:
                        logw = logw + visiting(carry, i, b)
                        total = total + visiting(carry, i, b)
                    w = jnp.exp(logw).astype(BF16)
                    half = w.shape[0] // 2
                    if is_partial(i, b):
                        w_last[i, p] = jnp.concatenate([w[:half], w[half:]], axis=1)
                        carry = jnp.concatenate([total[:half], carry[last_rows:Q_BLOCK],
                                                 total[half:], carry[Q_BLOCK + last_rows:]], axis=0)
                    else:
                        ws += [w[:half], w[half:]]
                        carry = total
                w2[i, p] = jnp.concatenate(ws, axis=1)
                if not partial:
                    carry_ref[slot, p] = carry
                top = carry if top is None else jnp.maximum(top, carry)
            tops.append(top)
        for i, (slot, kbs) in enumerate(jobs):
            n_full = len(kbs) - 1 if partial else len(kbs)
            for p in pairs:
                v2 = jnp.concatenate([stack(kv_block(kbs[b], p)[1]) for b in range(n_full)], axis=0)
                pv = _dot(w2[i, p], v2)
                if partial:
                    extra = _dot(w_last[i, p], stack(kv_block(kbs[n_full], p)[1]))
                    pv = jnp.concatenate([pv[:last_rows] + extra, pv[last_rows:]], axis=0)
                acc_ref[slot, p] = pv if from_diagonal else acc_ref[slot, p] + pv
        if between[2] is not None:
            between[2]()
        return tops

    dense = {}

    def merge_stage():
        x = x_ref[...]
        d = x.shape[1]
        h = _rms(x, gm_ref[...]).astype(BF16)
        ya = ya_ref[(step + 1) % 2]
        merged = None
        for i, (y, w_ref) in enumerate(((ya, wa_ref), (yb_ref[...], wb_ref), (yc_ref[...], wc_ref))):
            logits = _dot(h, wg_ref[:, i * d:(i + 1) * d]) + bg_ref[:, i * d:(i + 1) * d]
            term = _dot(y, w_ref[...]) / (1.0 + jnp.exp(-logits))
            merged = term if merged is None else merged + term
        dense["x"] = x + _dot(merged.astype(BF16), wo_ref[...])
        dense["h"] = _rms(dense["x"], gf_ref[...]).astype(BF16)

    def up_stage():
        a = _dot(dense["h"], w1_ref[...])
        b = _dot(dense["h"], w3_ref[...])
        dense["act"] = (a * b / (1.0 + jnp.exp(-a))).astype(BF16)

    def down_stage():
        o_ref[...] = dense["x"] + 0.5 * _dot(dense["act"], w2_ref[...])

    def fetch(kb):
        rows = pl.ds(pl.multiple_of(kb * Q_BLOCK, Q_BLOCK), Q_BLOCK)
        copies = [pltpu.make_async_copy(k_hbm.at[rows, :], kblk_ref, sem.at[0]),
                  pltpu.make_async_copy(v_hbm.at[rows, :], vblk_ref, sem.at[1])]
        for c in copies:
            c.start()
        for c in copies:
            c.wait()

    def cond(state):
        kb, top = state
        return jnp.logical_and(kb >= 0, top > SB_LOG_CUTOFF)

    def general():
        for slot in range(n_slots):
            def visit(kb, from_diagonal, slot=slot):
                fetch(kb)
                return jnp.max(walk([(slot, [None])], from_diagonal)[0])

            def body(state, visit=visit):
                kb, _ = state
                return kb - 1, visit(kb, False)

            qb = first_qb + slot
            lax.while_loop(cond, body, (qb - 1, visit(qb, True)))

    n_fused = SB_FUSED_BLOCKS
    assert n_fused - 1 <= n_slots

    def usual():
        jobs = [(slot, [n_slots + slot - b for b in range(n_fused)]) for slot in range(n_slots)]
        tops = walk(jobs, True, SB_LAST_BLOCK_ROWS, (merge_stage, up_stage, down_stage))
        return jnp.max(functools.reduce(jnp.maximum, tops))

    def edge():
        @pl.when(step == n_tiles)
        def _():
            merge_stage()
            up_stage()
            down_stage()
        return jnp.float32(jnp.inf)

    fused_top = lax.cond(jnp.logical_and(step >= 1, step < n_tiles), usual, edge)
    lax.cond(jnp.logical_and(fused_top > SB_LOG_CUTOFF, step < n_tiles), general, lambda: None)
    for slot in range(n_slots):
        for p in pairs:
            ya_ref[step % 2, slot * Q_BLOCK:(slot + 1) * Q_BLOCK, sls[p]] = acc_ref[slot, p].astype(BF16)


def _tail(q, k, v, x, gm, wg, bg, yb, yc, wa, wb, wc, wo, gf, w1, w3, w2):
    s, w = q.shape
    d = x.shape[1]
    n_pairs = w // LANES
    n_slots = SB_QBLOCKS_PER_STEP
    rows = n_slots * Q_BLOCK
    n_tiles = s // rows
    this = lambda i: (jnp.minimum(i, n_tiles - 1), 0)
    last = lambda i: (jnp.maximum(i - 1, 0), 0)
    hbm = pl.BlockSpec(memory_space=pl.ANY)
    return pl.pallas_call(
        _tail_kernel,
        out_shape=jax.ShapeDtypeStruct((s, d), F32),
        grid=(n_tiles + 1,),
        in_specs=[pl.BlockSpec((rows, w), this), pl.BlockSpec((rows, w), last), pl.BlockSpec((rows, w), this),
                  pl.BlockSpec((rows, w), last), pl.BlockSpec((rows, w), this), hbm, hbm,
                  pl.BlockSpec((rows, d), last), _resident((1, d)), _resident(wg.shape), _resident(bg.shape),
                  pl.BlockSpec((rows, DSA_OUT_W), last), pl.BlockSpec((rows, MEM_W), last),
                  _resident(wa.shape), _resident(wb.shape), _resident(wc.shape), _resident(wo.shape),
                  _resident((1, d)), _resident(w1.shape), _resident(w3.shape), _resident(w2.shape)],
        out_specs=pl.BlockSpec((rows, d), last),
        scratch_shapes=[pltpu.VMEM((n_slots, n_pairs, 2 * Q_BLOCK, LANES), BF16),
                        pltpu.VMEM((n_slots, n_pairs, 2 * Q_BLOCK, LANES), F32),
                        pltpu.VMEM((n_slots, n_pairs, Q_BLOCK, LANES), F32),
                        pltpu.VMEM((2, rows, w), BF16),
                        pltpu.VMEM((Q_BLOCK, w), BF16), pltpu.VMEM((Q_BLOCK, w), BF16),
                        pltpu.SemaphoreType.DMA((2,))],
        compiler_params=_params(1),
        name="tail",
    )(q, k, k, v, v, k, v, x, gm, wg, bg, yb, yc, wa, wb, wc, wo, gf, w1, w3, w2)


def _rope_tables(s):
    half = HEAD_DIM // 2
    inv_freq = jnp.power(ROPE_THETA, -jnp.arange(half, dtype=F32) / half)
    ang = inv_freq[:, None] * jnp.arange(s).astype(F32)[None, :]
    return jnp.cos(ang), jnp.sin(ang)


def _layer(x, mem, p):
    s = x.shape[0]
    bf = lambda w: w.astype(BF16)
    vec = lambda v: v.reshape(1, -1)
    heads = lambda v, n: jnp.tile(v, n).reshape(1, -1)
    cos, sin = _rope_tables(s)

    later = ("w_in", "w_gate", "w_branch_sb", "w_branch_dsa", "w_branch_mem", "w_out",
             "ffn2_w1", "ffn2_w3", "ffn2_w2")
    x, casted = _ffn(x, vec(p["ffn1_norm"]), bf(p["ffn1_w1"]), bf(p["ffn1_w3"]), bf(p["ffn1_w2"]),
                     cast=[p[name] for name in later])
    w = dict(zip(later, casted))
    km, vm = _memkv(mem, vec(p["mem_norm"]), bf(p["w_mem_kv"]), heads(p["kn_mem"], MEM_HEADS))
    n_dsa = DSA_W // HEAD_DIM
    qa, ka, va, qb, kb, vb, yc = _proj(
        x, vec(p["mix_norm"]), w["w_in"], cos, sin,
        heads(p["qn_dsa"], n_dsa), heads(p["kn_dsa"], n_dsa), heads(p["qn_mem"], MEM_HEADS), km, vm)
    yb = _dilated(qb, kb, vb)
    return _tail(qa, ka, va, x, vec(p["mix_norm"]), w["w_gate"], vec(p["b_gate"]), yb, yc,
                 w["w_branch_sb"], w["w_branch_dsa"], w["w_branch_mem"], w["w_out"],
                 vec(p["ffn2_norm"]), w["ffn2_w1"], w["ffn2_w3"], w["ffn2_w2"])


_PARAM_NAMES = ("ffn1_norm", "ffn1_w1", "ffn1_w3", "ffn1_w2", "mix_norm", "mem_norm", "w_in", "w_mem_kv",
                "qn_dsa", "kn_dsa", "qn_mem", "kn_mem", "w_branch_sb", "w_branch_dsa", "w_branch_mem",
                "w_gate", "b_gate", "w_out", "ffn2_norm", "ffn2_w1", "ffn2_w3", "ffn2_w2")


def kernel(x, mem, ffn1_norm, ffn1_w1, ffn1_w3, ffn1_w2, mix_norm, mem_norm, w_in, w_mem_kv, qn_dsa, kn_dsa, qn_mem, kn_mem, w_branch_sb, w_branch_dsa, w_branch_mem, w_gate, b_gate, w_out, ffn2_norm, ffn2_w1, ffn2_w3, ffn2_w2):
    stacked = dict(zip(_PARAM_NAMES, (ffn1_norm, ffn1_w1, ffn1_w3, ffn1_w2, mix_norm, mem_norm, w_in,
                                      w_mem_kv, qn_dsa, kn_dsa, qn_mem, kn_mem, w_branch_sb, w_branch_dsa,
                                      w_branch_mem, w_gate, b_gate, w_out, ffn2_norm, ffn2_w1, ffn2_w3,
                                      ffn2_w2)))
    depth = ffn1_norm.shape[0]
    outs = []
    for b in range(x.shape[0]):
        xb = x[b]
        for l in range(depth):
            xb = _layer(xb, mem[b], {k: v[l] for k, v in stacked.items()})
        outs.append(xb)
    return jnp.stack(outs)
```

```python
import functools

import jax
import jax.numpy as jnp
from jax import lax
from jax.experimental import pallas as pl
from jax.experimental.pallas import tpu as pltpu

F32 = jnp.float32
BF16 = jnp.bfloat16

HEAD_DIM = 64
SB_HEADS = 8
DSA_GROUPS = ((128, 1), (512, 4), (2048, 16))
DSA_HEADS_PER_GROUP = 4
MEM_HEADS = 4
ROPE_THETA = 10000.0
NORM_EPS = 1e-6
Q_BLOCK = 128
SB_W = SB_HEADS * HEAD_DIM
DSA_W = DSA_HEADS_PER_GROUP * len(DSA_GROUPS) * HEAD_DIM
DSA_OUT_W = DSA_HEADS_PER_GROUP * HEAD_DIM
MEM_W = MEM_HEADS * HEAD_DIM
QK_SCALE = HEAD_DIM ** -0.5

LANES = 128
MXU_WIDTH = 256
DSA_UNIT = Q_BLOCK * max(r for _, r in DSA_GROUPS)
DSA_UNROLL = 16
DSA_STAGE = 4
ROW_TILE = 512
VMEM_LIMIT = 62 * 1024 * 1024
CAST_SLABS = 16
SB_LOG_CUTOFF = -104.0
SB_MASKED_LOGIT = -1e30
SB_FUSED_BLOCKS = 3
SB_QBLOCKS_PER_STEP = 4
SB_LAST_BLOCK_ROWS = 48


def _resident(shape):
    zeros = (0,) * len(shape)
    return pl.BlockSpec(shape, lambda *_: zeros, pipeline_mode=pl.Buffered(1))


def _params(n_axes):
    return pltpu.CompilerParams(dimension_semantics=("arbitrary",) * n_axes,
                                vmem_limit_bytes=VMEM_LIMIT)


def _rms(x, g):
    return x * lax.rsqrt(jnp.mean(x * x, axis=-1, keepdims=True) + NORM_EPS) * g


def _dot(a, b):
    return jnp.dot(a, b, preferred_element_type=F32)


def _dot_nt(a, b):
    return lax.dot_general(a, b, (((1,), (1,)), ((), ())), preferred_element_type=F32)


def _split_bf16(x):
    hi = x.astype(BF16)
    lo = (x - hi.astype(F32)).astype(BF16)
    return hi, lo


def _head_norm(x, g):
    n = x.shape[-1]
    w = min(n, MXU_WIDTH)
    r = lax.broadcasted_iota(jnp.int32, (w, w), 0) // HEAD_DIM
    c = lax.broadcasted_iota(jnp.int32, (w, w), 1) // HEAD_DIM
    bd = jnp.where(r == c, 1.0, 0.0).astype(BF16)
    hi, lo = _split_bf16(x * x)
    ms = jnp.concatenate([_dot(hi[:, j:j + w], bd) + _dot(lo[:, j:j + w], bd) for j in range(0, n, w)],
                         axis=1) * (1.0 / HEAD_DIM)
    return x * lax.rsqrt(ms + NORM_EPS) * g


def _ffn_kernel(x_ref, g_ref, w1_ref, w3_ref, w2_ref, *refs):
    n_cast = (len(refs) - 1) // 2
    o_ref = refs[n_cast]
    x = x_ref[...]
    h = _rms(x, g_ref[...]).astype(BF16)
    a = _dot(h, w1_ref[...])
    b = _dot(h, w3_ref[...])
    act = (a * b / (1.0 + jnp.exp(-a))).astype(BF16)
    o_ref[...] = x + 0.5 * _dot(act, w2_ref[...])
    for src, dst in zip(refs[:n_cast], refs[n_cast + 1:]):
        dst[...] = src[...].astype(BF16)


def _ffn(x, g, w1, w3, w2, cast=()):
    s, d = x.shape
    f = w1.shape[1]
    steps = s // ROW_TILE
    row = lambda i: (i, 0)
    n_slabs = min(CAST_SLABS, steps)
    per_slab = steps // n_slabs
    slab = lambda i: (i // per_slab, 0)
    slabs = [pl.BlockSpec((w.shape[0] // n_slabs, w.shape[1]), slab) for w in cast]
    out = pl.pallas_call(
        _ffn_kernel,
        out_shape=(jax.ShapeDtypeStruct((s, d), F32), *[jax.ShapeDtypeStruct(w.shape, BF16) for w in cast]),
        grid=(steps,),
        in_specs=[pl.BlockSpec((ROW_TILE, d), row), _resident((1, d)),
                  _resident((d, f)), _resident((d, f)), _resident((f, d)), *slabs],
        out_specs=(pl.BlockSpec((ROW_TILE, d), row), *slabs),
        compiler_params=_params(1),
        name="ffn",
    )(x, g, w1, w3, w2, *cast)
    return out[0], list(out[1:])


def _memkv_kernel(mem_ref, g_ref, w_ref, kn_ref, k_ref, v_ref):
    h = _rms(mem_ref[...], g_ref[...]).astype(BF16)
    kv = _dot(h, w_ref[...])
    k_ref[...] = _head_norm(kv[:, :MEM_W], kn_ref[...]).astype(BF16)
    v_ref[...] = kv[:, MEM_W:].astype(BF16)


def _memkv(mem, g, w, kn):
    m = mem.shape[0]
    out = jax.ShapeDtypeStruct((m, MEM_W), BF16)
    return pl.pallas_call(_memkv_kernel, out_shape=(out, out), name="memkv",
                          compiler_params=pltpu.CompilerParams(vmem_limit_bytes=VMEM_LIMIT),
                          )(mem, g, w, kn)


def _rope(x, cos, sin_signed):
    lane = lax.broadcasted_iota(jnp.int32, (x.shape[0], LANES), 1)
    first_half = (lane % HEAD_DIM) < (HEAD_DIM // 2)
    out = []
    for j in range(x.shape[1] // LANES):
        xs = x[:, j * LANES:(j + 1) * LANES]
        partner = jnp.where(first_half, pltpu.roll(xs, LANES - HEAD_DIM // 2, 1),
                            pltpu.roll(xs, HEAD_DIM // 2, 1))
        out.append(xs * cos + partner * sin_signed)
    return jnp.concatenate(out, axis=1)


def _rope_lanes(cos_half, sin_half):
    half = HEAD_DIM // 2
    f = lax.broadcasted_iota(jnp.int32, (half, LANES), 0)
    lane = lax.broadcasted_iota(jnp.int32, (half, LANES), 1)
    hit = (lane % half) == f
    spread = jnp.where(hit, 1.0, 0.0).astype(BF16)
    signed = jnp.where(hit, jnp.where((lane % HEAD_DIM) < half, -1.0, 1.0), 0.0).astype(BF16)
    c_hi, c_lo = _split_bf16(cos_half)
    s_hi, s_lo = _split_bf16(sin_half)
    return _dot(c_hi, spread) + _dot(c_lo, spread), _dot(s_hi, signed) + _dot(s_lo, signed)


def _proj_kernel(x_ref, g_ref, w_ref, cos_ref, sin_ref, qn_d_ref, kn_d_ref, qn_m_ref, km_ref, vm_ref,
                 qa_ref, ka_ref, va_ref, qb_ref, kb_ref, vb_ref, yc_ref):
    h = _rms(x_ref[...], g_ref[...]).astype(BF16)
    cos, sin_signed = _rope_lanes(cos_ref[...].T, sin_ref[...].T)

    def cols(lo, width):
        return _dot(h, w_ref[:, lo:lo + width])

    qa_ref[...] = (cols(0, SB_W) * QK_SCALE).astype(BF16)
    ka_ref[...] = cols(SB_W, SB_W).astype(BF16)
    va_ref[...] = cols(2 * SB_W, SB_W).astype(BF16)
    base = 3 * SB_W
    qb_ref[...] = _rope(_head_norm(cols(base, DSA_W), qn_d_ref[...]), cos, sin_signed) * QK_SCALE
    kb_ref[...] = _rope(_head_norm(cols(base + DSA_W, DSA_W), kn_d_ref[...]), cos, sin_signed)
    vb_ref[...] = cols(base + 2 * DSA_W, DSA_W)

    qc = _head_norm(cols(base + 3 * DSA_W, MEM_W), qn_m_ref[...]) * QK_SCALE
    lane = lax.broadcasted_iota(jnp.int32, (qc.shape[0], LANES), 1)
    for j in range(MEM_W // LANES):
        sl = slice(j * LANES, (j + 1) * LANES)
        qs, km, vm = qc[:, sl], km_ref[:, sl], vm_ref[:, sl]
        outs = []
        for half in range(2):
            in_head = (lane < HEAD_DIM) == (half == 0)
            sc = _dot_nt(jnp.where(in_head, qs, 0.0).astype(BF16), km)
            p = jnp.exp(sc - jnp.max(sc, axis=-1, keepdims=True))
            outs.append(_dot(p.astype(BF16), vm) / jnp.sum(p, axis=-1, keepdims=True))
        yc_ref[:, sl] = jnp.where(lane < HEAD_DIM, outs[0], outs[1]).astype(BF16)


def _proj(x, g, w_in, cos, sin_signed, qn_d, kn_d, qn_m, km, vm):
    s, d = x.shape
    row = lambda i: (i, 0)
    tile = lambda w: pl.BlockSpec((ROW_TILE, w), row)
    sb = jax.ShapeDtypeStruct((s, SB_W), BF16)
    dsa = jax.ShapeDtypeStruct((s, DSA_W), F32)
    table = pl.BlockSpec((HEAD_DIM // 2, ROW_TILE), lambda i: (0, i))
    return pl.pallas_call(
        _proj_kernel,
        out_shape=(sb, sb, sb, dsa, dsa, dsa, jax.ShapeDtypeStruct((s, MEM_W), BF16)),
        grid=(s // ROW_TILE,),
        in_specs=[tile(d), _resident((1, d)), _resident(w_in.shape), table, table,
                  _resident((1, DSA_W)), _resident((1, DSA_W)), _resident((1, MEM_W)),
                  _resident(km.shape), _resident(vm.shape)],
        out_specs=(tile(SB_W), tile(SB_W), tile(SB_W), tile(DSA_W), tile(DSA_W), tile(DSA_W),
                   tile(MEM_W)),
        compiler_params=_params(1),
        name="proj",
    )(x, g, w_in, cos, sin_signed, qn_d, kn_d, qn_m, km, vm)


def _dsa_kernel(*refs):
    n_g = len(DSA_GROUPS)
    ins = [refs[5 * g:5 * g + 5] for g in range(n_g)]
    o_ref = refs[5 * n_g]
    bias_ref = refs[5 * n_g + 1]
    out_scratch = refs[5 * n_g + 2:5 * n_g + 2 + 2 * n_g]
    stage_scratch = refs[5 * n_g + 2 + 2 * n_g:]
    step = pl.program_id(0)
    lane = lax.broadcasted_iota(jnp.int32, (Q_BLOCK, LANES), 1)
    head0 = lane < HEAD_DIM

    def own_lanes(rows):
        ln = lax.broadcasted_iota(jnp.int32, (rows, LANES), 1)
        rw = lax.broadcasted_iota(jnp.int32, (rows, LANES), 0)
        return (ln < HEAD_DIM) == (rw < rows // 2)

    own_q, own_v = own_lanes(2 * Q_BLOCK), own_lanes(4 * Q_BLOCK)
    qi = lax.broadcasted_iota(jnp.int32, (2 * Q_BLOCK, 2 * Q_BLOCK), 0) % Q_BLOCK
    kj = lax.broadcasted_iota(jnp.int32, (2 * Q_BLOCK, 2 * Q_BLOCK), 1)
    dist = Q_BLOCK + qi - kj
    in_band = (dist >= 0) & (dist <= Q_BLOCK)
    bias_ref[0] = jnp.where(in_band, 0.0, -jnp.inf)
    bias_ref[1] = jnp.where(in_band & (kj >= Q_BLOCK), 0.0, -jnp.inf)
    seq_start = jnp.where(step == 0, 1, 0)

    def attend(qs, ks, vs, biases):
        units = range(len(qs))
        sc, v2 = [], []
        for j in units:
            q2 = jnp.where(own_q, jnp.concatenate([qs[j], qs[j]], axis=0), 0.0).astype(BF16)
            v = vs[j].astype(BF16)
            v2.append(jnp.where(own_v, jnp.concatenate([v, v], axis=0), jnp.zeros((), BF16)))
            sc.append(_dot_nt(q2, ks[j].astype(BF16)))
        p2, m, den = [], [], []
        for j in units:
            s_j = sc[j] + biases[j]
            m.append(jnp.max(s_j, axis=-1, keepdims=True))
            p = jnp.exp(s_j - m[j])
            den.append(jnp.sum(p, axis=-1, keepdims=True))
            p = p.astype(BF16)
            p2.append(jnp.concatenate([p[:Q_BLOCK], p[Q_BLOCK:]], axis=1))
        pv = [_dot(p2[j], v2[j]) for j in units]
        outs = []
        for j in units:
            inv = 1.0 / den[j]
            lse = m[j] + jnp.log(den[j])
            outs.append((pv[j] * jnp.where(head0, inv[:Q_BLOCK], inv[Q_BLOCK:]),
                         jnp.where(head0, lse[:Q_BLOCK], lse[Q_BLOCK:])))
        return outs

    n_trips = DSA_UNIT // Q_BLOCK // DSA_UNROLL
    staged = 0
    for g, (window, r) in enumerate(DSA_GROUPS):
        assert window // r == Q_BLOCK
        q_ref, k_ref, kp_ref, v_ref, vp_ref = ins[g]
        og_ref, lg_ref = out_scratch[2 * g:2 * g + 2]
        prev = Q_BLOCK * r

        if r <= DSA_STAGE:
            assert DSA_UNROLL % r == 0

            def trip(t, first, r=r, prev=prev, q_ref=q_ref, k_ref=k_ref, kp_ref=kp_ref, v_ref=v_ref,
                     vp_ref=vp_ref, og_ref=og_ref, lg_ref=lg_ref):
                qs, ks, vs, biases, rows = [], [], [], [], []
                for j in range(DSA_UNROLL):
                    sub, c = t * (DSA_UNROLL // r) + j // r, j % r
                    rows.append(pl.ds(sub * prev + c, Q_BLOCK, stride=r))
                    qs.append(q_ref[rows[j], :])
                    if first and sub == 0:
                        half = pl.ds(c, Q_BLOCK, stride=r)
                        ks.append(jnp.concatenate([kp_ref[half, :], k_ref[half, :]], axis=0))
                        vs.append(jnp.concatenate([vp_ref[half, :], v_ref[half, :]], axis=0))
                        biases.append(bias_ref[seq_start])
                    else:
                        both = pl.ds((sub - 1) * prev + c, 2 * Q_BLOCK, stride=r)
                        ks.append(k_ref[both, :])
                        vs.append(v_ref[both, :])
                        biases.append(bias_ref[0])
                for j, (o, lse) in enumerate(attend(qs, ks, vs, biases)):
                    og_ref[rows[j], :] = o
                    lg_ref[rows[j], :] = lse

            trip(0, True)

            def later_trip(t, carry, trip=trip):
                trip(t, False)
                return carry

            lax.fori_loop(1, n_trips, later_trip, 0)
        else:
            inner = r // DSA_STAGE
            per_trip = DSA_UNROLL // DSA_STAGE
            assert inner <= DSA_STAGE and prev == DSA_UNIT and DSA_UNROLL % DSA_STAGE == 0 and inner % per_trip == 0
            qs_ref, ks_ref, vs_ref, os_ref, ls_ref = stage_scratch[5 * staged:5 * staged + 5]
            staged += 1
            half = Q_BLOCK * inner
            for c in range(DSA_STAGE):
                coarse = pl.ds(c, half, stride=DSA_STAGE)
                qs_ref[c] = q_ref[coarse, :]
                ks_ref[c, :half] = kp_ref[coarse, :]
                ks_ref[c, half:] = k_ref[coarse, :]
                vs_ref[c, :half] = vp_ref[coarse, :]
                vs_ref[c, half:] = v_ref[coarse, :]

            def staged_trip(t, carry, inner=inner, per_trip=per_trip, qs_ref=qs_ref, ks_ref=ks_ref,
                            vs_ref=vs_ref, os_ref=os_ref, ls_ref=ls_ref):
                bias = bias_ref[seq_start]
                spots = [(c, t * per_trip + f) for f in range(per_trip) for c in range(DSA_STAGE)]
                fine_q = lambda fine: pl.ds(fine, Q_BLOCK, stride=inner)
                fine_k = lambda fine: pl.ds(fine, 2 * Q_BLOCK, stride=inner)
                outs = attend([qs_ref.at[c][fine_q(fine), :] for c, fine in spots],
                              [ks_ref.at[c][fine_k(fine), :] for c, fine in spots],
                              [vs_ref.at[c][fine_k(fine), :] for c, fine in spots],
                              [bias] * len(spots))
                for (c, fine), (o, lse) in zip(spots, outs):
                    os_ref.at[c][fine_q(fine), :] = o
                    ls_ref.at[c][fine_q(fine), :] = lse
                return carry

            lax.fori_loop(0, n_trips, staged_trip, 0)
            for c in range(DSA_STAGE):
                coarse = pl.ds(c, half, stride=DSA_STAGE)
                og_ref[coarse, :] = os_ref[c]
                lg_ref[coarse, :] = ls_ref[c]

    lse = [out_scratch[2 * g + 1][...] for g in range(n_g)]
    top = functools.reduce(jnp.maximum, lse)
    e = [jnp.exp(l - top) for l in lse]
    num = sum(e[g] * out_scratch[2 * g][...] for g in range(n_g))
    o_ref[...] = (num / sum(e)).astype(o_ref.dtype)


def _dilated(q, k, v):
    s = q.shape[0]
    n_pairs = DSA_OUT_W // LANES
    in_specs, stage_scratch = [], []
    for g, (_, r) in enumerate(DSA_GROUPS):
        prev = Q_BLOCK * r
        per_unit = DSA_UNIT // prev
        cur = pl.BlockSpec((DSA_UNIT, LANES), lambda i, p, g=g: (i, n_pairs * g + p))
        prv = pl.BlockSpec((prev, LANES),
                           lambda i, p, g=g, n=per_unit: (jnp.maximum(i * n - 1, 0), n_pairs * g + p))
        in_specs += [cur, cur, prv, cur, prv]
        if r > DSA_STAGE:
            rows = DSA_UNIT // DSA_STAGE
            stage_scratch += [pltpu.VMEM((DSA_STAGE, rows, LANES), F32),
                              pltpu.VMEM((DSA_STAGE, 2 * rows, LANES), F32),
                              pltpu.VMEM((DSA_STAGE, 2 * rows, LANES), F32),
                              pltpu.VMEM((DSA_STAGE, rows, LANES), F32),
                              pltpu.VMEM((DSA_STAGE, rows, LANES), F32)]
    scratch = ([pltpu.VMEM((2, 2 * Q_BLOCK, 2 * Q_BLOCK), F32)]
               + [pltpu.VMEM((DSA_UNIT, LANES), F32)] * (2 * len(DSA_GROUPS)) + stage_scratch)
    args = []
    for _ in DSA_GROUPS:
        args += [q, k, k, v, v]
    return pl.pallas_call(
        _dsa_kernel,
        out_shape=jax.ShapeDtypeStruct((s, DSA_OUT_W), BF16),
        grid=(s // DSA_UNIT, n_pairs),
        in_specs=in_specs,
        out_specs=pl.BlockSpec((DSA_UNIT, LANES), lambda i, p: (i, p)),
        scratch_shapes=scratch,
        compiler_params=_params(2),
        name="dilated",
    )(*args)


def _tail_kernel(q_ref, kprev_ref, kcur_ref, vprev_ref, vcur_ref, k_hbm, v_hbm,
                 x_ref, gm_ref, wg_ref, bg_ref, yb_ref, yc_ref, wa_ref, wb_ref, wc_ref, wo_ref,
                 gf_ref, w1_ref, w3_ref, w2_ref, o_ref,
                 q2_ref, carry_ref, acc_ref, ya_ref, kblk_ref, vblk_ref, sem):
    n_slots = SB_QBLOCKS_PER_STEP
    step = pl.program_id(0)
    n_tiles = pl.num_programs(0) - 1
    first_qb = step * n_slots
    n_pairs = SB_W // LANES
    lane = lax.broadcasted_iota(jnp.int32, (2 * Q_BLOCK, LANES), 1)
    row = lax.broadcasted_iota(jnp.int32, (2 * Q_BLOCK, LANES), 0)
    own_lanes = (lane < HEAD_DIM) == (row < Q_BLOCK)
    before = lane < (row % Q_BLOCK)
    j = lax.broadcasted_iota(jnp.int32, (2 * Q_BLOCK, 2 * Q_BLOCK), 0) % Q_BLOCK
    s = lax.broadcasted_iota(jnp.int32, (2 * Q_BLOCK, 2 * Q_BLOCK), 1)
    suffix = jnp.where((s >= Q_BLOCK) | (j >= s), 1.0, 0.0).astype(BF16)

    def stack(x):
        return jnp.where(own_lanes, jnp.concatenate([x, x], axis=0), jnp.zeros((), x.dtype))

    pairs = range(n_pairs)
    sls = [slice(p * LANES, (p + 1) * LANES) for p in pairs]
    for slot in range(n_slots):
        for p in pairs:
            q2_ref[slot, p] = stack(q_ref[slot * Q_BLOCK:(slot + 1) * Q_BLOCK, sls[p]])

    def kv_block(handle, p):
        if handle is None:
            return kblk_ref[:, sls[p]], vblk_ref[:, sls[p]]
        k_win, v_win = (kprev_ref, vprev_ref) if handle < n_slots else (kcur_ref, vcur_ref)
        rows = slice((handle % n_slots) * Q_BLOCK, (handle % n_slots + 1) * Q_BLOCK)
        return k_win[rows, sls[p]], v_win[rows, sls[p]]

    def walk(jobs, from_diagonal, last_rows=Q_BLOCK, between=(None, None, None)):
        partial = last_rows < Q_BLOCK
        assert not partial or (from_diagonal and all(len(kbs) > 1 for _, kbs in jobs))
        chains = [(i, b) for i, (_, kbs) in enumerate(jobs) for b in range(len(kbs))]

        def is_partial(i, b):
            return partial and b == len(jobs[i][1]) - 1

        def visiting(x, i, b):
            if is_partial(i, b):
                return jnp.concatenate([x[:last_rows], x[Q_BLOCK:Q_BLOCK + last_rows]], axis=0)
            return x

        z = {}
        for i, (slot, kbs) in enumerate(jobs):
            n_full = len(kbs) - 1 if partial else len(kbs)
            for p in pairs:
                for b in range(0, n_full - 1, 2):
                    keys = jnp.concatenate([kv_block(kbs[b], p)[0], kv_block(kbs[b + 1], p)[0]], axis=0)
                    both = _dot_nt(q2_ref[slot, p], keys)
                    z[i, b, p], z[i, b + 1, p] = both[:, :Q_BLOCK], both[:, Q_BLOCK:]
                for b in list(range(n_full - n_full % 2, n_full)) + list(range(n_full, len(kbs))):
                    z[i, b, p] = _dot_nt(visiting(q2_ref[slot, p], i, b), kv_block(kbs[b], p)[0])
        if between[0] is not None:
            between[0]()
        split = {}
        for i, b in chains:
            for p in pairs:
                if from_diagonal and b == 0:
                    z[i, b, p] = jnp.where(before, z[i, b, p], SB_MASKED_LOGIT)
                nz = -z[i, b, p]
                lf = jnp.minimum(nz, 0.0) - jnp.log(1.0 + jnp.exp(jnp.minimum(z[i, b, p], nz)))
                split[i, b, p] = jnp.concatenate(_split_bf16(lf), axis=1)
        sums = {(i, b, p): _dot(split[i, b, p], suffix) for i, b in chains for p in pairs}
        if between[1] is not None:
            between[1]()
        w2, w_last, tops = {}, {}, []
        for i, (slot, kbs) in enumerate(jobs):
            top = None
            for p in pairs:
                carry = None if from_diagonal else carry_ref[slot, p]
                ws = []
                for b in range(len(kbs)):
                    logw = z[i, b, p] + sums[i, b, p][:, :Q_BLOCK]
                    total = sums[i, b, p][:, Q_BLOCK:]
                    if carry is not None:
                        logw = logw + visiting(carry, i, b)
                        total = total + visiting(carry, i, b)
                    w = jnp.exp(logw).astype(BF16)
                    half = w.shape[0] // 2
                    if is_partial(i, b):
                        w_last[i, p] = jnp.concatenate([w[:half], w[half:]], axis=1)
                        carry = jnp.concatenate([total[:half], carry[last_rows:Q_BLOCK],
                                                 total[half:], carry[Q_BLOCK + last_rows:]], axis=0)
                    else:
                        ws += [w[:half], w[half:]]
                        carry = total
                w2[i, p] = jnp.concatenate(ws, axis=1)
                if not partial:
                    carry_ref[slot, p] = carry
                top = carry if top is None else jnp.maximum(top, carry)
            tops.append(top)
        for i, (slot, kbs) in enumerate(jobs):
            n_full = len(kbs) - 1 if partial else len(kbs)
            for p in pairs:
                v2 = jnp.concatenate([stack(kv_block(kbs[b], p)[1]) for b in range(n_full)], axis=0)
                pv = _dot(w2[i, p], v2)
                if partial:
                    extra = _dot(w_last[i, p], stack(kv_block(kbs[n_full], p)[1]))
                    pv = jnp.concatenate([pv[:last_rows] + extra, pv[last_rows:]], axis=0)
                acc_ref[slot, p] = pv if from_diagonal else acc_ref[slot, p] + pv
        if between[2] is not None:
            between[2]()
        return tops

    dense = {}

    def merge_stage():
        x = x_ref[...]
        d = x.shape[1]
        h = _rms(x, gm_ref[...]).astype(BF16)
        ya = ya_ref[(step + 1) % 2]
        merged = None
        for i, (y, w_ref) in enumerate(((ya, wa_ref), (yb_ref[...], wb_ref), (yc_ref[...], wc_ref))):
            logits = _dot(h, wg_ref[:, i * d:(i + 1) * d]) + bg_ref[:, i * d:(i + 1) * d]
            term = _dot(y, w_ref[...]) / (1.0 + jnp.exp(-logits))
            merged = term if merged is None else merged + term
        dense["x"] = x + _dot(merged.astype(BF16), wo_ref[...])
        dense["h"] = _rms(dense["x"], gf_ref[...]).astype(BF16)

    def up_stage():
        a = _dot(dense["h"], w1_ref[...])
        b = _dot(dense["h"], w3_ref[...])
        dense["act"] = (a * b / (1.0 + jnp.exp(-a))).astype(BF16)

    def down_stage():
        o_ref[...] = dense["x"] + 0.5 * _dot(dense["act"], w2_ref[...])

    def fetch(kb):
        rows = pl.ds(pl.multiple_of(kb * Q_BLOCK, Q_BLOCK), Q_BLOCK)
        copies = [pltpu.make_async_copy(k_hbm.at[rows, :], kblk_ref, sem.at[0]),
                  pltpu.make_async_copy(v_hbm.at[rows, :], vblk_ref, sem.at[1])]
        for c in copies:
            c.start()
        for c in copies:
            c.wait()

    def cond(state):
        kb, top = state
        return jnp.logical_and(kb >= 0, top > SB_LOG_CUTOFF)

    def general():
        for slot in range(n_slots):
            def visit(kb, from_diagonal, slot=slot):
                fetch(kb)
                return jnp.max(walk([(slot, [None])], from_diagonal)[0])

            def body(state, visit=visit):
                kb, _ = state
                return kb - 1, visit(kb, False)

            qb = first_qb + slot
            lax.while_loop(cond, body, (qb - 1, visit(qb, True)))

    n_fused = SB_FUSED_BLOCKS
    assert n_fused - 1 <= n_slots

    def usual():
        jobs = [(slot, [n_slots + slot - b for b in range(n_fused)]) for slot in range(n_slots)]
        tops = walk(jobs, True, SB_LAST_BLOCK_ROWS, (merge_stage, up_stage, down_stage))
        return jnp.max(functools.reduce(jnp.maximum, tops))

    def edge():
        @pl.when(step == n_tiles)
        def _():
            merge_stage()
            up_stage()
            down_stage()
        return jnp.float32(jnp.inf)

    fused_top = lax.cond(jnp.logical_and(step >= 1, step < n_tiles), usual, edge)
    lax.cond(jnp.logical_and(fused_top > SB_LOG_CUTOFF, step < n_tiles), general, lambda: None)
    for slot in range(n_slots):
        for p in pairs:
            ya_ref[step % 2, slot * Q_BLOCK:(slot + 1) * Q_BLOCK, sls[p]] = acc_ref[slot, p].astype(BF16)


def _tail(q, k, v, x, gm, wg, bg, yb, yc, wa, wb, wc, wo, gf, w1, w3, w2):
    s, w = q.shape
    d = x.shape[1]
    n_pairs = w // LANES
    n_slots = SB_QBLOCKS_PER_STEP
    rows = n_slots * Q_BLOCK
    n_tiles = s // rows
    this = lambda i: (jnp.minimum(i, n_tiles - 1), 0)
    last = lambda i: (jnp.maximum(i - 1, 0), 0)
    hbm = pl.BlockSpec(memory_space=pl.ANY)
    return pl.pallas_call(
        _tail_kernel,
        out_shape=jax.ShapeDtypeStruct((s, d), F32),
        grid=(n_tiles + 1,),
        in_specs=[pl.BlockSpec((rows, w), this), pl.BlockSpec((rows, w), last), pl.BlockSpec((rows, w), this),
                  pl.BlockSpec((rows, w), last), pl.BlockSpec((rows, w), this), hbm, hbm,
                  pl.BlockSpec((rows, d), last), _resident((1, d)), _resident(wg.shape), _resident(bg.shape),
                  pl.BlockSpec((rows, DSA_OUT_W), last), pl.BlockSpec((rows, MEM_W), last),
                  _resident(wa.shape), _resident(wb.shape), _resident(wc.shape), _resident(wo.shape),
                  _resident((1, d)), _resident(w1.shape), _resident(w3.shape), _resident(w2.shape)],
        out_specs=pl.BlockSpec((rows, d), last),
        scratch_shapes=[pltpu.VMEM((n_slots, n_pairs, 2 * Q_BLOCK, LANES), BF16),
                        pltpu.VMEM((n_slots, n_pairs, 2 * Q_BLOCK, LANES), F32),
                        pltpu.VMEM((n_slots, n_pairs, Q_BLOCK, LANES), F32),
                        pltpu.VMEM((2, rows, w), BF16),
                        pltpu.VMEM((Q_BLOCK, w), BF16), pltpu.VMEM((Q_BLOCK, w), BF16),
                        pltpu.SemaphoreType.DMA((2,))],
        compiler_params=_params(1),
        name="tail",
    )(q, k, k, v, v, k, v, x, gm, wg, bg, yb, yc, wa, wb, wc, wo, gf, w1, w3, w2)


def _rope_tables(s):
    half = HEAD_DIM // 2
    inv_freq = jnp.power(ROPE_THETA, -jnp.arange(half, dtype=F32) / half)
    ang = inv_freq[:, None] * jnp.arange(s).astype(F32)[None, :]
    return jnp.cos(ang), jnp.sin(ang)


def _layer(x, mem, p):
    s = x.shape[0]
    bf = lambda w: w.astype(BF16)
    vec = lambda v: v.reshape(1, -1)
    heads = lambda v, n: jnp.tile(v, n).reshape(1, -1)
    cos, sin = _rope_tables(s)

    later = ("w_in", "w_gate", "w_branch_sb", "w_branch_dsa", "w_branch_mem", "w_out",
             "ffn2_w1", "ffn2_w3", "ffn2_w2")
    x, casted = _ffn(x, vec(p["ffn1_norm"]), bf(p["ffn1_w1"]), bf(p["ffn1_w3"]), bf(p["ffn1_w2"]),
                     cast=[p[name] for name in later])
    w = dict(zip(later, casted))
    km, vm = _memkv(mem, vec(p["mem_norm"]), bf(p["w_mem_kv"]), heads(p["kn_mem"], MEM_HEADS))
    n_dsa = DSA_W // HEAD_DIM
    qa, ka, va, qb, kb, vb, yc = _proj(
        x, vec(p["mix_norm"]), w["w_in"], cos, sin,
        heads(p["qn_dsa"], n_dsa), heads(p["kn_dsa"], n_dsa), heads(p["qn_mem"], MEM_HEADS), km, vm)
    yb = _dilated(qb, kb, vb)
    return _tail(qa, ka, va, x, vec(p["mix_norm"]), w["w_gate"], vec(p["b_gate"]), yb, yc,
                 w["w_branch_sb"], w["w_branch_dsa"], w["w_branch_mem"], w["w_out"],
                 vec(p["ffn2_norm"]), w["ffn2_w1"], w["ffn2_w3"], w["ffn2_w2"])


_PARAM_NAMES = ("ffn1_norm", "ffn1_w1", "ffn1_w3", "ffn1_w2", "mix_norm", "mem_norm", "w_in", "w_mem_kv",
                "qn_dsa", "kn_dsa", "qn_mem", "kn_mem", "w_branch_sb", "w_branch_dsa", "w_branch_mem",
                "w_gate", "b_gate", "w_out", "ffn2_norm", "ffn2_w1", "ffn2_w3", "ffn2_w2")


def kernel(x, mem, ffn1_norm, ffn1_w1, ffn1_w3, ffn1_w2, mix_norm, mem_norm, w_in, w_mem_kv, qn_dsa, kn_dsa, qn_mem, kn_mem, w_branch_sb, w_branch_dsa, w_branch_mem, w_gate, b_gate, w_out, ffn2_norm, ffn2_w1, ffn2_w3, ffn2_w2):
    stacked = dict(zip(_PARAM_NAMES, (ffn1_norm, ffn1_w1, ffn1_w3, ffn1_w2, mix_norm, mem_norm, w_in,
                                      w_mem_kv, qn_dsa, kn_dsa, qn_mem, kn_mem, w_branch_sb, w_branch_dsa,
                                      w_branch_mem, w_gate, b_gate, w_out, ffn2_norm, ffn2_w1, ffn2_w3,
                                      ffn2_w2)))
    depth = ffn1_norm.shape[0]
    outs = []
    for b in range(x.shape[0]):
        xb = x[b]
        for l in range(depth):
            xb = _layer(xb, mem[b], {k: v[l] for k, v in stacked.items()})
        outs.append(xb)
    return jnp.stack(outs)
```

```python
import functools

import jax
import jax.numpy as jnp
from jax import lax
from jax.experimental import pallas as pl
from jax.experimental.pallas import tpu as pltpu

F32 = jnp.float32
BF16 = jnp.bfloat16

HEAD_DIM = 64
SB_HEADS = 8
DSA_GROUPS = ((128, 1), (512, 4), (2048, 16))
DSA_HEADS_PER_GROUP = 4
MEM_HEADS = 4
ROPE_THETA = 10000.0
NORM_EPS = 1e-6
Q_BLOCK = 128
SB_W = SB_HEADS * HEAD_DIM
DSA_W = DSA_HEADS_PER_GROUP * len(DSA_GROUPS) * HEAD_DIM
DSA_OUT_W = DSA_HEADS_PER_GROUP * HEAD_DIM
MEM_W = MEM_HEADS * HEAD_DIM
QK_SCALE = HEAD_DIM ** -0.5

LANES = 128
MXU_WIDTH = 256
DSA_UNIT = Q_BLOCK * max(r for _, r in DSA_GROUPS)
DSA_UNROLL = 16
DSA_STAGE = 4
ROW_TILE = 512
VMEM_LIMIT = 56 * 1024 * 1024
CAST_SLABS = 16
SB_LOG_CUTOFF = -104.0
SB_MASKED_LOGIT = -1e30
SB_FUSED_BLOCKS = 3
SB_QBLOCKS_PER_STEP = 2
SB_LAST_BLOCK_ROWS = 48


def _resident(shape):
    zeros = (0,) * len(shape)
    return pl.BlockSpec(shape, lambda *_: zeros, pipeline_mode=pl.Buffered(1))


def _params(n_axes):
    return pltpu.CompilerParams(dimension_semantics=("arbitrary",) * n_axes,
                                vmem_limit_bytes=VMEM_LIMIT)


def _rms(x, g):
    return x * lax.rsqrt(jnp.mean(x * x, axis=-1, keepdims=True) + NORM_EPS) * g


def _dot(a, b):
    return jnp.dot(a, b, preferred_element_type=F32)


def _dot_nt(a, b):
    return lax.dot_general(a, b, (((1,), (1,)), ((), ())), preferred_element_type=F32)


def _split_bf16(x):
    hi = x.astype(BF16)
    lo = (x - hi.astype(F32)).astype(BF16)
    return hi, lo


def _head_norm(x, g):
    n = x.shape[-1]
    w = min(n, MXU_WIDTH)
    r = lax.broadcasted_iota(jnp.int32, (w, w), 0) // HEAD_DIM
    c = lax.broadcasted_iota(jnp.int32, (w, w), 1) // HEAD_DIM
    bd = jnp.where(r == c, 1.0, 0.0).astype(BF16)
    hi, lo = _split_bf16(x * x)
    ms = jnp.concatenate([_dot(hi[:, j:j + w], bd) + _dot(lo[:, j:j + w], bd) for j in range(0, n, w)],
                         axis=1) * (1.0 / HEAD_DIM)
    return x * lax.rsqrt(ms + NORM_EPS) * g


def _ffn_kernel(x_ref, g_ref, w1_ref, w3_ref, w2_ref, *refs):
    n_cast = (len(refs) - 1) // 2
    o_ref = refs[n_cast]
    x = x_ref[...]
    h = _rms(x, g_ref[...]).astype(BF16)
    a = _dot(h, w1_ref[...])
    b = _dot(h, w3_ref[...])
    act = (a * b / (1.0 + jnp.exp(-a))).astype(BF16)
    o_ref[...] = x + 0.5 * _dot(act, w2_ref[...])
    for src, dst in zip(refs[:n_cast], refs[n_cast + 1:]):
        dst[...] = src[...].astype(BF16)


def _ffn(x, g, w1, w3, w2, cast=()):
    s, d = x.shape
    f = w1.shape[1]
    steps = s // ROW_TILE
    row = lambda i: (i, 0)
    n_slabs = min(CAST_SLABS, steps)
    per_slab = steps // n_slabs
    slab = lambda i: (i // per_slab, 0)
    slabs = [pl.BlockSpec((w.shape[0] // n_slabs, w.shape[1]), slab) for w in cast]
    out = pl.pallas_call(
        _ffn_kernel,
        out_shape=(jax.ShapeDtypeStruct((s, d), F32), *[jax.ShapeDtypeStruct(w.shape, BF16) for w in cast]),
        grid=(steps,),
        in_specs=[pl.BlockSpec((ROW_TILE, d), row), _resident((1, d)),
                  _resident((d, f)), _resident((d, f)), _resident((f, d)), *slabs],
        out_specs=(pl.BlockSpec((ROW_TILE, d), row), *slabs),
        compiler_params=_params(1),
        name="ffn",
    )(x, g, w1, w3, w2, *cast)
    return out[0], list(out[1:])


def _memkv_kernel(mem_ref, g_ref, w_ref, kn_ref, k_ref, v_ref):
    h = _rms(mem_ref[...], g_ref[...]).astype(BF16)
    kv = _dot(h, w_ref[...])
    k_ref[...] = _head_norm(kv[:, :MEM_W], kn_ref[...]).astype(BF16)
    v_ref[...] = kv[:, MEM_W:].astype(BF16)


def _memkv(mem, g, w, kn):
    m = mem.shape[0]
    out = jax.ShapeDtypeStruct((m, MEM_W), BF16)
    return pl.pallas_call(_memkv_kernel, out_shape=(out, out), name="memkv",
                          compiler_params=pltpu.CompilerParams(vmem_limit_bytes=VMEM_LIMIT),
                          )(mem, g, w, kn)


def _rope(x, cos, sin_signed):
    lane = lax.broadcasted_iota(jnp.int32, (x.shape[0], LANES), 1)
    first_half = (lane % HEAD_DIM) < (HEAD_DIM // 2)
    out = []
    for j in range(x.shape[1] // LANES):
        xs = x[:, j * LANES:(j + 1) * LANES]
        partner = jnp.where(first_half, pltpu.roll(xs, LANES - HEAD_DIM // 2, 1),
                            pltpu.roll(xs, HEAD_DIM // 2, 1))
        out.append(xs * cos + partner * sin_signed)
    return jnp.concatenate(out, axis=1)


def _rope_lanes(cos_half, sin_half):
    half = HEAD_DIM // 2
    f = lax.broadcasted_iota(jnp.int32, (half, LANES), 0)
    lane = lax.broadcasted_iota(jnp.int32, (half, LANES), 1)
    hit = (lane % half) == f
    spread = jnp.where(hit, 1.0, 0.0).astype(BF16)
    signed = jnp.where(hit, jnp.where((lane % HEAD_DIM) < half, -1.0, 1.0), 0.0).astype(BF16)
    c_hi, c_lo = _split_bf16(cos_half)
    s_hi, s_lo = _split_bf16(sin_half)
    return _dot(c_hi, spread) + _dot(c_lo, spread), _dot(s_hi, signed) + _dot(s_lo, signed)


def _proj_kernel(x_ref, g_ref, w_ref, cos_ref, sin_ref, qn_d_ref, kn_d_ref, qn_m_ref, km_ref, vm_ref,
                 qkv_ref, qb_ref, kb_ref, vb_ref, yc_ref):
    h = _rms(x_ref[...], g_ref[...]).astype(BF16)
    cos, sin_signed = _rope_lanes(cos_ref[...].T, sin_ref[...].T)

    def cols(lo, width):
        return _dot(h, w_ref[:, lo:lo + width])

    qkv_ref[:, :SB_W] = (cols(0, SB_W) * QK_SCALE).astype(BF16)
    qkv_ref[:, SB_W:] = cols(SB_W, 2 * SB_W).astype(BF16)
    base = 3 * SB_W
    qb_ref[...] = _rope(_head_norm(cols(base, DSA_W), qn_d_ref[...]), cos, sin_signed) * QK_SCALE
    kb_ref[...] = _rope(_head_norm(cols(base + DSA_W, DSA_W), kn_d_ref[...]), cos, sin_signed)
    vb_ref[...] = cols(base + 2 * DSA_W, DSA_W)

    qc = _head_norm(cols(base + 3 * DSA_W, MEM_W), qn_m_ref[...]) * QK_SCALE
    lane = lax.broadcasted_iota(jnp.int32, (qc.shape[0], LANES), 1)
    for j in range(MEM_W // LANES):
        sl = slice(j * LANES, (j + 1) * LANES)
        qs, km, vm = qc[:, sl], km_ref[:, sl], vm_ref[:, sl]
        outs = []
        for half in range(2):
            in_head = (lane < HEAD_DIM) == (half == 0)
            sc = _dot_nt(jnp.where(in_head, qs, 0.0).astype(BF16), km)
            p = jnp.exp(sc - jnp.max(sc, axis=-1, keepdims=True))
            outs.append(_dot(p.astype(BF16), vm) / jnp.sum(p, axis=-1, keepdims=True))
        yc_ref[:, sl] = jnp.where(lane < HEAD_DIM, outs[0], outs[1]).astype(BF16)


def _proj(x, g, w_in, cos, sin_signed, qn_d, kn_d, qn_m, km, vm):
    s, d = x.shape
    row = lambda i: (i, 0)
    tile = lambda w: pl.BlockSpec((ROW_TILE, w), row)
    dsa = jax.ShapeDtypeStruct((s, DSA_W), F32)
    table = pl.BlockSpec((HEAD_DIM // 2, ROW_TILE), lambda i: (0, i))
    return pl.pallas_call(
        _proj_kernel,
        out_shape=(jax.ShapeDtypeStruct((s, 3 * SB_W), BF16), dsa, dsa, dsa,
                   jax.ShapeDtypeStruct((s, MEM_W), BF16)),
        grid=(s // ROW_TILE,),
        in_specs=[tile(d), _resident((1, d)), _resident(w_in.shape), table, table,
                  _resident((1, DSA_W)), _resident((1, DSA_W)), _resident((1, MEM_W)),
                  _resident(km.shape), _resident(vm.shape)],
        out_specs=(tile(3 * SB_W), tile(DSA_W), tile(DSA_W), tile(DSA_W), tile(MEM_W)),
        compiler_params=_params(1),
        name="proj",
    )(x, g, w_in, cos, sin_signed, qn_d, kn_d, qn_m, km, vm)


def _dsa_kernel(*refs):
    n_g = len(DSA_GROUPS)
    ins = [refs[5 * g:5 * g + 5] for g in range(n_g)]
    o_ref = refs[5 * n_g]
    bias_ref = refs[5 * n_g + 1]
    out_scratch = refs[5 * n_g + 2:5 * n_g + 2 + 2 * n_g]
    stage_scratch = refs[5 * n_g + 2 + 2 * n_g:]
    step = pl.program_id(0)
    lane = lax.broadcasted_iota(jnp.int32, (Q_BLOCK, LANES), 1)
    head0 = lane < HEAD_DIM

    def own_lanes(rows):
        ln = lax.broadcasted_iota(jnp.int32, (rows, LANES), 1)
        rw = lax.broadcasted_iota(jnp.int32, (rows, LANES), 0)
        return (ln < HEAD_DIM) == (rw < rows // 2)

    own_q, own_v = own_lanes(2 * Q_BLOCK), own_lanes(4 * Q_BLOCK)
    qi = lax.broadcasted_iota(jnp.int32, (2 * Q_BLOCK, 2 * Q_BLOCK), 0) % Q_BLOCK
    kj = lax.broadcasted_iota(jnp.int32, (2 * Q_BLOCK, 2 * Q_BLOCK), 1)
    dist = Q_BLOCK + qi - kj
    in_band = (dist >= 0) & (dist <= Q_BLOCK)
    bias_ref[0] = jnp.where(in_band, 0.0, -jnp.inf)
    bias_ref[1] = jnp.where(in_band & (kj >= Q_BLOCK), 0.0, -jnp.inf)
    seq_start = jnp.where(step == 0, 1, 0)

    def attend(qs, ks, vs, biases):
        units = range(len(qs))
        sc, v2 = [], []
        for j in units:
            q2 = jnp.where(own_q, jnp.concatenate([qs[j], qs[j]], axis=0), 0.0).astype(BF16)
            v = vs[j].astype(BF16)
            v2.append(jnp.where(own_v, jnp.concatenate([v, v], axis=0), jnp.zeros((), BF16)))
            sc.append(_dot_nt(q2, ks[j].astype(BF16)))
        p2, m, den = [], [], []
        for j in units:
            s_j = sc[j] + biases[j]
            m.append(jnp.max(s_j, axis=-1, keepdims=True))
            p = jnp.exp(s_j - m[j])
            den.append(jnp.sum(p, axis=-1, keepdims=True))
            p = p.astype(BF16)
            p2.append(jnp.concatenate([p[:Q_BLOCK], p[Q_BLOCK:]], axis=1))
        pv = [_dot(p2[j], v2[j]) for j in units]
        outs = []
        for j in units:
            inv = 1.0 / den[j]
            lse = m[j] + jnp.log(den[j])
            outs.append((pv[j] * jnp.where(head0, inv[:Q_BLOCK], inv[Q_BLOCK:]),
                         jnp.where(head0, lse[:Q_BLOCK], lse[Q_BLOCK:])))
        return outs

    n_trips = DSA_UNIT // Q_BLOCK // DSA_UNROLL
    staged = 0
    for g, (window, r) in enumerate(DSA_GROUPS):
        assert window // r == Q_BLOCK
        q_ref, k_ref, kp_ref, v_ref, vp_ref = ins[g]
        og_ref, lg_ref = out_scratch[2 * g:2 * g + 2]
        prev = Q_BLOCK * r

        if r <= DSA_STAGE:
            assert DSA_UNROLL % r == 0

            def trip(t, first, r=r, prev=prev, q_ref=q_ref, k_ref=k_ref, kp_ref=kp_ref, v_ref=v_ref,
                     vp_ref=vp_ref, og_ref=og_ref, lg_ref=lg_ref):
                qs, ks, vs, biases, rows = [], [], [], [], []
                for j in range(DSA_UNROLL):
                    sub, c = t * (DSA_UNROLL // r) + j // r, j % r
                    rows.append(pl.ds(sub * prev + c, Q_BLOCK, stride=r))
                    qs.append(q_ref[rows[j], :])
                    if first and sub == 0:
                        half = pl.ds(c, Q_BLOCK, stride=r)
                        ks.append(jnp.concatenate([kp_ref[half, :], k_ref[half, :]], axis=0))
                        vs.append(jnp.concatenate([vp_ref[half, :], v_ref[half, :]], axis=0))
                        biases.append(bias_ref[seq_start])
                    else:
                        both = pl.ds((sub - 1) * prev + c, 2 * Q_BLOCK, stride=r)
                        ks.append(k_ref[both, :])
                        vs.append(v_ref[both, :])
                        biases.append(bias_ref[0])
                for j, (o, lse) in enumerate(attend(qs, ks, vs, biases)):
                    og_ref[rows[j], :] = o
                    lg_ref[rows[j], :] = lse

            trip(0, True)

            def later_trip(t, carry, trip=trip):
                trip(t, False)
                return carry

            lax.fori_loop(1, n_trips, later_trip, 0)
        else:
            inner = r // DSA_STAGE
            per_trip = DSA_UNROLL // DSA_STAGE
            assert inner <= DSA_STAGE and prev == DSA_UNIT and DSA_UNROLL % DSA_STAGE == 0 and inner % per_trip == 0
            qs_ref, ks_ref, vs_ref, os_ref, ls_ref = stage_scratch[5 * staged:5 * staged + 5]
            staged += 1
            half = Q_BLOCK * inner
            for c in range(DSA_STAGE):
                coarse = pl.ds(c, half, stride=DSA_STAGE)
                qs_ref[c] = q_ref[coarse, :]
                ks_ref[c, :half] = kp_ref[coarse, :]
                ks_ref[c, half:] = k_ref[coarse, :]
                vs_ref[c, :half] = vp_ref[coarse, :]
                vs_ref[c, half:] = v_ref[coarse, :]

            def staged_trip(t, carry, inner=inner, per_trip=per_trip, qs_ref=qs_ref, ks_ref=ks_ref,
                            vs_ref=vs_ref, os_ref=os_ref, ls_ref=ls_ref):
                bias = bias_ref[seq_start]
                spots = [(c, t * per_trip + f) for f in range(per_trip) for c in range(DSA_STAGE)]
                fine_q = lambda fine: pl.ds(fine, Q_BLOCK, stride=inner)
                fine_k = lambda fine: pl.ds(fine, 2 * Q_BLOCK, stride=inner)
                outs = attend([qs_ref.at[c][fine_q(fine), :] for c, fine in spots],
                              [ks_ref.at[c][fine_k(fine), :] for c, fine in spots],
                              [vs_ref.at[c][fine_k(fine), :] for c, fine in spots],
                              [bias] * len(spots))
                for (c, fine), (o, lse) in zip(spots, outs):
                    os_ref.at[c][fine_q(fine), :] = o
                    ls_ref.at[c][fine_q(fine), :] = lse
                return carry

            lax.fori_loop(0, n_trips, staged_trip, 0)
            for c in range(DSA_STAGE):
                coarse = pl.ds(c, half, stride=DSA_STAGE)
                og_ref[coarse, :] = os_ref[c]
                lg_ref[coarse, :] = ls_ref[c]

    lse = [out_scratch[2 * g + 1][...] for g in range(n_g)]
    top = functools.reduce(jnp.maximum, lse)
    e = [jnp.exp(l - top) for l in lse]
    num = sum(e[g] * out_scratch[2 * g][...] for g in range(n_g))
    o_ref[...] = (num / sum(e)).astype(o_ref.dtype)


def _dilated(q, k, v):
    s = q.shape[0]
    n_pairs = DSA_OUT_W // LANES
    in_specs, stage_scratch = [], []
    for g, (_, r) in enumerate(DSA_GROUPS):
        prev = Q_BLOCK * r
        per_unit = DSA_UNIT // prev
        cur = pl.BlockSpec((DSA_UNIT, LANES), lambda i, p, g=g: (i, n_pairs * g + p))
        prv = pl.BlockSpec((prev, LANES),
                           lambda i, p, g=g, n=per_unit: (jnp.maximum(i * n - 1, 0), n_pairs * g + p))
        in_specs += [cur, cur, prv, cur, prv]
        if r > DSA_STAGE:
            rows = DSA_UNIT // DSA_STAGE
            stage_scratch += [pltpu.VMEM((DSA_STAGE, rows, LANES), F32),
                              pltpu.VMEM((DSA_STAGE, 2 * rows, LANES), F32),
                              pltpu.VMEM((DSA_STAGE, 2 * rows, LANES), F32),
                              pltpu.VMEM((DSA_STAGE, rows, LANES), F32),
                              pltpu.VMEM((DSA_STAGE, rows, LANES), F32)]
    scratch = ([pltpu.VMEM((2, 2 * Q_BLOCK, 2 * Q_BLOCK), F32)]
               + [pltpu.VMEM((DSA_UNIT, LANES), F32)] * (2 * len(DSA_GROUPS)) + stage_scratch)
    args = []
    for _ in DSA_GROUPS:
        args += [q, k, k, v, v]
    return pl.pallas_call(
        _dsa_kernel,
        out_shape=jax.ShapeDtypeStruct((s, DSA_OUT_W), BF16),
        grid=(s // DSA_UNIT, n_pairs),
        in_specs=in_specs,
        out_specs=pl.BlockSpec((DSA_UNIT, LANES), lambda i, p: (i, p)),
        scratch_shapes=scratch,
        compiler_params=_params(2),
        name="dilated",
    )(*args)


def _tail_kernel(qkv_ref, qkv_hbm,
                 x_ref, gm_ref, wg_ref, bg_ref, yb_ref, yc_ref, wa_ref, wb_ref, wc_ref, wo_ref,
                 gf_ref, w1_ref, w3_ref, w2_ref, o_ref,
                 q2_ref, carry_ref, acc_ref, ya_ref, kvprev_ref, kblk_ref, vblk_ref, sem):
    n_slots = SB_QBLOCKS_PER_STEP
    step = pl.program_id(0)
    n_tiles = pl.num_programs(0) - 1
    first_qb = step * n_slots
    n_pairs = SB_W // LANES
    lane = lax.broadcasted_iota(jnp.int32, (2 * Q_BLOCK, LANES), 1)
    row = lax.broadcasted_iota(jnp.int32, (2 * Q_BLOCK, LANES), 0)
    own_lanes = (lane < HEAD_DIM) == (row < Q_BLOCK)
    before = lane < (row % Q_BLOCK)
    j = lax.broadcasted_iota(jnp.int32, (2 * Q_BLOCK, 2 * Q_BLOCK), 0) % Q_BLOCK
    s = lax.broadcasted_iota(jnp.int32, (2 * Q_BLOCK, 2 * Q_BLOCK), 1)
    suffix = jnp.where((s >= Q_BLOCK) | (j >= s), 1.0, 0.0).astype(BF16)

    def stack(x):
        return jnp.where(own_lanes, jnp.concatenate([x, x], axis=0), jnp.zeros((), x.dtype))

    pairs = range(n_pairs)
    sls = [slice(p * LANES, (p + 1) * LANES) for p in pairs]
    for slot in range(n_slots):
        for p in pairs:
            q2_ref[slot, p] = stack(qkv_ref[slot * Q_BLOCK:(slot + 1) * Q_BLOCK, sls[p]])

    def kv_block(handle, p):
        if handle is None:
            return kblk_ref[:, sls[p]], vblk_ref[:, sls[p]]
        rows = slice((handle % n_slots) * Q_BLOCK, (handle % n_slots + 1) * Q_BLOCK)
        k_cols = pl.ds(p * LANES, LANES)
        v_cols = pl.ds(SB_W + p * LANES, LANES)
        if handle < n_slots:
            return kvprev_ref[rows, k_cols], kvprev_ref[rows, v_cols]
        return qkv_ref[rows, pl.ds(SB_W + p * LANES, LANES)], qkv_ref[rows, pl.ds(2 * SB_W + p * LANES, LANES)]

    def walk(jobs, from_diagonal, last_rows=Q_BLOCK, between=(None, None, None)):
        partial = last_rows < Q_BLOCK
        assert not partial or (from_diagonal and all(len(kbs) > 1 for _, kbs in jobs))
        chains = [(i, b) for i, (_, kbs) in enumerate(jobs) for b in range(len(kbs))]

        def is_partial(i, b):
            return partial and b == len(jobs[i][1]) - 1

        def visiting(x, i, b):
            if is_partial(i, b):
                return jnp.concatenate([x[:last_rows], x[Q_BLOCK:Q_BLOCK + last_rows]], axis=0)
            return x

        z = {}
        for i, (slot, kbs) in enumerate(jobs):
            n_full = len(kbs) - 1 if partial else len(kbs)
            for p in pairs:
                for b in range(0, n_full - 1, 2):
                    keys = jnp.concatenate([kv_block(kbs[b], p)[0], kv_block(kbs[b + 1], p)[0]], axis=0)
                    both = _dot_nt(q2_ref[slot, p], keys)
                    z[i, b, p], z[i, b + 1, p] = both[:, :Q_BLOCK], both[:, Q_BLOCK:]
                for b in list(range(n_full - n_full % 2, n_full)) + list(range(n_full, len(kbs))):
                    z[i, b, p] = _dot_nt(visiting(q2_ref[slot, p], i, b), kv_block(kbs[b], p)[0])
        if between[0] is not None:
            between[0]()
        split = {}
        for i, b in chains:
            for p in pairs:
                if from_diagonal and b == 0:
                    z[i, b, p] = jnp.where(before, z[i, b, p], SB_MASKED_LOGIT)
                nz = -z[i, b, p]
                lf = jnp.minimum(nz, 0.0) - jnp.log(1.0 + jnp.exp(jnp.minimum(z[i, b, p], nz)))
                split[i, b, p] = jnp.concatenate(_split_bf16(lf), axis=1)
        sums = {(i, b, p): _dot(split[i, b, p], suffix) for i, b in chains for p in pairs}
        if between[1] is not None:
            between[1]()
        w2, w_last, tops = {}, {}, []
        for i, (slot, kbs) in enumerate(jobs):
            top = None
            for p in pairs:
                carry = None if from_diagonal else carry_ref[slot, p]
                ws = []
                for b in range(len(kbs)):
                    logw = z[i, b, p] + sums[i, b, p][:, :Q_BLOCK]
                    total = sums[i, b, p][:, Q_BLOCK:]
                    if carry is not None:
                        logw = logw + visiting(carry, i, b)
                        total = total + visiting(carry, i, b)
                    w = jnp.exp(logw).astype(BF16)
                    half = w.shape[0] // 2
                    if is_partial(i, b):
                        w_last[i, p] = jnp.concatenate([w[:half], w[half:]], axis=1)
                        carry = jnp.concatenate([total[:half], carry[last_rows:Q_BLOCK],
                                                 total[half:], carry[Q_BLOCK + last_rows:]], axis=0)
                    else:
                        ws += [w[:half], w[half:]]
                        carry = total
                w2[i, p] = jnp.concatenate(ws, axis=1)
                if not partial:
                    carry_ref[slot, p] = carry
                top = carry if top is None else jnp.maximum(top, carry)
            tops.append(top)
        for i, (slot, kbs) in enumerate(jobs):
            n_full = len(kbs) - 1 if partial else len(kbs)
            for p in pairs:
                v2 = jnp.concatenate([stack(kv_block(kbs[b], p)[1]) for b in range(n_full)], axis=0)
                pv = _dot(w2[i, p], v2)
                if partial:
                    extra = _dot(w_last[i, p], stack(kv_block(kbs[n_full], p)[1]))
                    pv = jnp.concatenate([pv[:last_rows] + extra, pv[last_rows:]], axis=0)
                acc_ref[slot, p] = pv if from_diagonal else acc_ref[slot, p] + pv
        if between[2] is not None:
            between[2]()
        return tops

    dense = {}

    def merge_stage():
        x = x_ref[...]
        d = x.shape[1]
        h = _rms(x, gm_ref[...]).astype(BF16)
        ya = ya_ref[(step + 1) % 2]
        merged = None
        for i, (y, w_ref) in enumerate(((ya, wa_ref), (yb_ref[...], wb_ref), (yc_ref[...], wc_ref))):
            logits = _dot(h, wg_ref[:, i * d:(i + 1) * d]) + bg_ref[:, i * d:(i + 1) * d]
            term = _dot(y, w_ref[...]) / (1.0 + jnp.exp(-logits))
            merged = term if merged is None else merged + term
        dense["x"] = x + _dot(merged.astype(BF16), wo_ref[...])
        dense["h"] = _rms(dense["x"], gf_ref[...]).astype(BF16)

    def up_stage():
        a = _dot(dense["h"], w1_ref[...])
        b = _dot(dense["h"], w3_ref[...])
        dense["act"] = (a * b / (1.0 + jnp.exp(-a))).astype(BF16)

    def down_stage():
        o_ref[...] = dense["x"] + 0.5 * _dot(dense["act"], w2_ref[...])

    def fetch(kb):
        rows = pl.ds(pl.multiple_of(kb * Q_BLOCK, Q_BLOCK), Q_BLOCK)
        copies = [pltpu.make_async_copy(qkv_hbm.at[rows, pl.ds(SB_W, SB_W)], kblk_ref, sem.at[0]),
                  pltpu.make_async_copy(qkv_hbm.at[rows, pl.ds(2 * SB_W, SB_W)], vblk_ref, sem.at[1])]
        for c in copies:
            c.start()
        for c in copies:
            c.wait()

    def cond(state):
        kb, top = state
        return jnp.logical_and(kb >= 0, top > SB_LOG_CUTOFF)

    def general():
        for slot in range(n_slots):
            def visit(kb, from_diagonal, slot=slot):
                fetch(kb)
                return jnp.max(walk([(slot, [None])], from_diagonal)[0])

            def body(state, visit=visit):
                kb, _ = state
                return kb - 1, visit(kb, False)

            qb = first_qb + slot
            lax.while_loop(cond, body, (qb - 1, visit(qb, True)))

    n_fused = SB_FUSED_BLOCKS
    assert n_fused - 1 <= n_slots

    def usual():
        jobs = [(slot, [n_slots + slot - b for b in range(n_fused)]) for slot in range(n_slots)]
        tops = walk(jobs, True, SB_LAST_BLOCK_ROWS, (merge_stage, up_stage, down_stage))
        return jnp.max(functools.reduce(jnp.maximum, tops))

    def edge():
        @pl.when(step == n_tiles)
        def _():
            merge_stage()
            up_stage()
            down_stage()
        return jnp.float32(jnp.inf)

    fused_top = lax.cond(jnp.logical_and(step >= 1, step < n_tiles), usual, edge)
    lax.cond(jnp.logical_and(fused_top > SB_LOG_CUTOFF, step < n_tiles), general, lambda: None)
    for slot in range(n_slots):
        for p in pairs:
            ya_ref[step % 2, slot * Q_BLOCK:(slot + 1) * Q_BLOCK, sls[p]] = acc_ref[slot, p].astype(BF16)
    kvprev_ref[...] = qkv_ref[:, SB_W:]


def _tail(qkv, x, gm, wg, bg, yb, yc, wa, wb, wc, wo, gf, w1, w3, w2):
    s, w = qkv.shape[0], qkv.shape[1] // 3
    d = x.shape[1]
    n_pairs = w // LANES
    n_slots = SB_QBLOCKS_PER_STEP
    rows = n_slots * Q_BLOCK
    n_tiles = s // rows
    this = lambda i: (jnp.minimum(i, n_tiles - 1), 0)
    last = lambda i: (jnp.maximum(i - 1, 0), 0)
    hbm = pl.BlockSpec(memory_space=pl.ANY)
    return pl.pallas_call(
        _tail_kernel,
        out_shape=jax.ShapeDtypeStruct((s, d), F32),
        grid=(n_tiles + 1,),
        in_specs=[pl.BlockSpec((rows, 3 * w), this), hbm,
                  pl.BlockSpec((rows, d), last), _resident((1, d)), _resident(wg.shape), _resident(bg.shape),
                  pl.BlockSpec((rows, DSA_OUT_W), last), pl.BlockSpec((rows, MEM_W), last),
                  _resident(wa.shape), _resident(wb.shape), _resident(wc.shape), _resident(wo.shape),
                  _resident((1, d)), _resident(w1.shape), _resident(w3.shape), _resident(w2.shape)],
        out_specs=pl.BlockSpec((rows, d), last),
        scratch_shapes=[pltpu.VMEM((n_slots, n_pairs, 2 * Q_BLOCK, LANES), BF16),
                        pltpu.VMEM((n_slots, n_pairs, 2 * Q_BLOCK, LANES), F32),
                        pltpu.VMEM((n_slots, n_pairs, Q_BLOCK, LANES), F32),
                        pltpu.VMEM((2, rows, w), BF16),
                        pltpu.VMEM((rows, 2 * w), BF16),
                        pltpu.VMEM((Q_BLOCK, w), BF16), pltpu.VMEM((Q_BLOCK, w), BF16),
                        pltpu.SemaphoreType.DMA((2,))],
        compiler_params=_params(1),
        name="tail",
    )(qkv, qkv, x, gm, wg, bg, yb, yc, wa, wb, wc, wo, gf, w1, w3, w2)


def _rope_tables(s):
    half = HEAD_DIM // 2
    inv_freq = jnp.power(ROPE_THETA, -jnp.arange(half, dtype=F32) / half)
    ang = inv_freq[:, None] * jnp.arange(s).astype(F32)[None, :]
    return jnp.cos(ang), jnp.sin(ang)


def _layer(x, mem, p):
    s = x.shape[0]
    bf = lambda w: w.astype(BF16)
    vec = lambda v: v.reshape(1, -1)
    heads = lambda v, n: jnp.tile(v, n).reshape(1, -1)
    cos, sin = _rope_tables(s)

    later = ("w_in", "w_gate", "w_branch_sb", "w_branch_dsa", "w_branch_mem", "w_out",
             "ffn2_w1", "ffn2_w3", "ffn2_w2")
    x, casted = _ffn(x, vec(p["ffn1_norm"]), bf(p["ffn1_w1"]), bf(p["ffn1_w3"]), bf(p["ffn1_w2"]),
                     cast=[p[name] for name in later])
    w = dict(zip(later, casted))
    km, vm = _memkv(mem, vec(p["mem_norm"]), bf(p["w_mem_kv"]), heads(p["kn_mem"], MEM_HEADS))
    n_dsa = DSA_W // HEAD_DIM
    qkv, qb, kb, vb, yc = _proj(
        x, vec(p["mix_norm"]), w["w_in"], cos, sin,
        heads(p["qn_dsa"], n_dsa), heads(p["kn_dsa"], n_dsa), heads(p["qn_mem"], MEM_HEADS), km, vm)
    yb = _dilated(qb, kb, vb)
    return _tail(qkv, x, vec(p["mix_norm"]), w["w_gate"], vec(p["b_gate"]), yb, yc,
                 w["w_branch_sb"], w["w_branch_dsa"], w["w_branch_mem"], w["w_out"],
                 vec(p["ffn2_norm"]), w["ffn2_w1"], w["ffn2_w3"], w["ffn2_w2"])


_PARAM_NAMES = ("ffn1_norm", "ffn1_w1", "ffn1_w3", "ffn1_w2", "mix_norm", "mem_norm", "w_in", "w_mem_kv",
                "qn_dsa", "kn_dsa", "qn_mem", "kn_mem", "w_branch_sb", "w_branch_dsa", "w_branch_mem",
                "w_gate", "b_gate", "w_out", "ffn2_norm", "ffn2_w1", "ffn2_w3", "ffn2_w2")


def kernel(x, mem, ffn1_norm, ffn1_w1, ffn1_w3, ffn1_w2, mix_norm, mem_norm, w_in, w_mem_kv, qn_dsa, kn_dsa, qn_mem, kn_mem, w_branch_sb, w_branch_dsa, w_branch_mem, w_gate, b_gate, w_out, ffn2_norm, ffn2_w1, ffn2_w3, ffn2_w2):
    stacked = dict(zip(_PARAM_NAMES, (ffn1_norm, ffn1_w1, ffn1_w3, ffn1_w2, mix_norm, mem_norm, w_in,
                                      w_mem_kv, qn_dsa, kn_dsa, qn_mem, kn_mem, w_branch_sb, w_branch_dsa,
                                      w_branch_mem, w_gate, b_gate, w_out, ffn2_norm, ffn2_w1, ffn2_w3,
                                      ffn2_w2)))
    depth = ffn1_norm.shape[0]
    outs = []
    for b in range(x.shape[0]):
        xb = x[b]
        for l in range(depth):
            xb = _layer(xb, mem[b], {k: v[l] for k, v in stacked.items()})
        outs.append(xb)
    return jnp.stack(outs)
```

```python
import functools

import jax
import jax.numpy as jnp
from jax import lax
from jax.experimental import pallas as pl
from jax.experimental.pallas import tpu as pltpu

F32 = jnp.float32
BF16 = jnp.bfloat16

HEAD_DIM = 64
SB_HEADS = 8
DSA_GROUPS = ((128, 1), (512, 4), (2048, 16))
DSA_HEADS_PER_GROUP = 4
MEM_HEADS = 4
ROPE_THETA = 10000.0
NORM_EPS = 1e-6
Q_BLOCK = 128
SB_W = SB_HEADS * HEAD_DIM
DSA_W = DSA_HEADS_PER_GROUP * len(DSA_GROUPS) * HEAD_DIM
DSA_OUT_W = DSA_HEADS_PER_GROUP * HEAD_DIM
MEM_W = MEM_HEADS * HEAD_DIM
QK_SCALE = HEAD_DIM ** -0.5

LANES = 128
MXU_WIDTH = 256
DSA_UNIT = Q_BLOCK * max(r for _, r in DSA_GROUPS)
DSA_UNROLL = 16
DSA_STAGE = 4
ROW_TILE = 512
VMEM_LIMIT = 56 * 1024 * 1024
CAST_SLABS = 16
SB_LOG_CUTOFF = -104.0
SB_MASKED_LOGIT = -1e30
SB_FUSED_BLOCKS = 3
SB_QBLOCKS_PER_STEP = 2
SB_LAST_BLOCK_ROWS = 48


def _resident(shape):
    zeros = (0,) * len(shape)
    return pl.BlockSpec(shape, lambda *_: zeros, pipeline_mode=pl.Buffered(1))


def _params(n_axes):
    return pltpu.CompilerParams(dimension_semantics=("arbitrary",) * n_axes,
                                vmem_limit_bytes=VMEM_LIMIT)


def _rms(x, g):
    return x * lax.rsqrt(jnp.mean(x * x, axis=-1, keepdims=True) + NORM_EPS) * g


def _dot(a, b):
    return jnp.dot(a, b, preferred_element_type=F32)


def _dot_nt(a, b):
    return lax.dot_general(a, b, (((1,), (1,)), ((), ())), preferred_element_type=F32)


def _split_bf16(x):
    hi = x.astype(BF16)
    lo = (x - hi.astype(F32)).astype(BF16)
    return hi, lo


def _head_norm(x, g):
    n = x.shape[-1]
    w = min(n, MXU_WIDTH)
    r = lax.broadcasted_iota(jnp.int32, (w, w), 0) // HEAD_DIM
    c = lax.broadcasted_iota(jnp.int32, (w, w), 1) // HEAD_DIM
    bd = jnp.where(r == c, 1.0, 0.0).astype(BF16)
    hi, lo = _split_bf16(x * x)
    ms = jnp.concatenate([_dot(hi[:, j:j + w], bd) + _dot(lo[:, j:j + w], bd) for j in range(0, n, w)],
                         axis=1) * (1.0 / HEAD_DIM)
    return x * lax.rsqrt(ms + NORM_EPS) * g


def _ffn_kernel(x_ref, g_ref, w1_ref, w3_ref, w2_ref, *refs):
    n_cast = (len(refs) - 1) // 2
    o_ref = refs[n_cast]
    x = x_ref[...]
    h = _rms(x, g_ref[...]).astype(BF16)
    a = _dot(h, w1_ref[...])
    b = _dot(h, w3_ref[...])
    act = (a * b / (1.0 + jnp.exp(-a))).astype(BF16)
    o_ref[...] = x + 0.5 * _dot(act, w2_ref[...])
    for src, dst in zip(refs[:n_cast], refs[n_cast + 1:]):
        dst[...] = src[...].astype(BF16)


def _ffn(x, g, w1, w3, w2, cast=()):
    s, d = x.shape
    f = w1.shape[1]
    steps = s // ROW_TILE
    row = lambda i: (i, 0)
    n_slabs = min(CAST_SLABS, steps)
    per_slab = steps // n_slabs
    slab = lambda i: (i // per_slab, 0)
    slabs = [pl.BlockSpec((w.shape[0] // n_slabs, w.shape[1]), slab) for w in cast]
    out = pl.pallas_call(
        _ffn_kernel,
        out_shape=(jax.ShapeDtypeStruct((s, d), F32), *[jax.ShapeDtypeStruct(w.shape, BF16) for w in cast]),
        grid=(steps,),
        in_specs=[pl.BlockSpec((ROW_TILE, d), row), _resident((1, d)),
                  _resident((d, f)), _resident((d, f)), _resident((f, d)), *slabs],
        out_specs=(pl.BlockSpec((ROW_TILE, d), row), *slabs),
        compiler_params=_params(1),
        name="ffn",
    )(x, g, w1, w3, w2, *cast)
    return out[0], list(out[1:])


def _memkv_kernel(mem_ref, g_ref, w_ref, kn_ref, k_ref, v_ref):
    h = _rms(mem_ref[...], g_ref[...]).astype(BF16)
    kv = _dot(h, w_ref[...])
    k_ref[...] = _head_norm(kv[:, :MEM_W], kn_ref[...]).astype(BF16)
    v_ref[...] = kv[:, MEM_W:].astype(BF16)


def _memkv(mem, g, w, kn):
    m = mem.shape[0]
    out = jax.ShapeDtypeStruct((m, MEM_W), BF16)
    return pl.pallas_call(_memkv_kernel, out_shape=(out, out), name="memkv",
                          compiler_params=pltpu.CompilerParams(vmem_limit_bytes=VMEM_LIMIT),
                          )(mem, g, w, kn)


def _rope(x, cos, sin_signed):
    lane = lax.broadcasted_iota(jnp.int32, (x.shape[0], LANES), 1)
    first_half = (lane % HEAD_DIM) < (HEAD_DIM // 2)
    out = []
    for j in range(x.shape[1] // LANES):
        xs = x[:, j * LANES:(j + 1) * LANES]
        partner = jnp.where(first_half, pltpu.roll(xs, LANES - HEAD_DIM // 2, 1),
                            pltpu.roll(xs, HEAD_DIM // 2, 1))
        out.append(xs * cos + partner * sin_signed)
    return jnp.concatenate(out, axis=1)


def _rope_lanes(cos_half, sin_half):
    half = HEAD_DIM // 2
    f = lax.broadcasted_iota(jnp.int32, (half, LANES), 0)
    lane = lax.broadcasted_iota(jnp.int32, (half, LANES), 1)
    hit = (lane % half) == f
    spread = jnp.where(hit, 1.0, 0.0).astype(BF16)
    signed = jnp.where(hit, jnp.where((lane % HEAD_DIM) < half, -1.0, 1.0), 0.0).astype(BF16)
    c_hi, c_lo = _split_bf16(cos_half)
    s_hi, s_lo = _split_bf16(sin_half)
    return _dot(c_hi, spread) + _dot(c_lo, spread), _dot(s_hi, signed) + _dot(s_lo, signed)


def _proj_kernel(x_ref, g_ref, w_ref, cos_ref, sin_ref, qn_d_ref, kn_d_ref, qn_m_ref, km_ref, vm_ref,
                 qkv_ref, qb_ref, kb_ref, vb_ref, yc_ref):
    h = _rms(x_ref[...], g_ref[...]).astype(BF16)
    cos, sin_signed = _rope_lanes(cos_ref[...].T, sin_ref[...].T)

    def cols(lo, width):
        return _dot(h, w_ref[:, lo:lo + width])

    base = 3 * SB_W
    qc_raw = cols(base + 3 * DSA_W, MEM_W)
    qb_raw = cols(base, DSA_W)
    qc = _head_norm(qc_raw, qn_m_ref[...]) * QK_SCALE
    kb_raw = cols(base + DSA_W, DSA_W)

    lane = lax.broadcasted_iota(jnp.int32, (qc.shape[0], LANES), 1)
    groups = [slice(j * LANES, (j + 1) * LANES) for j in range(MEM_W // LANES)]
    scores = [[_dot_nt(jnp.where((lane < HEAD_DIM) == (half == 0), qc[:, sl], 0.0).astype(BF16), km_ref[:, sl])
               for half in range(2)] for sl in groups]
    qb = _head_norm(qb_raw, qn_d_ref[...])
    vb_ref[...] = cols(base + 2 * DSA_W, DSA_W)
    for sl, (sc0, sc1) in zip(groups, scores):
        outs = []
        for sc in (sc0, sc1):
            p = jnp.exp(sc - jnp.max(sc, axis=-1, keepdims=True))
            outs.append(_dot(p.astype(BF16), vm_ref[:, sl]) / jnp.sum(p, axis=-1, keepdims=True))
        yc_ref[:, sl] = jnp.where(lane < HEAD_DIM, outs[0], outs[1]).astype(BF16)
    kb = _head_norm(kb_raw, kn_d_ref[...])
    qb_ref[...] = _rope(qb, cos, sin_signed) * QK_SCALE
    kb_ref[...] = _rope(kb, cos, sin_signed)
    qkv_ref[:, :SB_W] = (cols(0, SB_W) * QK_SCALE).astype(BF16)
    qkv_ref[:, SB_W:] = cols(SB_W, 2 * SB_W).astype(BF16)


def _proj(x, g, w_in, cos, sin_signed, qn_d, kn_d, qn_m, km, vm):
    s, d = x.shape
    row = lambda i: (i, 0)
    tile = lambda w: pl.BlockSpec((ROW_TILE, w), row)
    dsa = jax.ShapeDtypeStruct((s, DSA_W), F32)
    table = pl.BlockSpec((HEAD_DIM // 2, ROW_TILE), lambda i: (0, i))
    return pl.pallas_call(
        _proj_kernel,
        out_shape=(jax.ShapeDtypeStruct((s, 3 * SB_W), BF16), dsa, dsa, dsa,
                   jax.ShapeDtypeStruct((s, MEM_W), BF16)),
        grid=(s // ROW_TILE,),
        in_specs=[tile(d), _resident((1, d)), _resident(w_in.shape), table, table,
                  _resident((1, DSA_W)), _resident((1, DSA_W)), _resident((1, MEM_W)),
                  _resident(km.shape), _resident(vm.shape)],
        out_specs=(tile(3 * SB_W), tile(DSA_W), tile(DSA_W), tile(DSA_W), tile(MEM_W)),
        compiler_params=_params(1),
        name="proj",
    )(x, g, w_in, cos, sin_signed, qn_d, kn_d, qn_m, km, vm)


def _dsa_kernel(*refs):
    n_g = len(DSA_GROUPS)
    ins = [refs[5 * g:5 * g + 5] for g in range(n_g)]
    o_ref = refs[5 * n_g]
    bias_ref = refs[5 * n_g + 1]
    out_scratch = refs[5 * n_g + 2:5 * n_g + 2 + 2 * n_g]
    stage_scratch = refs[5 * n_g + 2 + 2 * n_g:]
    step = pl.program_id(0)
    lane = lax.broadcasted_iota(jnp.int32, (Q_BLOCK, LANES), 1)
    head0 = lane < HEAD_DIM

    def own_lanes(rows):
        ln = lax.broadcasted_iota(jnp.int32, (rows, LANES), 1)
        rw = lax.broadcasted_iota(jnp.int32, (rows, LANES), 0)
        return (ln < HEAD_DIM) == (rw < rows // 2)

    own_q, own_v = own_lanes(2 * Q_BLOCK), own_lanes(4 * Q_BLOCK)
    qi = lax.broadcasted_iota(jnp.int32, (2 * Q_BLOCK, 2 * Q_BLOCK), 0) % Q_BLOCK
    kj = lax.broadcasted_iota(jnp.int32, (2 * Q_BLOCK, 2 * Q_BLOCK), 1)
    dist = Q_BLOCK + qi - kj
    in_band = (dist >= 0) & (dist <= Q_BLOCK)
    bias_ref[0] = jnp.where(in_band, 0.0, -jnp.inf)
    bias_ref[1] = jnp.where(in_band & (kj >= Q_BLOCK), 0.0, -jnp.inf)
    seq_start = jnp.where(step == 0, 1, 0)

    def attend(qs, ks, vs, biases):
        units = range(len(qs))
        sc, v2 = [], []
        for j in units:
            q2 = jnp.where(own_q, jnp.concatenate([qs[j], qs[j]], axis=0), 0.0).astype(BF16)
            v = vs[j].astype(BF16)
            v2.append(jnp.where(own_v, jnp.concatenate([v, v], axis=0), jnp.zeros((), BF16)))
            sc.append(_dot_nt(q2, ks[j].astype(BF16)))
        p2, m, den = [], [], []
        for j in units:
            s_j = sc[j] + biases[j]
            m.append(jnp.max(s_j, axis=-1, keepdims=True))
            p = jnp.exp(s_j - m[j])
            den.append(jnp.sum(p, axis=-1, keepdims=True))
            p = p.astype(BF16)
            p2.append(jnp.concatenate([p[:Q_BLOCK], p[Q_BLOCK:]], axis=1))
        pv = [_dot(p2[j], v2[j]) for j in units]
        outs = []
        for j in units:
            inv = 1.0 / den[j]
            lse = m[j] + jnp.log(den[j])
            outs.append((pv[j] * jnp.where(head0, inv[:Q_BLOCK], inv[Q_BLOCK:]),
                         jnp.where(head0, lse[:Q_BLOCK], lse[Q_BLOCK:])))
        return outs

    n_trips = DSA_UNIT // Q_BLOCK // DSA_UNROLL
    staged = 0
    for g, (window, r) in enumerate(DSA_GROUPS):
        assert window // r == Q_BLOCK
        q_ref, k_ref, kp_ref, v_ref, vp_ref = ins[g]
        og_ref, lg_ref = out_scratch[2 * g:2 * g + 2]
        prev = Q_BLOCK * r

        if r <= DSA_STAGE:
            assert DSA_UNROLL % r == 0

            def trip(t, first, r=r, prev=prev, q_ref=q_ref, k_ref=k_ref, kp_ref=kp_ref, v_ref=v_ref,
                     vp_ref=vp_ref, og_ref=og_ref, lg_ref=lg_ref):
                qs, ks, vs, biases, rows = [], [], [], [], []
                for j in range(DSA_UNROLL):
                    sub, c = t * (DSA_UNROLL // r) + j // r, j % r
                    rows.append(pl.ds(sub * prev + c, Q_BLOCK, stride=r))
                    qs.append(q_ref[rows[j], :])
                    if first and sub == 0:
                        half = pl.ds(c, Q_BLOCK, stride=r)
                        ks.append(jnp.concatenate([kp_ref[half, :], k_ref[half, :]], axis=0))
                        vs.append(jnp.concatenate([vp_ref[half, :], v_ref[half, :]], axis=0))
                        biases.append(bias_ref[seq_start])
                    else:
                        both = pl.ds((sub - 1) * prev + c, 2 * Q_BLOCK, stride=r)
                        ks.append(k_ref[both, :])
                        vs.append(v_ref[both, :])
                        biases.append(bias_ref[0])
                for j, (o, lse) in enumerate(attend(qs, ks, vs, biases)):
                    og_ref[rows[j], :] = o
                    lg_ref[rows[j], :] = lse

            trip(0, True)

            def later_trip(t, carry, trip=trip):
                trip(t, False)
                return carry

            lax.fori_loop(1, n_trips, later_trip, 0)
        else:
            inner = r // DSA_STAGE
            per_trip = DSA_UNROLL // DSA_STAGE
            assert inner <= DSA_STAGE and prev == DSA_UNIT and DSA_UNROLL % DSA_STAGE == 0 and inner % per_trip == 0
            qs_ref, ks_ref, vs_ref, os_ref, ls_ref = stage_scratch[5 * staged:5 * staged + 5]
            staged += 1
            half = Q_BLOCK * inner
            for c in range(DSA_STAGE):
                coarse = pl.ds(c, half, stride=DSA_STAGE)
                qs_ref[c] = q_ref[coarse, :]
                ks_ref[c, :half] = kp_ref[coarse, :]
                ks_ref[c, half:] = k_ref[coarse, :]
                vs_ref[c, :half] = vp_ref[coarse, :]
                vs_ref[c, half:] = v_ref[coarse, :]

            def staged_trip(t, carry, inner=inner, per_trip=per_trip, qs_ref=qs_ref, ks_ref=ks_ref,
                            vs_ref=vs_ref, os_ref=os_ref, ls_ref=ls_ref):
                bias = bias_ref[seq_start]
                spots = [(c, t * per_trip + f) for f in range(per_trip) for c in range(DSA_STAGE)]
                fine_q = lambda fine: pl.ds(fine, Q_BLOCK, stride=inner)
                fine_k = lambda fine: pl.ds(fine, 2 * Q_BLOCK, stride=inner)
                outs = attend([qs_ref.at[c][fine_q(fine), :] for c, fine in spots],
                              [ks_ref.at[c][fine_k(fine), :] for c, fine in spots],
                              [vs_ref.at[c][fine_k(fine), :] for c, fine in spots],
                              [bias] * len(spots))
                for (c, fine), (o, lse) in zip(spots, outs):
                    os_ref.at[c][fine_q(fine), :] = o
                    ls_ref.at[c][fine_q(fine), :] = lse
                return carry

            lax.fori_loop(0, n_trips, staged_trip, 0)
            for c in range(DSA_STAGE):
                coarse = pl.ds(c, half, stride=DSA_STAGE)
                og_ref[coarse, :] = os_ref[c]
                lg_ref[coarse, :] = ls_ref[c]

    lse = [out_scratch[2 * g + 1][...] for g in range(n_g)]
    top = functools.reduce(jnp.maximum, lse)
    e = [jnp.exp(l - top) for l in lse]
    num = sum(e[g] * out_scratch[2 * g][...] for g in range(n_g))
    o_ref[...] = (num / sum(e)).astype(o_ref.dtype)


def _dilated(q, k, v):
    s = q.shape[0]
    n_pairs = DSA_OUT_W // LANES
    in_specs, stage_scratch = [], []
    for g, (_, r) in enumerate(DSA_GROUPS):
        prev = Q_BLOCK * r
        per_unit = DSA_UNIT // prev
        cur = pl.BlockSpec((DSA_UNIT, LANES), lambda i, p, g=g: (i, n_pairs * g + p))
        prv = pl.BlockSpec((prev, LANES),
                           lambda i, p, g=g, n=per_unit: (jnp.maximum(i * n - 1, 0), n_pairs * g + p))
        in_specs += [cur, cur, prv, cur, prv]
        if r > DSA_STAGE:
            rows = DSA_UNIT // DSA_STAGE
            stage_scratch += [pltpu.VMEM((DSA_STAGE, rows, LANES), F32),
                              pltpu.VMEM((DSA_STAGE, 2 * rows, LANES), F32),
                              pltpu.VMEM((DSA_STAGE, 2 * rows, LANES), F32),
                              pltpu.VMEM((DSA_STAGE, rows, LANES), F32),
                              pltpu.VMEM((DSA_STAGE, rows, LANES), F32)]
    scratch = ([pltpu.VMEM((2, 2 * Q_BLOCK, 2 * Q_BLOCK), F32)]
               + [pltpu.VMEM((DSA_UNIT, LANES), F32)] * (2 * len(DSA_GROUPS)) + stage_scratch)
    args = []
    for _ in DSA_GROUPS:
        args += [q, k, k, v, v]
    return pl.pallas_call(
        _dsa_kernel,
        out_shape=jax.ShapeDtypeStruct((s, DSA_OUT_W), BF16),
        grid=(s // DSA_UNIT, n_pairs),
        in_specs=in_specs,
        out_specs=pl.BlockSpec((DSA_UNIT, LANES), lambda i, p: (i, p)),
        scratch_shapes=scratch,
        compiler_params=_params(2),
        name="dilated",
    )(*args)


def _tail_kernel(qkv_ref, qkv_hbm,
                 x_ref, gm_ref, wg_ref, bg_ref, yb_ref, yc_ref, wa_ref, wb_ref, wc_ref, wo_ref,
                 gf_ref, w1_ref, w3_ref, w2_ref, o_ref,
                 q2_ref, carry_ref, acc_ref, ya_ref, kvprev_ref, kblk_ref, vblk_ref, sem):
    n_slots = SB_QBLOCKS_PER_STEP
    step = pl.program_id(0)
    n_tiles = pl.num_programs(0) - 1
    first_qb = step * n_slots
    n_pairs = SB_W // LANES
    lane = lax.broadcasted_iota(jnp.int32, (2 * Q_BLOCK, LANES), 1)
    row = lax.broadcasted_iota(jnp.int32, (2 * Q_BLOCK, LANES), 0)
    own_lanes = (lane < HEAD_DIM) == (row < Q_BLOCK)
    before = lane < (row % Q_BLOCK)
    j = lax.broadcasted_iota(jnp.int32, (2 * Q_BLOCK, 2 * Q_BLOCK), 0) % Q_BLOCK
    s = lax.broadcasted_iota(jnp.int32, (2 * Q_BLOCK, 2 * Q_BLOCK), 1)
    suffix = jnp.where((s >= Q_BLOCK) | (j >= s), 1.0, 0.0).astype(BF16)

    def stack(x):
        return jnp.where(own_lanes, jnp.concatenate([x, x], axis=0), jnp.zeros((), x.dtype))

    pairs = range(n_pairs)
    sls = [slice(p * LANES, (p + 1) * LANES) for p in pairs]
    for slot in range(n_slots):
        for p in pairs:
            q2_ref[slot, p] = stack(qkv_ref[slot * Q_BLOCK:(slot + 1) * Q_BLOCK, sls[p]])

    def kv_block(handle, p):
        if handle is None:
            return kblk_ref[:, sls[p]], vblk_ref[:, sls[p]]
        rows = slice((handle % n_slots) * Q_BLOCK, (handle % n_slots + 1) * Q_BLOCK)
        k_cols = pl.ds(p * LANES, LANES)
        v_cols = pl.ds(SB_W + p * LANES, LANES)
        if handle < n_slots:
            return kvprev_ref[rows, k_cols], kvprev_ref[rows, v_cols]
        return qkv_ref[rows, pl.ds(SB_W + p * LANES, LANES)], qkv_ref[rows, pl.ds(2 * SB_W + p * LANES, LANES)]

    def walk(jobs, from_diagonal, last_rows=Q_BLOCK, between=(None, None, None)):
        partial = last_rows < Q_BLOCK
        assert not partial or (from_diagonal and all(len(kbs) > 1 for _, kbs in jobs))
        chains = [(i, b) for i, (_, kbs) in enumerate(jobs) for b in range(len(kbs))]

        def is_partial(i, b):
            return partial and b == len(jobs[i][1]) - 1

        def visiting(x, i, b):
            if is_partial(i, b):
                return jnp.concatenate([x[:last_rows], x[Q_BLOCK:Q_BLOCK + last_rows]], axis=0)
            return x

        z = {}
        for i, (slot, kbs) in enumerate(jobs):
            n_full = len(kbs) - 1 if partial else len(kbs)
            for p in pairs:
                for b in range(0, n_full - 1, 2):
                    keys = jnp.concatenate([kv_block(kbs[b], p)[0], kv_block(kbs[b + 1], p)[0]], axis=0)
                    both = _dot_nt(q2_ref[slot, p], keys)
                    z[i, b, p], z[i, b + 1, p] = both[:, :Q_BLOCK], both[:, Q_BLOCK:]
                for b in list(range(n_full - n_full % 2, n_full)) + list(range(n_full, len(kbs))):
                    z[i, b, p] = _dot_nt(visiting(q2_ref[slot, p], i, b), kv_block(kbs[b], p)[0])
        if between[0] is not None:
            between[0]()
        split = {}
        for i, b in chains:
            for p in pairs:
                if from_diagonal and b == 0:
                    z[i, b, p] = jnp.where(before, z[i, b, p], SB_MASKED_LOGIT)
                nz = -z[i, b, p]
                lf = jnp.minimum(nz, 0.0) - jnp.log(1.0 + jnp.exp(jnp.minimum(z[i, b, p], nz)))
                split[i, b, p] = jnp.concatenate(_split_bf16(lf), axis=1)
        sums = {(i, b, p): _dot(split[i, b, p], suffix) for i, b in chains for p in pairs}
        if between[1] is not None:
            between[1]()
        w2, w_last, tops = {}, {}, []
        for i, (slot, kbs) in enumerate(jobs):
            top = None
            for p in pairs:
                carry = None if from_diagonal else carry_ref[slot, p]
                ws = []
                for b in range(len(kbs)):
                    logw = z[i, b, p] + sums[i, b, p][:, :Q_BLOCK]
                    total = sums[i, b, p][:, Q_BLOCK:]
                    if carry is not None:
                        logw = logw + visiting(carry, i, b)
                        total = total + visiting(carry, i, b)
                    w = jnp.exp(logw).astype(BF16)
                    half = w.shape[0] // 2
                    if is_partial(i, b):
                        w_last[i, p] = jnp.concatenate([w[:half], w[half:]], axis=1)
                        carry = jnp.concatenate([total[:half], carry[last_rows:Q_BLOCK],
                                                 total[half:], carry[Q_BLOCK + last_rows:]], axis=0)
                    else:
                        ws += [w[:half], w[half:]]
                        carry = total
                w2[i, p] = jnp.concatenate(ws, axis=1)
                if not partial:
                    carry_ref[slot, p] = carry
                top = carry if top is None else jnp.maximum(top, carry)
            tops.append(top)
        for i, (slot, kbs) in enumerate(jobs):
            n_full = len(kbs) - 1 if partial else len(kbs)
            for p in pairs:
                v2 = jnp.concatenate([stack(kv_block(kbs[b], p)[1]) for b in range(n_full)], axis=0)
                pv = _dot(w2[i, p], v2)
                if partial:
                    extra = _dot(w_last[i, p], stack(kv_block(kbs[n_full], p)[1]))
                    pv = jnp.concatenate([pv[:last_rows] + extra, pv[last_rows:]], axis=0)
                acc_ref[slot, p] = pv if from_diagonal else acc_ref[slot, p] + pv
        if between[2] is not None:
            between[2]()
        return tops

    dense = {}

    def merge_stage():
        x = x_ref[...]
        d = x.shape[1]
        h = _rms(x, gm_ref[...]).astype(BF16)
        ya = ya_ref[(step + 1) % 2]
        merged = None
        for i, (y, w_ref) in enumerate(((ya, wa_ref), (yb_ref[...], wb_ref), (yc_ref[...], wc_ref))):
            logits = _dot(h, wg_ref[:, i * d:(i + 1) * d]) + bg_ref[:, i * d:(i + 1) * d]
            term = _dot(y, w_ref[...]) / (1.0 + jnp.exp(-logits))
            merged = term if merged is None else merged + term
        dense["x"] = x + _dot(merged.astype(BF16), wo_ref[...])
        dense["h"] = _rms(dense["x"], gf_ref[...]).astype(BF16)

    def up_stage():
        a = _dot(dense["h"], w1_ref[...])
        b = _dot(dense["h"], w3_ref[...])
        dense["act"] = (a * b / (1.0 + jnp.exp(-a))).astype(BF16)

    def down_stage():
        o_ref[...] = dense["x"] + 0.5 * _dot(dense["act"], w2_ref[...])

    def fetch(kb):
        rows = pl.ds(pl.multiple_of(kb * Q_BLOCK, Q_BLOCK), Q_BLOCK)
        copies = [pltpu.make_async_copy(qkv_hbm.at[rows, pl.ds(SB_W, SB_W)], kblk_ref, sem.at[0]),
                  pltpu.make_async_copy(qkv_hbm.at[rows, pl.ds(2 * SB_W, SB_W)], vblk_ref, sem.at[1])]
        for c in copies:
            c.start()
        for c in copies:
            c.wait()

    def cond(state):
        kb, top = state
        return jnp.logical_and(kb >= 0, top > SB_LOG_CUTOFF)

    def general():
        for slot in range(n_slots):
            def visit(kb, from_diagonal, slot=slot):
                fetch(kb)
                return jnp.max(walk([(slot, [None])], from_diagonal)[0])

            def body(state, visit=visit):
                kb, _ = state
                return kb - 1, visit(kb, False)

            qb = first_qb + slot
            lax.while_loop(cond, body, (qb - 1, visit(qb, True)))

    n_fused = SB_FUSED_BLOCKS
    assert n_fused - 1 <= n_slots

    def usual():
        jobs = [(slot, [n_slots + slot - b for b in range(n_fused)]) for slot in range(n_slots)]
        tops = walk(jobs, True, SB_LAST_BLOCK_ROWS, (merge_stage, up_stage, down_stage))
        return jnp.max(functools.reduce(jnp.maximum, tops))

    def edge():
        @pl.when(step == n_tiles)
        def _():
            merge_stage()
            up_stage()
            down_stage()
        return jnp.float32(jnp.inf)

    fused_top = lax.cond(jnp.logical_and(step >= 1, step < n_tiles), usual, edge)
    lax.cond(jnp.logical_and(fused_top > SB_LOG_CUTOFF, step < n_tiles), general, lambda: None)
    for slot in range(n_slots):
        for p in pairs:
            ya_ref[step % 2, slot * Q_BLOCK:(slot + 1) * Q_BLOCK, sls[p]] = acc_ref[slot, p].astype(BF16)
    kvprev_ref[...] = qkv_ref[:, SB_W:]


def _tail(qkv, x, gm, wg, bg, yb, yc, wa, wb, wc, wo, gf, w1, w3, w2):
    s, w = qkv.shape[0], qkv.shape[1] // 3
    d = x.shape[1]
    n_pairs = w // LANES
    n_slots = SB_QBLOCKS_PER_STEP
    rows = n_slots * Q_BLOCK
    n_tiles = s // rows
    this = lambda i: (jnp.minimum(i, n_tiles - 1), 0)
    last = lambda i: (jnp.maximum(i - 1, 0), 0)
    hbm = pl.BlockSpec(memory_space=pl.ANY)
    return pl.pallas_call(
        _tail_kernel,
        out_shape=jax.ShapeDtypeStruct((s, d), F32),
        grid=(n_tiles + 1,),
        in_specs=[pl.BlockSpec((rows, 3 * w), this), hbm,
                  pl.BlockSpec((rows, d), last), _resident((1, d)), _resident(wg.shape), _resident(bg.shape),
                  pl.BlockSpec((rows, DSA_OUT_W), last), pl.BlockSpec((rows, MEM_W), last),
                  _resident(wa.shape), _resident(wb.shape), _resident(wc.shape), _resident(wo.shape),
                  _resident((1, d)), _resident(w1.shape), _resident(w3.shape), _resident(w2.shape)],
        out_specs=pl.BlockSpec((rows, d), last),
        scratch_shapes=[pltpu.VMEM((n_slots, n_pairs, 2 * Q_BLOCK, LANES), BF16),
                        pltpu.VMEM((n_slots, n_pairs, 2 * Q_BLOCK, LANES), F32),
                        pltpu.VMEM((n_slots, n_pairs, Q_BLOCK, LANES), F32),
                        pltpu.VMEM((2, rows, w), BF16),
                        pltpu.VMEM((rows, 2 * w), BF16),
                        pltpu.VMEM((Q_BLOCK, w), BF16), pltpu.VMEM((Q_BLOCK, w), BF16),
                        pltpu.SemaphoreType.DMA((2,))],
        compiler_params=_params(1),
        name="tail",
    )(qkv, qkv, x, gm, wg, bg, yb, yc, wa, wb, wc, wo, gf, w1, w3, w2)


def _rope_tables(s):
    half = HEAD_DIM // 2
    inv_freq = jnp.power(ROPE_THETA, -jnp.arange(half, dtype=F32) / half)
    ang = inv_freq[:, None] * jnp.arange(s).astype(F32)[None, :]
    return jnp.cos(ang), jnp.sin(ang)


def _layer(x, mem, p):
    s = x.shape[0]
    bf = lambda w: w.astype(BF16)
    vec = lambda v: v.reshape(1, -1)
    heads = lambda v, n: jnp.tile(v, n).reshape(1, -1)
    cos, sin = _rope_tables(s)

    later = ("w_in", "w_gate", "w_branch_sb", "w_branch_dsa", "w_branch_mem", "w_out",
             "ffn2_w1", "ffn2_w3", "ffn2_w2")
    x, casted = _ffn(x, vec(p["ffn1_norm"]), bf(p["ffn1_w1"]), bf(p["ffn1_w3"]), bf(p["ffn1_w2"]),
                     cast=[p[name] for name in later])
    w = dict(zip(later, casted))
    km, vm = _memkv(mem, vec(p["mem_norm"]), bf(p["w_mem_kv"]), heads(p["kn_mem"], MEM_HEADS))
    n_dsa = DSA_W // HEAD_DIM
    qkv, qb, kb, vb, yc = _proj(
        x, vec(p["mix_norm"]), w["w_in"], cos, sin,
        heads(p["qn_dsa"], n_dsa), heads(p["kn_dsa"], n_dsa), heads(p["qn_mem"], MEM_HEADS), km, vm)
    yb = _dilated(qb, kb, vb)
    return _tail(qkv, x, vec(p["mix_norm"]), w["w_gate"], vec(p["b_gate"]), yb, yc,
                 w["w_branch_sb"], w["w_branch_dsa"], w["w_branch_mem"], w["w_out"],
                 vec(p["ffn2_norm"]), w["ffn2_w1"], w["ffn2_w3"], w["ffn2_w2"])


_PARAM_NAMES = ("ffn1_norm", "ffn1_w1", "ffn1_w3", "ffn1_w2", "mix_norm", "mem_norm", "w_in", "w_mem_kv",
                "qn_dsa", "kn_dsa", "qn_mem", "kn_mem", "w_branch_sb", "w_branch_dsa", "w_branch_mem",
                "w_gate", "b_gate", "w_out", "ffn2_norm", "ffn2_w1", "ffn2_w3", "ffn2_w2")


def kernel(x, mem, ffn1_norm, ffn1_w1, ffn1_w3, ffn1_w2, mix_norm, mem_norm, w_in, w_mem_kv, qn_dsa, kn_dsa, qn_mem, kn_mem, w_branch_sb, w_branch_dsa, w_branch_mem, w_gate, b_gate, w_out, ffn2_norm, ffn2_w1, ffn2_w3, ffn2_w2):
    stacked = dict(zip(_PARAM_NAMES, (ffn1_norm, ffn1_w1, ffn1_w3, ffn1_w2, mix_norm, mem_norm, w_in,
                                      w_mem_kv, qn_dsa, kn_dsa, qn_mem, kn_mem, w_branch_sb, w_branch_dsa,
                                      w_branch_mem, w_gate, b_gate, w_out, ffn2_norm, ffn2_w1, ffn2_w3,
                                      ffn2_w2)))
    depth = ffn1_norm.shape[0]
    outs = []
    for b in range(x.shape[0]):
        xb = x[b]
        for l in range(depth):
            xb = _layer(xb, mem[b], {k: v[l] for k, v in stacked.items()})
        outs.append(xb)
    return jnp.stack(outs)
```

```python
import functools

import jax
import jax.numpy as jnp
from jax import lax
from jax.experimental import pallas as pl
from jax.experimental.pallas import tpu as pltpu

F32 = jnp.float32
BF16 = jnp.bfloat16

HEAD_DIM = 64
SB_HEADS = 8
DSA_GROUPS = ((128, 1), (512, 4), (2048, 16))
DSA_HEADS_PER_GROUP = 4
MEM_HEADS = 4
ROPE_THETA = 10000.0
NORM_EPS = 1e-6
Q_BLOCK = 128
SB_W = SB_HEADS * HEAD_DIM
DSA_W = DSA_HEADS_PER_GROUP * len(DSA_GROUPS) * HEAD_DIM
DSA_OUT_W = DSA_HEADS_PER_GROUP * HEAD_DIM
MEM_W = MEM_HEADS * HEAD_DIM
QK_SCALE = HEAD_DIM ** -0.5

LANES = 128
MXU_WIDTH = 256
DSA_UNIT = Q_BLOCK * max(r for _, r in DSA_GROUPS)
DSA_UNROLL = 16
DSA_STAGE = 4
ROW_TILE = 512
VMEM_LIMIT = 56 * 1024 * 1024
FFN_CHUNK = 256
CAST_SLABS = 16
SB_LOG_CUTOFF = -104.0
SB_MASKED_LOGIT = -1e30
SB_FUSED_BLOCKS = 3
SB_QBLOCKS_PER_STEP = 2
SB_LAST_BLOCK_ROWS = 48


def _resident(shape):
    zeros = (0,) * len(shape)
    return pl.BlockSpec(shape, lambda *_: zeros, pipeline_mode=pl.Buffered(1))


def _params(n_axes):
    return pltpu.CompilerParams(dimension_semantics=("arbitrary",) * n_axes,
                                vmem_limit_bytes=VMEM_LIMIT)


def _rms(x, g):
    return x * lax.rsqrt(jnp.mean(x * x, axis=-1, keepdims=True) + NORM_EPS) * g


def _dot(a, b):
    return jnp.dot(a, b, preferred_element_type=F32)


def _dot_nt(a, b):
    return lax.dot_general(a, b, (((1,), (1,)), ((), ())), preferred_element_type=F32)


def _split_bf16(x):
    hi = x.astype(BF16)
    lo = (x - hi.astype(F32)).astype(BF16)
    return hi, lo


def _head_norm(x, g):
    n = x.shape[-1]
    w = min(n, MXU_WIDTH)
    r = lax.broadcasted_iota(jnp.int32, (w, w), 0) // HEAD_DIM
    c = lax.broadcasted_iota(jnp.int32, (w, w), 1) // HEAD_DIM
    bd = jnp.where(r == c, 1.0, 0.0).astype(BF16)
    hi, lo = _split_bf16(x * x)
    ms = jnp.concatenate([_dot(hi[:, j:j + w], bd) + _dot(lo[:, j:j + w], bd) for j in range(0, n, w)],
                         axis=1) * (1.0 / HEAD_DIM)
    return x * lax.rsqrt(ms + NORM_EPS) * g


def _ffn_kernel(x_ref, g_ref, w1_hbm, w3_hbm, w2_hbm, *refs):
    n_cast = (len(refs) - 8) // 2
    o_ref = refs[n_cast]
    w1_ref, w3_ref, w2_ref, s1_ref, s3_ref, s2_ref, sem = refs[2 * n_cast + 1:]
    f = w1_ref.shape[1]
    x = x_ref[...]
    h = _rms(x, g_ref[...]).astype(BF16)

    def swiglu(w1, w3, w2):
        a = _dot(h, w1)
        b = _dot(h, w3)
        return _dot((a * b / (1.0 + jnp.exp(-a))).astype(BF16), w2)

    def side_casts():
        for src, dst in zip(refs[:n_cast], refs[n_cast + 1:2 * n_cast + 1]):
            dst[...] = src[...].astype(BF16)

    def slice_copies(c, slot):
        cols = pl.ds(c * FFN_CHUNK, FFN_CHUNK)
        return [pltpu.make_async_copy(w1_hbm.at[:, cols], s1_ref.at[slot], sem.at[slot, 0]),
                pltpu.make_async_copy(w3_hbm.at[:, cols], s3_ref.at[slot], sem.at[slot, 1]),
                pltpu.make_async_copy(w2_hbm.at[cols, :], s2_ref.at[slot], sem.at[slot, 2])]

    @pl.when(pl.program_id(0) == 0)
    def _():
        n_chunks = f // FFN_CHUNK
        for copy in slice_copies(0, 0):
            copy.start()
        total = None
        for c in range(n_chunks):
            slot = c % 2
            if c + 1 < n_chunks:
                for copy in slice_copies(c + 1, 1 - slot):
                    copy.start()
            for copy in slice_copies(c, slot):
                copy.wait()
            cols = slice(c * FFN_CHUNK, (c + 1) * FFN_CHUNK)
            w1_ref[:, cols] = s1_ref[slot].astype(BF16)
            w3_ref[:, cols] = s3_ref[slot].astype(BF16)
            w2_ref[cols, :] = s2_ref[slot].astype(BF16)
            part = swiglu(w1_ref[:, cols], w3_ref[:, cols], w2_ref[cols, :])
            total = part if total is None else total + part
        o_ref[...] = x + 0.5 * total
        side_casts()

    @pl.when(pl.program_id(0) > 0)
    def _():
        o_ref[...] = x + 0.5 * swiglu(w1_ref[...], w3_ref[...], w2_ref[...])
        side_casts()


def _ffn(x, g, w1, w3, w2, cast=()):
    s, d = x.shape
    f = w1.shape[1]
    steps = s // ROW_TILE
    row = lambda i: (i, 0)
    n_slabs = min(CAST_SLABS, steps)
    per_slab = steps // n_slabs
    slab = lambda i: (i // per_slab, 0)
    slabs = [pl.BlockSpec((w.shape[0] // n_slabs, w.shape[1]), slab) for w in cast]
    hbm = pl.BlockSpec(memory_space=pl.ANY)
    out = pl.pallas_call(
        _ffn_kernel,
        out_shape=(jax.ShapeDtypeStruct((s, d), F32), *[jax.ShapeDtypeStruct(w.shape, BF16) for w in cast]),
        grid=(steps,),
        in_specs=[pl.BlockSpec((ROW_TILE, d), row), _resident((1, d)), hbm, hbm, hbm, *slabs],
        out_specs=(pl.BlockSpec((ROW_TILE, d), row), *slabs),
        scratch_shapes=[pltpu.VMEM((d, f), BF16), pltpu.VMEM((d, f), BF16), pltpu.VMEM((f, d), BF16),
                        pltpu.VMEM((2, d, FFN_CHUNK), F32), pltpu.VMEM((2, d, FFN_CHUNK), F32),
                        pltpu.VMEM((2, FFN_CHUNK, d), F32), pltpu.SemaphoreType.DMA((2, 3))],
        compiler_params=_params(1),
        name="ffn",
    )(x, g, w1, w3, w2, *cast)
    return out[0], list(out[1:])


def _memkv_kernel(mem_ref, g_ref, w_ref, kn_ref, k_ref, v_ref):
    h = _rms(mem_ref[...], g_ref[...]).astype(BF16)
    kv = _dot(h, w_ref[...])
    k_ref[...] = _head_norm(kv[:, :MEM_W], kn_ref[...]).astype(BF16)
    v_ref[...] = kv[:, MEM_W:].astype(BF16)


def _memkv(mem, g, w, kn):
    m = mem.shape[0]
    out = jax.ShapeDtypeStruct((m, MEM_W), BF16)
    return pl.pallas_call(_memkv_kernel, out_shape=(out, out), name="memkv",
                          compiler_params=pltpu.CompilerParams(vmem_limit_bytes=VMEM_LIMIT),
                          )(mem, g, w, kn)


def _rope(x, cos, sin_signed):
    lane = lax.broadcasted_iota(jnp.int32, (x.shape[0], LANES), 1)
    first_half = (lane % HEAD_DIM) < (HEAD_DIM // 2)
    out = []
    for j in range(x.shape[1] // LANES):
        xs = x[:, j * LANES:(j + 1) * LANES]
        partner = jnp.where(first_half, pltpu.roll(xs, LANES - HEAD_DIM // 2, 1),
                            pltpu.roll(xs, HEAD_DIM // 2, 1))
        out.append(xs * cos + partner * sin_signed)
    return jnp.concatenate(out, axis=1)


def _rope_lanes(cos_half, sin_half):
    half = HEAD_DIM // 2
    f = lax.broadcasted_iota(jnp.int32, (half, LANES), 0)
    lane = lax.broadcasted_iota(jnp.int32, (half, LANES), 1)
    hit = (lane % half) == f
    spread = jnp.where(hit, 1.0, 0.0).astype(BF16)
    signed = jnp.where(hit, jnp.where((lane % HEAD_DIM) < half, -1.0, 1.0), 0.0).astype(BF16)
    c_hi, c_lo = _split_bf16(cos_half)
    s_hi, s_lo = _split_bf16(sin_half)
    return _dot(c_hi, spread) + _dot(c_lo, spread), _dot(s_hi, signed) + _dot(s_lo, signed)


def _proj_kernel(x_ref, g_ref, w_ref, cos_ref, sin_ref, qn_d_ref, kn_d_ref, qn_m_ref, km_ref, vm_ref,
                 qkv_ref, qb_ref, kb_ref, vb_ref, yc_ref):
    h = _rms(x_ref[...], g_ref[...]).astype(BF16)
    cos, sin_signed = _rope_lanes(cos_ref[...].T, sin_ref[...].T)

    def cols(lo, width):
        return _dot(h, w_ref[:, lo:lo + width])

    base = 3 * SB_W
    qc_raw = cols(base + 3 * DSA_W, MEM_W)
    qb_raw = cols(base, DSA_W)
    qc = _head_norm(qc_raw, qn_m_ref[...]) * QK_SCALE
    kb_raw = cols(base + DSA_W, DSA_W)

    lane = lax.broadcasted_iota(jnp.int32, (qc.shape[0], LANES), 1)
    groups = [slice(j * LANES, (j + 1) * LANES) for j in range(MEM_W // LANES)]
    scores = [[_dot_nt(jnp.where((lane < HEAD_DIM) == (half == 0), qc[:, sl], 0.0).astype(BF16), km_ref[:, sl])
               for half in range(2)] for sl in groups]
    qb = _head_norm(qb_raw, qn_d_ref[...])
    vb_ref[...] = cols(base + 2 * DSA_W, DSA_W)
    for sl, (sc0, sc1) in zip(groups, scores):
        outs = []
        for sc in (sc0, sc1):
            p = jnp.exp(sc - jnp.max(sc, axis=-1, keepdims=True))
            outs.append(_dot(p.astype(BF16), vm_ref[:, sl]) / jnp.sum(p, axis=-1, keepdims=True))
        yc_ref[:, sl] = jnp.where(lane < HEAD_DIM, outs[0], outs[1]).astype(BF16)
    kb = _head_norm(kb_raw, kn_d_ref[...])
    qb_ref[...] = _rope(qb, cos, sin_signed) * QK_SCALE
    kb_ref[...] = _rope(kb, cos, sin_signed)
    qkv_ref[:, :SB_W] = (cols(0, SB_W) * QK_SCALE).astype(BF16)
    qkv_ref[:, SB_W:] = cols(SB_W, 2 * SB_W).astype(BF16)


def _proj(x, g, w_in, cos, sin_signed, qn_d, kn_d, qn_m, km, vm):
    s, d = x.shape
    row = lambda i: (i, 0)
    tile = lambda w: pl.BlockSpec((ROW_TILE, w), row)
    dsa = jax.ShapeDtypeStruct((s, DSA_W), F32)
    table = pl.BlockSpec((HEAD_DIM // 2, ROW_TILE), lambda i: (0, i))
    return pl.pallas_call(
        _proj_kernel,
        out_shape=(jax.ShapeDtypeStruct((s, 3 * SB_W), BF16), dsa, dsa, dsa,
                   jax.ShapeDtypeStruct((s, MEM_W), BF16)),
        grid=(s // ROW_TILE,),
        in_specs=[tile(d), _resident((1, d)), _resident(w_in.shape), table, table,
                  _resident((1, DSA_W)), _resident((1, DSA_W)), _resident((1, MEM_W)),
                  _resident(km.shape), _resident(vm.shape)],
        out_specs=(tile(3 * SB_W), tile(DSA_W), tile(DSA_W), tile(DSA_W), tile(MEM_W)),
        compiler_params=_params(1),
        name="proj",
    )(x, g, w_in, cos, sin_signed, qn_d, kn_d, qn_m, km, vm)


def _dsa_kernel(*refs):
    n_g = len(DSA_GROUPS)
    ins = [refs[5 * g:5 * g + 5] for g in range(n_g)]
    o_ref = refs[5 * n_g]
    bias_ref = refs[5 * n_g + 1]
    out_scratch = refs[5 * n_g + 2:5 * n_g + 2 + 2 * n_g]
    stage_scratch = refs[5 * n_g + 2 + 2 * n_g:]
    step = pl.program_id(0)
    lane = lax.broadcasted_iota(jnp.int32, (Q_BLOCK, LANES), 1)
    head0 = lane < HEAD_DIM

    def own_lanes(rows):
        ln = lax.broadcasted_iota(jnp.int32, (rows, LANES), 1)
        rw = lax.broadcasted_iota(jnp.int32, (rows, LANES), 0)
        return (ln < HEAD_DIM) == (rw < rows // 2)

    own_q, own_v = own_lanes(2 * Q_BLOCK), own_lanes(4 * Q_BLOCK)
    qi = lax.broadcasted_iota(jnp.int32, (2 * Q_BLOCK, 2 * Q_BLOCK), 0) % Q_BLOCK
    kj = lax.broadcasted_iota(jnp.int32, (2 * Q_BLOCK, 2 * Q_BLOCK), 1)
    dist = Q_BLOCK + qi - kj
    in_band = (dist >= 0) & (dist <= Q_BLOCK)
    bias_ref[0] = jnp.where(in_band, 0.0, -jnp.inf)
    bias_ref[1] = jnp.where(in_band & (kj >= Q_BLOCK), 0.0, -jnp.inf)
    seq_start = jnp.where(step == 0, 1, 0)

    def attend(qs, ks, vs, biases):
        units = range(len(qs))
        sc, v2 = [], []
        for j in units:
            q2 = jnp.where(own_q, jnp.concatenate([qs[j], qs[j]], axis=0), 0.0).astype(BF16)
            v = vs[j].astype(BF16)
            v2.append(jnp.where(own_v, jnp.concatenate([v, v], axis=0), jnp.zeros((), BF16)))
            sc.append(_dot_nt(q2, ks[j].astype(BF16)))
        p2, m, den = [], [], []
        for j in units:
            s_j = sc[j] + biases[j]
            m.append(jnp.max(s_j, axis=-1, keepdims=True))
            p = jnp.exp(s_j - m[j])
            den.append(jnp.sum(p, axis=-1, keepdims=True))
            p = p.astype(BF16)
            p2.append(jnp.concatenate([p[:Q_BLOCK], p[Q_BLOCK:]], axis=1))
        pv = [_dot(p2[j], v2[j]) for j in units]
        outs = []
        for j in units:
            inv = 1.0 / den[j]
            lse = m[j] + jnp.log(den[j])
            outs.append((pv[j] * jnp.where(head0, inv[:Q_BLOCK], inv[Q_BLOCK:]),
                         jnp.where(head0, lse[:Q_BLOCK], lse[Q_BLOCK:])))
        return outs

    n_trips = DSA_UNIT // Q_BLOCK // DSA_UNROLL
    staged = 0
    for g, (window, r) in enumerate(DSA_GROUPS):
        assert window // r == Q_BLOCK
        q_ref, k_ref, kp_ref, v_ref, vp_ref = ins[g]
        og_ref, lg_ref = out_scratch[2 * g:2 * g + 2]
        prev = Q_BLOCK * r

        if r <= DSA_STAGE:
            assert DSA_UNROLL % r == 0

            def trip(t, first, r=r, prev=prev, q_ref=q_ref, k_ref=k_ref, kp_ref=kp_ref, v_ref=v_ref,
                     vp_ref=vp_ref, og_ref=og_ref, lg_ref=lg_ref):
                qs, ks, vs, biases, rows = [], [], [], [], []
                for j in range(DSA_UNROLL):
                    sub, c = t * (DSA_UNROLL // r) + j // r, j % r
                    rows.append(pl.ds(sub * prev + c, Q_BLOCK, stride=r))
                    qs.append(q_ref[rows[j], :])
                    if first and sub == 0:
                        half = pl.ds(c, Q_BLOCK, stride=r)
                        ks.append(jnp.concatenate([kp_ref[half, :], k_ref[half, :]], axis=0))
                        vs.append(jnp.concatenate([vp_ref[half, :], v_ref[half, :]], axis=0))
                        biases.append(bias_ref[seq_start])
                    else:
                        both = pl.ds((sub - 1) * prev + c, 2 * Q_BLOCK, stride=r)
                        ks.append(k_ref[both, :])
                        vs.append(v_ref[both, :])
                        biases.append(bias_ref[0])
                for j, (o, lse) in enumerate(attend(qs, ks, vs, biases)):
                    og_ref[rows[j], :] = o
                    lg_ref[rows[j], :] = lse

            trip(0, True)

            def later_trip(t, carry, trip=trip):
                trip(t, False)
                return carry

            lax.fori_loop(1, n_trips, later_trip, 0)
        else:
            inner = r // DSA_STAGE
            per_trip = DSA_UNROLL // DSA_STAGE
            assert inner <= DSA_STAGE and prev == DSA_UNIT and DSA_UNROLL % DSA_STAGE == 0 and inner % per_trip == 0
            qs_ref, ks_ref, vs_ref, os_ref, ls_ref = stage_scratch[5 * staged:5 * staged + 5]
            staged += 1
            half = Q_BLOCK * inner
            for c in range(DSA_STAGE):
                coarse = pl.ds(c, half, stride=DSA_STAGE)
                qs_ref[c] = q_ref[coarse, :]
                ks_ref[c, :half] = kp_ref[coarse, :]
                ks_ref[c, half:] = k_ref[coarse, :]
                vs_ref[c, :half] = vp_ref[coarse, :]
                vs_ref[c, half:] = v_ref[coarse, :]

            def staged_trip(t, carry, inner=inner, per_trip=per_trip, qs_ref=qs_ref, ks_ref=ks_ref,
                            vs_ref=vs_ref, os_ref=os_ref, ls_ref=ls_ref):
                bias = bias_ref[seq_start]
                spots = [(c, t * per_trip + f) for f in range(per_trip) for c in range(DSA_STAGE)]
                fine_q = lambda fine: pl.ds(fine, Q_BLOCK, stride=inner)
                fine_k = lambda fine: pl.ds(fine, 2 * Q_BLOCK, stride=inner)
                outs = attend([qs_ref.at[c][fine_q(fine), :] for c, fine in spots],
                              [ks_ref.at[c][fine_k(fine), :] for c, fine in spots],
                              [vs_ref.at[c][fine_k(fine), :] for c, fine in spots],
                              [bias] * len(spots))
                for (c, fine), (o, lse) in zip(spots, outs):
                    os_ref.at[c][fine_q(fine), :] = o
                    ls_ref.at[c][fine_q(fine), :] = lse
                return carry

            lax.fori_loop(0, n_trips, staged_trip, 0)
            for c in range(DSA_STAGE):
                coarse = pl.ds(c, half, stride=DSA_STAGE)
                og_ref[coarse, :] = os_ref[c]
                lg_ref[coarse, :] = ls_ref[c]

    lse = [out_scratch[2 * g + 1][...] for g in range(n_g)]
    top = functools.reduce(jnp.maximum, lse)
    e = [jnp.exp(l - top) for l in lse]
    num = sum(e[g] * out_scratch[2 * g][...] for g in range(n_g))
    o_ref[...] = (num / sum(e)).astype(o_ref.dtype)


def _dilated(q, k, v):
    s = q.shape[0]
    n_pairs = DSA_OUT_W // LANES
    in_specs, stage_scratch = [], []
    for g, (_, r) in enumerate(DSA_GROUPS):
        prev = Q_BLOCK * r
        per_unit = DSA_UNIT // prev
        cur = pl.BlockSpec((DSA_UNIT, LANES), lambda i, p, g=g: (i, n_pairs * g + p))
        prv = pl.BlockSpec((prev, LANES),
                           lambda i, p, g=g, n=per_unit: (jnp.maximum(i * n - 1, 0), n_pairs * g + p))
        in_specs += [cur, cur, prv, cur, prv]
        if r > DSA_STAGE:
            rows = DSA_UNIT // DSA_STAGE
            stage_scratch += [pltpu.VMEM((DSA_STAGE, rows, LANES), F32),
                              pltpu.VMEM((DSA_STAGE, 2 * rows, LANES), F32),
                              pltpu.VMEM((DSA_STAGE, 2 * rows, LANES), F32),
                              pltpu.VMEM((DSA_STAGE, rows, LANES), F32),
                              pltpu.VMEM((DSA_STAGE, rows, LANES), F32)]
    scratch = ([pltpu.VMEM((2, 2 * Q_BLOCK, 2 * Q_BLOCK), F32)]
               + [pltpu.VMEM((DSA_UNIT, LANES), F32)] * (2 * len(DSA_GROUPS)) + stage_scratch)
    args = []
    for _ in DSA_GROUPS:
        args += [q, k, k, v, v]
    return pl.pallas_call(
        _dsa_kernel,
        out_shape=jax.ShapeDtypeStruct((s, DSA_OUT_W), BF16),
        grid=(s // DSA_UNIT, n_pairs),
        in_specs=in_specs,
        out_specs=pl.BlockSpec((DSA_UNIT, LANES), lambda i, p: (i, p)),
        scratch_shapes=scratch,
        compiler_params=_params(2),
        name="dilated",
    )(*args)


def _tail_kernel(qkv_ref, qkv_hbm,
                 x_ref, gm_ref, wg_ref, bg_ref, yb_ref, yc_ref, wa_ref, wb_ref, wc_ref, wo_ref,
                 gf_ref, w1_ref, w3_ref, w2_ref, o_ref,
                 q2_ref, carry_ref, acc_ref, ya_ref, kvprev_ref, kblk_ref, vblk_ref, sem):
    n_slots = SB_QBLOCKS_PER_STEP
    step = pl.program_id(0)
    n_tiles = pl.num_programs(0) - 1
    first_qb = step * n_slots
    n_pairs = SB_W // LANES
    lane = lax.broadcasted_iota(jnp.int32, (2 * Q_BLOCK, LANES), 1)
    row = lax.broadcasted_iota(jnp.int32, (2 * Q_BLOCK, LANES), 0)
    own_lanes = (lane < HEAD_DIM) == (row < Q_BLOCK)
    before = lane < (row % Q_BLOCK)
    j = lax.broadcasted_iota(jnp.int32, (2 * Q_BLOCK, 2 * Q_BLOCK), 0) % Q_BLOCK
    s = lax.broadcasted_iota(jnp.int32, (2 * Q_BLOCK, 2 * Q_BLOCK), 1)
    suffix = jnp.where((s >= Q_BLOCK) | (j >= s), 1.0, 0.0).astype(BF16)

    def stack(x):
        return jnp.where(own_lanes, jnp.concatenate([x, x], axis=0), jnp.zeros((), x.dtype))

    pairs = range(n_pairs)
    sls = [slice(p * LANES, (p + 1) * LANES) for p in pairs]
    for slot in range(n_slots):
        for p in pairs:
            q2_ref[slot, p] = stack(qkv_ref[slot * Q_BLOCK:(slot + 1) * Q_BLOCK, sls[p]])

    def kv_block(handle, p):
        if handle is None:
            return kblk_ref[:, sls[p]], vblk_ref[:, sls[p]]
        rows = slice((handle % n_slots) * Q_BLOCK, (handle % n_slots + 1) * Q_BLOCK)
        k_cols = pl.ds(p * LANES, LANES)
        v_cols = pl.ds(SB_W + p * LANES, LANES)
        if handle < n_slots:
            return kvprev_ref[rows, k_cols], kvprev_ref[rows, v_cols]
        return qkv_ref[rows, pl.ds(SB_W + p * LANES, LANES)], qkv_ref[rows, pl.ds(2 * SB_W + p * LANES, LANES)]

    def walk(jobs, from_diagonal, last_rows=Q_BLOCK, between=(None, None, None)):
        partial = last_rows < Q_BLOCK
        assert not partial or (from_diagonal and all(len(kbs) > 1 for _, kbs in jobs))
        chains = [(i, b) for i, (_, kbs) in enumerate(jobs) for b in range(len(kbs))]

        def is_partial(i, b):
            return partial and b == len(jobs[i][1]) - 1

        def visiting(x, i, b):
            if is_partial(i, b):
                return jnp.concatenate([x[:last_rows], x[Q_BLOCK:Q_BLOCK + last_rows]], axis=0)
            return x

        z = {}
        for i, (slot, kbs) in enumerate(jobs):
            n_full = len(kbs) - 1 if partial else len(kbs)
            for p in pairs:
                for b in range(0, n_full - 1, 2):
                    keys = jnp.concatenate([kv_block(kbs[b], p)[0], kv_block(kbs[b + 1], p)[0]], axis=0)
                    both = _dot_nt(q2_ref[slot, p], keys)
                    z[i, b, p], z[i, b + 1, p] = both[:, :Q_BLOCK], both[:, Q_BLOCK:]
                for b in list(range(n_full - n_full % 2, n_full)) + list(range(n_full, len(kbs))):
                    z[i, b, p] = _dot_nt(visiting(q2_ref[slot, p], i, b), kv_block(kbs[b], p)[0])
        if between[0] is not None:
            between[0]()
        split = {}
        for i, b in chains:
            for p in pairs:
                if from_diagonal and b == 0:
                    z[i, b, p] = jnp.where(before, z[i, b, p], SB_MASKED_LOGIT)
                nz = -z[i, b, p]
                lf = jnp.minimum(nz, 0.0) - jnp.log(1.0 + jnp.exp(jnp.minimum(z[i, b, p], nz)))
                split[i, b, p] = jnp.concatenate(_split_bf16(lf), axis=1)
        sums = {(i, b, p): _dot(split[i, b, p], suffix) for i, b in chains for p in pairs}
        if between[1] is not None:
            between[1]()
        w2, w_last, tops = {}, {}, []
        for i, (slot, kbs) in enumerate(jobs):
            top = None
            for p in pairs:
                carry = None if from_diagonal else carry_ref[slot, p]
                ws = []
                for b in range(len(kbs)):
                    logw = z[i, b, p] + sums[i, b, p][:, :Q_BLOCK]
                    total = sums[i, b, p][:, Q_BLOCK:]
                    if carry is not None:
                        logw = logw + visiting(carry, i, b)
                        total = total + visiting(carry, i, b)
                    w = jnp.exp(logw).astype(BF16)
                    half = w.shape[0] // 2
                    if is_partial(i, b):
                        w_last[i, p] = jnp.concatenate([w[:half], w[half:]], axis=1)
                        carry = jnp.concatenate([total[:half], carry[last_rows:Q_BLOCK],
                                                 total[half:], carry[Q_BLOCK + last_rows:]], axis=0)
                    else:
                        ws += [w[:half], w[half:]]
                        carry = total
                w2[i, p] = jnp.concatenate(ws, axis=1)
                if not partial:
                    carry_ref[slot, p] = carry
                top = carry if top is None else jnp.maximum(top, carry)
            tops.append(top)
        for i, (slot, kbs) in enumerate(jobs):
            n_full = len(kbs) - 1 if partial else len(kbs)
            for p in pairs:
                v2 = jnp.concatenate([stack(kv_block(kbs[b], p)[1]) for b in range(n_full)], axis=0)
                pv = _dot(w2[i, p], v2)
                if partial:
                    extra = _dot(w_last[i, p], stack(kv_block(kbs[n_full], p)[1]))
                    pv = jnp.concatenate([pv[:last_rows] + extra, pv[last_rows:]], axis=0)
                acc_ref[slot, p] = pv if from_diagonal else acc_ref[slot, p] + pv
        if between[2] is not None:
            between[2]()
        return tops

    dense = {}

    def merge_stage():
        x = x_ref[...]
        d = x.shape[1]
        h = _rms(x, gm_ref[...]).astype(BF16)
        ya = ya_ref[(step + 1) % 2]
        merged = None
        for i, (y, w_ref) in enumerate(((ya, wa_ref), (yb_ref[...], wb_ref), (yc_ref[...], wc_ref))):
            logits = _dot(h, wg_ref[:, i * d:(i + 1) * d]) + bg_ref[:, i * d:(i + 1) * d]
            term = _dot(y, w_ref[...]) / (1.0 + jnp.exp(-logits))
            merged = term if merged is None else merged + term
        dense["x"] = x + _dot(merged.astype(BF16), wo_ref[...])
        dense["h"] = _rms(dense["x"], gf_ref[...]).astype(BF16)

    def up_stage():
        a = _dot(dense["h"], w1_ref[...])
        b = _dot(dense["h"], w3_ref[...])
        dense["act"] = (a * b / (1.0 + jnp.exp(-a))).astype(BF16)

    def down_stage():
        o_ref[...] = dense["x"] + 0.5 * _dot(dense["act"], w2_ref[...])

    def fetch(kb):
        rows = pl.ds(pl.multiple_of(kb * Q_BLOCK, Q_BLOCK), Q_BLOCK)
        copies = [pltpu.make_async_copy(qkv_hbm.at[rows, pl.ds(SB_W, SB_W)], kblk_ref, sem.at[0]),
                  pltpu.make_async_copy(qkv_hbm.at[rows, pl.ds(2 * SB_W, SB_W)], vblk_ref, sem.at[1])]
        for c in copies:
            c.start()
        for c in copies:
            c.wait()

    def cond(state):
        kb, top = state
        return jnp.logical_and(kb >= 0, top > SB_LOG_CUTOFF)

    def general():
        for slot in range(n_slots):
            def visit(kb, from_diagonal, slot=slot):
                fetch(kb)
                return jnp.max(walk([(slot, [None])], from_diagonal)[0])

            def body(state, visit=visit):
                kb, _ = state
                return kb - 1, visit(kb, False)

            qb = first_qb + slot
            lax.while_loop(cond, body, (qb - 1, visit(qb, True)))

    n_fused = SB_FUSED_BLOCKS
    assert n_fused - 1 <= n_slots

    def usual():
        jobs = [(slot, [n_slots + slot - b for b in range(n_fused)]) for slot in range(n_slots)]
        tops = walk(jobs, True, SB_LAST_BLOCK_ROWS, (merge_stage, up_stage, down_stage))
        return jnp.max(functools.reduce(jnp.maximum, tops))

    def edge():
        @pl.when(step == n_tiles)
        def _():
            merge_stage()
            up_stage()
            down_stage()
        return jnp.float32(jnp.inf)

    fused_top = lax.cond(jnp.logical_and(step >= 1, step < n_tiles), usual, edge)
    lax.cond(jnp.logical_and(fused_top > SB_LOG_CUTOFF, step < n_tiles), general, lambda: None)
    for slot in range(n_slots):
        for p in pairs:
            ya_ref[step % 2, slot * Q_BLOCK:(slot + 1) * Q_BLOCK, sls[p]] = acc_ref[slot, p].astype(BF16)
    kvprev_ref[...] = qkv_ref[:, SB_W:]


def _tail(qkv, x, gm, wg, bg, yb, yc, wa, wb, wc, wo, gf, w1, w3, w2):
    s, w = qkv.shape[0], qkv.shape[1] // 3
    d = x.shape[1]
    n_pairs = w // LANES
    n_slots = SB_QBLOCKS_PER_STEP
    rows = n_slots * Q_BLOCK
    n_tiles = s // rows
    this = lambda i: (jnp.minimum(i, n_tiles - 1), 0)
    last = lambda i: (jnp.maximum(i - 1, 0), 0)
    hbm = pl.BlockSpec(memory_space=pl.ANY)
    return pl.pallas_call(
        _tail_kernel,
        out_shape=jax.ShapeDtypeStruct((s, d), F32),
        grid=(n_tiles + 1,),
        in_specs=[pl.BlockSpec((rows, 3 * w), this), hbm,
                  pl.BlockSpec((rows, d), last), _resident((1, d)), _resident(wg.shape), _resident(bg.shape),
                  pl.BlockSpec((rows, DSA_OUT_W), last), pl.BlockSpec((rows, MEM_W), last),
                  _resident(wa.shape), _resident(wb.shape), _resident(wc.shape), _resident(wo.shape),
                  _resident((1, d)), _resident(w1.shape), _resident(w3.shape), _resident(w2.shape)],
        out_specs=pl.BlockSpec((rows, d), last),
        scratch_shapes=[pltpu.VMEM((n_slots, n_pairs, 2 * Q_BLOCK, LANES), BF16),
                        pltpu.VMEM((n_slots, n_pairs, 2 * Q_BLOCK, LANES), F32),
                        pltpu.VMEM((n_slots, n_pairs, Q_BLOCK, LANES), F32),
                        pltpu.VMEM((2, rows, w), BF16),
                        pltpu.VMEM((rows, 2 * w), BF16),
                        pltpu.VMEM((Q_BLOCK, w), BF16), pltpu.VMEM((Q_BLOCK, w), BF16),
                        pltpu.SemaphoreType.DMA((2,))],
        compiler_params=_params(1),
        name="tail",
    )(qkv, qkv, x, gm, wg, bg, yb, yc, wa, wb, wc, wo, gf, w1, w3, w2)


def _rope_tables(s):
    half = HEAD_DIM // 2
    inv_freq = jnp.power(ROPE_THETA, -jnp.arange(half, dtype=F32) / half)
    ang = inv_freq[:, None] * jnp.arange(s).astype(F32)[None, :]
    return jnp.cos(ang), jnp.sin(ang)


def _layer(x, mem, p):
    s = x.shape[0]
    bf = lambda w: w.astype(BF16)
    vec = lambda v: v.reshape(1, -1)
    heads = lambda v, n: jnp.tile(v, n).reshape(1, -1)
    cos, sin = _rope_tables(s)

    later = ("w_in", "w_gate", "w_branch_sb", "w_branch_dsa", "w_branch_mem", "w_out",
             "ffn2_w1", "ffn2_w3", "ffn2_w2")
    x, casted = _ffn(x, vec(p["ffn1_norm"]), p["ffn1_w1"], p["ffn1_w3"], p["ffn1_w2"],
                     cast=[p[name] for name in later])
    w = dict(zip(later, casted))
    km, vm = _memkv(mem, vec(p["mem_norm"]), bf(p["w_mem_kv"]), heads(p["kn_mem"], MEM_HEADS))
    n_dsa = DSA_W // HEAD_DIM
    qkv, qb, kb, vb, yc = _proj(
        x, vec(p["mix_norm"]), w["w_in"], cos, sin,
        heads(p["qn_dsa"], n_dsa), heads(p["kn_dsa"], n_dsa), heads(p["qn_mem"], MEM_HEADS), km, vm)
    yb = _dilated(qb, kb, vb)
    return _tail(qkv, x, vec(p["mix_norm"]), w["w_gate"], vec(p["b_gate"]), yb, yc,
                 w["w_branch_sb"], w["w_branch_dsa"], w["w_branch_mem"], w["w_out"],
                 vec(p["ffn2_norm"]), w["ffn2_w1"], w["ffn2_w3"], w["ffn2_w2"])


_PARAM_NAMES = ("ffn1_norm", "ffn1_w1", "ffn1_w3", "ffn1_w2", "mix_norm", "mem_norm", "w_in", "w_mem_kv",
                "qn_dsa", "kn_dsa", "qn_mem", "kn_mem", "w_branch_sb", "w_branch_dsa", "w_branch_mem",
                "w_gate", "b_gate", "w_out", "ffn2_norm", "ffn2_w1", "ffn2_w3", "ffn2_w2")


def kernel(x, mem, ffn1_norm, ffn1_w1, ffn1_w3, ffn1_w2, mix_norm, mem_norm, w_in, w_mem_kv, qn_dsa, kn_dsa, qn_mem, kn_mem, w_branch_sb, w_branch_dsa, w_branch_mem, w_gate, b_gate, w_out, ffn2_norm, ffn2_w1, ffn2_w3, ffn2_w2):
    stacked = dict(zip(_PARAM_NAMES, (ffn1_norm, ffn1_w1, ffn1_w3, ffn1_w2, mix_norm, mem_norm, w_in,
                                      w_mem_kv, qn_dsa, kn_dsa, qn_mem, kn_mem, w_branch_sb, w_branch_dsa,
                                      w_branch_mem, w_gate, b_gate, w_out, ffn2_norm, ffn2_w1, ffn2_w3,
                                      ffn2_w2)))
    depth = ffn1_norm.shape[0]
    outs = []
    for b in range(x.shape[0]):
        xb = x[b]
        for l in range(depth):
            xb = _layer(xb, mem[b], {k: v[l] for k, v in stacked.items()})
        outs.append(xb)
    return jnp.stack(outs)
```

```python
import functools

import jax
import jax.numpy as jnp
from jax import lax
from jax.experimental import pallas as pl
from jax.experimental.pallas import tpu as pltpu

F32 = jnp.float32
BF16 = jnp.bfloat16

HEAD_DIM = 64
SB_HEADS = 8
DSA_GROUPS = ((128, 1), (512, 4), (2048, 16))
DSA_HEADS_PER_GROUP = 4
MEM_HEADS = 4
ROPE_THETA = 10000.0
NORM_EPS = 1e-6
Q_BLOCK = 128
SB_W = SB_HEADS * HEAD_DIM
DSA_W = DSA_HEADS_PER_GROUP * len(DSA_GROUPS) * HEAD_DIM
DSA_OUT_W = DSA_HEADS_PER_GROUP * HEAD_DIM
MEM_W = MEM_HEADS * HEAD_DIM
QK_SCALE = HEAD_DIM ** -0.5

LANES = 128
MXU_WIDTH = 256
DSA_UNIT = Q_BLOCK * max(r for _, r in DSA_GROUPS)
DSA_UNROLL = 16
DSA_STAGE = 4
ROW_TILE = 512
VMEM_LIMIT = 56 * 1024 * 1024
FFN_CHUNK = 256
CAST_SLABS = 16
SB_LOG_CUTOFF = -104.0
SB_MASKED_LOGIT = -1e30
SB_FUSED_BLOCKS = 3
SB_QBLOCKS_PER_STEP = 2
SB_LAST_BLOCK_ROWS = 48


def _resident(shape):
    zeros = (0,) * len(shape)
    return pl.BlockSpec(shape, lambda *_: zeros, pipeline_mode=pl.Buffered(1))


def _params(n_axes):
    return pltpu.CompilerParams(dimension_semantics=("arbitrary",) * n_axes,
                                vmem_limit_bytes=VMEM_LIMIT)


def _rms(x, g):
    return x * lax.rsqrt(jnp.mean(x * x, axis=-1, keepdims=True) + NORM_EPS) * g


def _dot(a, b):
    return jnp.dot(a, b, preferred_element_type=F32)


def _dot_nt(a, b):
    return lax.dot_general(a, b, (((1,), (1,)), ((), ())), preferred_element_type=F32)


def _split_bf16(x):
    hi = x.astype(BF16)
    lo = (x - hi.astype(F32)).astype(BF16)
    return hi, lo


def _head_norm(x, g):
    n = x.shape[-1]
    w = min(n, MXU_WIDTH)
    r = lax.broadcasted_iota(jnp.int32, (w, w), 0) // HEAD_DIM
    c = lax.broadcasted_iota(jnp.int32, (w, w), 1) // HEAD_DIM
    bd = jnp.where(r == c, 1.0, 0.0).astype(BF16)
    hi, lo = _split_bf16(x * x)
    ms = jnp.concatenate([_dot(hi[:, j:j + w], bd) + _dot(lo[:, j:j + w], bd) for j in range(0, n, w)],
                         axis=1) * (1.0 / HEAD_DIM)
    return x * lax.rsqrt(ms + NORM_EPS) * g


def _ffn_kernel(x_ref, g_ref, freq_ref, w1_hbm, w3_hbm, w2_hbm, *refs):
    n_cast = (len(refs) - 10) // 2
    o_ref, cos_ref, sin_ref = refs[n_cast:n_cast + 3]
    w1_ref, w3_ref, w2_ref, s1_ref, s3_ref, s2_ref, sem = refs[2 * n_cast + 3:]
    f = w1_ref.shape[1]
    x = x_ref[...]
    h = _rms(x, g_ref[...]).astype(BF16)

    def side_jobs():
        rows = x_ref.shape[0]
        pos = pl.program_id(0) * rows + lax.broadcasted_iota(jnp.int32, (freq_ref.shape[0], rows), 1)
        ang = freq_ref[...] * pos.astype(F32)
        cos_ref[...] = jnp.cos(ang)
        sin_ref[...] = jnp.sin(ang)
        for src, dst in zip(refs[:n_cast], refs[n_cast + 3:2 * n_cast + 3]):
            dst[...] = src[...].astype(BF16)

    def swiglu(w1, w3, w2):
        a = _dot(h, w1)
        b = _dot(h, w3)
        return _dot((a * b / (1.0 + jnp.exp(-a))).astype(BF16), w2)

    def slice_copies(c, slot):
        cols = pl.ds(c * FFN_CHUNK, FFN_CHUNK)
        return [pltpu.make_async_copy(w1_hbm.at[:, cols], s1_ref.at[slot], sem.at[slot, 0]),
                pltpu.make_async_copy(w3_hbm.at[:, cols], s3_ref.at[slot], sem.at[slot, 1]),
                pltpu.make_async_copy(w2_hbm.at[cols, :], s2_ref.at[slot], sem.at[slot, 2])]

    @pl.when(pl.program_id(0) == 0)
    def _():
        n_chunks = f // FFN_CHUNK
        for copy in slice_copies(0, 0):
            copy.start()
        total = None
        for c in range(n_chunks):
            slot = c % 2
            if c + 1 < n_chunks:
                for copy in slice_copies(c + 1, 1 - slot):
                    copy.start()
            for copy in slice_copies(c, slot):
                copy.wait()
            cols = slice(c * FFN_CHUNK, (c + 1) * FFN_CHUNK)
            w1_ref[:, cols] = s1_ref[slot].astype(BF16)
            w3_ref[:, cols] = s3_ref[slot].astype(BF16)
            w2_ref[cols, :] = s2_ref[slot].astype(BF16)
            part = swiglu(w1_ref[:, cols], w3_ref[:, cols], w2_ref[cols, :])
            total = part if total is None else total + part
        o_ref[...] = x + 0.5 * total
        side_jobs()

    @pl.when(pl.program_id(0) > 0)
    def _():
        o_ref[...] = x + 0.5 * swiglu(w1_ref[...], w3_ref[...], w2_ref[...])
        side_jobs()


def _ffn(x, g, w1, w3, w2, inv_freq, cast=()):
    s, d = x.shape
    f = w1.shape[1]
    half = inv_freq.shape[0]
    table = pl.BlockSpec((half, ROW_TILE), lambda i: (0, i))
    steps = s // ROW_TILE
    row = lambda i: (i, 0)
    n_slabs = min(CAST_SLABS, steps)
    per_slab = steps // n_slabs
    slab = lambda i: (i // per_slab, 0)
    slabs = [pl.BlockSpec((w.shape[0] // n_slabs, w.shape[1]), slab) for w in cast]
    hbm = pl.BlockSpec(memory_space=pl.ANY)
    out = pl.pallas_call(
        _ffn_kernel,
        out_shape=(jax.ShapeDtypeStruct((s, d), F32), jax.ShapeDtypeStruct((half, s), F32),
                   jax.ShapeDtypeStruct((half, s), F32), *[jax.ShapeDtypeStruct(w.shape, BF16) for w in cast]),
        grid=(steps,),
        in_specs=[pl.BlockSpec((ROW_TILE, d), row), _resident((1, d)), _resident((half, 1)), hbm, hbm, hbm, *slabs],
        out_specs=(pl.BlockSpec((ROW_TILE, d), row), table, table, *slabs),
        scratch_shapes=[pltpu.VMEM((d, f), BF16), pltpu.VMEM((d, f), BF16), pltpu.VMEM((f, d), BF16),
                        pltpu.VMEM((2, d, FFN_CHUNK), F32), pltpu.VMEM((2, d, FFN_CHUNK), F32),
                        pltpu.VMEM((2, FFN_CHUNK, d), F32), pltpu.SemaphoreType.DMA((2, 3))],
        compiler_params=_params(1),
        name="ffn",
    )(x, g, inv_freq, w1, w3, w2, *cast)
    return out[0], out[1], out[2], list(out[3:])


def _memkv_kernel(mem_ref, g_ref, w_ref, kn_ref, k_ref, v_ref):
    h = _rms(mem_ref[...], g_ref[...]).astype(BF16)
    kv = _dot(h, w_ref[...])
    k_ref[...] = _head_norm(kv[:, :MEM_W], kn_ref[...]).astype(BF16)
    v_ref[...] = kv[:, MEM_W:].astype(BF16)


def _memkv(mem, g, w, kn):
    m = mem.shape[0]
    out = jax.ShapeDtypeStruct((m, MEM_W), BF16)
    return pl.pallas_call(_memkv_kernel, out_shape=(out, out), name="memkv",
                          compiler_params=pltpu.CompilerParams(vmem_limit_bytes=VMEM_LIMIT),
                          )(mem, g, w, kn)


def _rope(x, cos, sin_signed):
    lane = lax.broadcasted_iota(jnp.int32, (x.shape[0], LANES), 1)
    first_half = (lane % HEAD_DIM) < (HEAD_DIM // 2)
    out = []
    for j in range(x.shape[1] // LANES):
        xs = x[:, j * LANES:(j + 1) * LANES]
        partner = jnp.where(first_half, pltpu.roll(xs, LANES - HEAD_DIM // 2, 1),
                            pltpu.roll(xs, HEAD_DIM // 2, 1))
        out.append(xs * cos + partner * sin_signed)
    return jnp.concatenate(out, axis=1)


def _rope_lanes(cos_half, sin_half):
    half = HEAD_DIM // 2
    f = lax.broadcasted_iota(jnp.int32, (half, LANES), 0)
    lane = lax.broadcasted_iota(jnp.int32, (half, LANES), 1)
    hit = (lane % half) == f
    spread = jnp.where(hit, 1.0, 0.0).astype(BF16)
    signed = jnp.where(hit, jnp.where((lane % HEAD_DIM) < half, -1.0, 1.0), 0.0).astype(BF16)
    c_hi, c_lo = _split_bf16(cos_half)
    s_hi, s_lo = _split_bf16(sin_half)
    return _dot(c_hi, spread) + _dot(c_lo, spread), _dot(s_hi, signed) + _dot(s_lo, signed)


def _proj_kernel(x_ref, g_ref, w_ref, cos_ref, sin_ref, qn_d_ref, kn_d_ref, qn_m_ref, km_ref, vm_ref,
                 qkv_ref, qb_ref, kb_ref, vb_ref, yc_ref):
    h = _rms(x_ref[...], g_ref[...]).astype(BF16)
    cos, sin_signed = _rope_lanes(cos_ref[...].T, sin_ref[...].T)

    def cols(lo, width):
        return _dot(h, w_ref[:, lo:lo + width])

    base = 3 * SB_W
    qc_raw = cols(base + 3 * DSA_W, MEM_W)
    qb_raw = cols(base, DSA_W)
    qc = _head_norm(qc_raw, qn_m_ref[...]) * QK_SCALE
    kb_raw = cols(base + DSA_W, DSA_W)

    lane = lax.broadcasted_iota(jnp.int32, (qc.shape[0], LANES), 1)
    groups = [slice(j * LANES, (j + 1) * LANES) for j in range(MEM_W // LANES)]
    scores = [[_dot_nt(jnp.where((lane < HEAD_DIM) == (half == 0), qc[:, sl], 0.0).astype(BF16), km_ref[:, sl])
               for half in range(2)] for sl in groups]
    qb = _head_norm(qb_raw, qn_d_ref[...])
    vb_ref[...] = cols(base + 2 * DSA_W, DSA_W)
    for sl, (sc0, sc1) in zip(groups, scores):
        outs = []
        for sc in (sc0, sc1):
            p = jnp.exp(sc - jnp.max(sc, axis=-1, keepdims=True))
            outs.append(_dot(p.astype(BF16), vm_ref[:, sl]) / jnp.sum(p, axis=-1, keepdims=True))
        yc_ref[:, sl] = jnp.where(lane < HEAD_DIM, outs[0], outs[1]).astype(BF16)
    kb = _head_norm(kb_raw, kn_d_ref[...])
    qb_ref[...] = _rope(qb, cos, sin_signed) * QK_SCALE
    kb_ref[...] = _rope(kb, cos, sin_signed)
    qkv_ref[:, :SB_W] = (cols(0, SB_W) * QK_SCALE).astype(BF16)
    qkv_ref[:, SB_W:] = cols(SB_W, 2 * SB_W).astype(BF16)


def _proj(x, g, w_in, cos, sin_signed, qn_d, kn_d, qn_m, km, vm):
    s, d = x.shape
    row = lambda i: (i, 0)
    tile = lambda w: pl.BlockSpec((ROW_TILE, w), row)
    dsa = jax.ShapeDtypeStruct((s, DSA_W), F32)
    table = pl.BlockSpec((HEAD_DIM // 2, ROW_TILE), lambda i: (0, i))
    return pl.pallas_call(
        _proj_kernel,
        out_shape=(jax.ShapeDtypeStruct((s, 3 * SB_W), BF16), dsa, dsa, dsa,
                   jax.ShapeDtypeStruct((s, MEM_W), BF16)),
        grid=(s // ROW_TILE,),
        in_specs=[tile(d), _resident((1, d)), _resident(w_in.shape), table, table,
                  _resident((1, DSA_W)), _resident((1, DSA_W)), _resident((1, MEM_W)),
                  _resident(km.shape), _resident(vm.shape)],
        out_specs=(tile(3 * SB_W), tile(DSA_W), tile(DSA_W), tile(DSA_W), tile(MEM_W)),
        compiler_params=_params(1),
        name="proj",
    )(x, g, w_in, cos, sin_signed, qn_d, kn_d, qn_m, km, vm)


def _dsa_kernel(*refs):
    n_g = len(DSA_GROUPS)
    ins = [refs[5 * g:5 * g + 5] for g in range(n_g)]
    o_ref = refs[5 * n_g]
    bias_ref = refs[5 * n_g + 1]
    out_scratch = refs[5 * n_g + 2:5 * n_g + 2 + 2 * n_g]
    stage_scratch = refs[5 * n_g + 2 + 2 * n_g:]
    step = pl.program_id(0)
    lane = lax.broadcasted_iota(jnp.int32, (Q_BLOCK, LANES), 1)
    head0 = lane < HEAD_DIM

    def own_lanes(rows):
        ln = lax.broadcasted_iota(jnp.int32, (rows, LANES), 1)
        rw = lax.broadcasted_iota(jnp.int32, (rows, LANES), 0)
        return (ln < HEAD_DIM) == (rw < rows // 2)

    own_q, own_v = own_lanes(2 * Q_BLOCK), own_lanes(4 * Q_BLOCK)
    qi = lax.broadcasted_iota(jnp.int32, (2 * Q_BLOCK, 2 * Q_BLOCK), 0) % Q_BLOCK
    kj = lax.broadcasted_iota(jnp.int32, (2 * Q_BLOCK, 2 * Q_BLOCK), 1)
    dist = Q_BLOCK + qi - kj
    in_band = (dist >= 0) & (dist <= Q_BLOCK)
    bias_ref[0] = jnp.where(in_band, 0.0, -jnp.inf)
    bias_ref[1] = jnp.where(in_band & (kj >= Q_BLOCK), 0.0, -jnp.inf)
    seq_start = jnp.where(step == 0, 1, 0)

    def attend(qs, ks, vs, biases):
        units = range(len(qs))
        sc, v2 = [], []
        for j in units:
            q2 = jnp.where(own_q, jnp.concatenate([qs[j], qs[j]], axis=0), 0.0).astype(BF16)
            v = vs[j].astype(BF16)
            v2.append(jnp.where(own_v, jnp.concatenate([v, v], axis=0), jnp.zeros((), BF16)))
            sc.append(_dot_nt(q2, ks[j].astype(BF16)))
        p2, m, den = [], [], []
        for j in units:
            s_j = sc[j] + biases[j]
            m.append(jnp.max(s_j, axis=-1, keepdims=True))
            p = jnp.exp(s_j - m[j])
            den.append(jnp.sum(p, axis=-1, keepdims=True))
            p = p.astype(BF16)
            p2.append(jnp.concatenate([p[:Q_BLOCK], p[Q_BLOCK:]], axis=1))
        pv = [_dot(p2[j], v2[j]) for j in units]
        outs = []
        for j in units:
            inv = 1.0 / den[j]
            lse = m[j] + jnp.log(den[j])
            outs.append((pv[j] * jnp.where(head0, inv[:Q_BLOCK], inv[Q_BLOCK:]),
                         jnp.where(head0, lse[:Q_BLOCK], lse[Q_BLOCK:])))
        return outs

    n_trips = DSA_UNIT // Q_BLOCK // DSA_UNROLL
    staged = 0
    for g, (window, r) in enumerate(DSA_GROUPS):
        assert window // r == Q_BLOCK
        q_ref, k_ref, kp_ref, v_ref, vp_ref = ins[g]
        og_ref, lg_ref = out_scratch[2 * g:2 * g + 2]
        prev = Q_BLOCK * r

        if r <= DSA_STAGE:
            assert DSA_UNROLL % r == 0

            def trip(t, first, r=r, prev=prev, q_ref=q_ref, k_ref=k_ref, kp_ref=kp_ref, v_ref=v_ref,
                     vp_ref=vp_ref, og_ref=og_ref, lg_ref=lg_ref):
                qs, ks, vs, biases, rows = [], [], [], [], []
                for j in range(DSA_UNROLL):
                    sub, c = t * (DSA_UNROLL // r) + j // r, j % r
                    rows.append(pl.ds(sub * prev + c, Q_BLOCK, stride=r))
                    qs.append(q_ref[rows[j], :])
                    if first and sub == 0:
                        half = pl.ds(c, Q_BLOCK, stride=r)
                        ks.append(jnp.concatenate([kp_ref[half, :], k_ref[half, :]], axis=0))
                        vs.append(jnp.concatenate([vp_ref[half, :], v_ref[half, :]], axis=0))
                        biases.append(bias_ref[seq_start])
                    else:
                        both = pl.ds((sub - 1) * prev + c, 2 * Q_BLOCK, stride=r)
                        ks.append(k_ref[both, :])
                        vs.append(v_ref[both, :])
                        biases.append(bias_ref[0])
                for j, (o, lse) in enumerate(attend(qs, ks, vs, biases)):
                    og_ref[rows[j], :] = o
                    lg_ref[rows[j], :] = lse

            trip(0, True)

            def later_trip(t, carry, trip=trip):
                trip(t, False)
                return carry

            lax.fori_loop(1, n_trips, later_trip, 0)
        else:
            inner = r // DSA_STAGE
            per_trip = DSA_UNROLL // DSA_STAGE
            assert inner <= DSA_STAGE and prev == DSA_UNIT and DSA_UNROLL % DSA_STAGE == 0 and inner % per_trip == 0
            qs_ref, ks_ref, vs_ref, os_ref, ls_ref = stage_scratch[5 * staged:5 * staged + 5]
            staged += 1
            half = Q_BLOCK * inner
            for c in range(DSA_STAGE):
                coarse = pl.ds(c, half, stride=DSA_STAGE)
                qs_ref[c] = q_ref[coarse, :]
                ks_ref[c, :half] = kp_ref[coarse, :]
                ks_ref[c, half:] = k_ref[coarse, :]
                vs_ref[c, :half] = vp_ref[coarse, :]
                vs_ref[c, half:] = v_ref[coarse, :]

            def staged_trip(t, carry, inner=inner, per_trip=per_trip, qs_ref=qs_ref, ks_ref=ks_ref,
                            vs_ref=vs_ref, os_ref=os_ref, ls_ref=ls_ref):
                bias = bias_ref[seq_start]
                spots = [(c, t * per_trip + f) for f in range(per_trip) for c in range(DSA_STAGE)]
                fine_q = lambda fine: pl.ds(fine, Q_BLOCK, stride=inner)
                fine_k = lambda fine: pl.ds(fine, 2 * Q_BLOCK, stride=inner)
                outs = attend([qs_ref.at[c][fine_q(fine), :] for c, fine in spots],
                              [ks_ref.at[c][fine_k(fine), :] for c, fine in spots],
                              [vs_ref.at[c][fine_k(fine), :] for c, fine in spots],
                              [bias] * len(spots))
                for (c, fine), (o, lse) in zip(spots, outs):
                    os_ref.at[c][fine_q(fine), :] = o
                    ls_ref.at[c][fine_q(fine), :] = lse
                return carry

            lax.fori_loop(0, n_trips, staged_trip, 0)
            for c in range(DSA_STAGE):
                coarse = pl.ds(c, half, stride=DSA_STAGE)
                og_ref[coarse, :] = os_ref[c]
                lg_ref[coarse, :] = ls_ref[c]

    lse = [out_scratch[2 * g + 1][...] for g in range(n_g)]
    top = functools.reduce(jnp.maximum, lse)
    e = [jnp.exp(l - top) for l in lse]
    num = sum(e[g] * out_scratch[2 * g][...] for g in range(n_g))
    o_ref[...] = (num / sum(e)).astype(o_ref.dtype)


def _dilated(q, k, v):
    s = q.shape[0]
    n_pairs = DSA_OUT_W // LANES
    in_specs, stage_scratch = [], []
    for g, (_, r) in enumerate(DSA_GROUPS):
        prev = Q_BLOCK * r
        per_unit = DSA_UNIT // prev
        cur = pl.BlockSpec((DSA_UNIT, LANES), lambda i, p, g=g: (i, n_pairs * g + p))
        prv = pl.BlockSpec((prev, LANES),
                           lambda i, p, g=g, n=per_unit: (jnp.maximum(i * n - 1, 0), n_pairs * g + p))
        in_specs += [cur, cur, prv, cur, prv]
        if r > DSA_STAGE:
            rows = DSA_UNIT // DSA_STAGE
            stage_scratch += [pltpu.VMEM((DSA_STAGE, rows, LANES), F32),
                              pltpu.VMEM((DSA_STAGE, 2 * rows, LANES), F32),
                              pltpu.VMEM((DSA_STAGE, 2 * rows, LANES), F32),
                              pltpu.VMEM((DSA_STAGE, rows, LANES), F32),
                              pltpu.VMEM((DSA_STAGE, rows, LANES), F32)]
    scratch = ([pltpu.VMEM((2, 2 * Q_BLOCK, 2 * Q_BLOCK), F32)]
               + [pltpu.VMEM((DSA_UNIT, LANES), F32)] * (2 * len(DSA_GROUPS)) + stage_scratch)
    args = []
    for _ in DSA_GROUPS:
        args += [q, k, k, v, v]
    return pl.pallas_call(
        _dsa_kernel,
        out_shape=jax.ShapeDtypeStruct((s, DSA_OUT_W), BF16),
        grid=(s // DSA_UNIT, n_pairs),
        in_specs=in_specs,
        out_specs=pl.BlockSpec((DSA_UNIT, LANES), lambda i, p: (i, p)),
        scratch_shapes=scratch,
        compiler_params=_params(2),
        name="dilated",
    )(*args)


def _tail_kernel(qkv_ref, qkv_hbm,
                 x_ref, gm_ref, wg_ref, bg_ref, yb_ref, yc_ref, wa_ref, wb_ref, wc_ref, wo_ref,
                 gf_ref, w1_ref, w3_ref, w2_ref, o_ref,
                 q2_ref, carry_ref, acc_ref, ya_ref, kvprev_ref, kblk_ref, vblk_ref, sem):
    n_slots = SB_QBLOCKS_PER_STEP
    step = pl.program_id(0)
    n_tiles = pl.num_programs(0) - 1
    first_qb = step * n_slots
    n_pairs = SB_W // LANES
    lane = lax.broadcasted_iota(jnp.int32, (2 * Q_BLOCK, LANES), 1)
    row = lax.broadcasted_iota(jnp.int32, (2 * Q_BLOCK, LANES), 0)
    own_lanes = (lane < HEAD_DIM) == (row < Q_BLOCK)
    before = lane < (row % Q_BLOCK)
    j = lax.broadcasted_iota(jnp.int32, (2 * Q_BLOCK, 2 * Q_BLOCK), 0) % Q_BLOCK
    s = lax.broadcasted_iota(jnp.int32, (2 * Q_BLOCK, 2 * Q_BLOCK), 1)
    suffix = jnp.where((s >= Q_BLOCK) | (j >= s), 1.0, 0.0).astype(BF16)

    def stack(x):
        return jnp.where(own_lanes, jnp.concatenate([x, x], axis=0), jnp.zeros((), x.dtype))

    pairs = range(n_pairs)
    sls = [slice(p * LANES, (p + 1) * LANES) for p in pairs]
    for slot in range(n_slots):
        for p in pairs:
            q2_ref[slot, p] = stack(qkv_ref[slot * Q_BLOCK:(slot + 1) * Q_BLOCK, sls[p]])

    def kv_block(handle, p):
        if handle is None:
            return kblk_ref[:, sls[p]], vblk_ref[:, sls[p]]
        rows = slice((handle % n_slots) * Q_BLOCK, (handle % n_slots + 1) * Q_BLOCK)
        k_cols = pl.ds(p * LANES, LANES)
        v_cols = pl.ds(SB_W + p * LANES, LANES)
        if handle < n_slots:
            return kvprev_ref[rows, k_cols], kvprev_ref[rows, v_cols]
        return qkv_ref[rows, pl.ds(SB_W + p * LANES, LANES)], qkv_ref[rows, pl.ds(2 * SB_W + p * LANES, LANES)]

    def walk(jobs, from_diagonal, last_rows=Q_BLOCK, between=(None, None, None)):
        partial = last_rows < Q_BLOCK
        assert not partial or (from_diagonal and all(len(kbs) > 1 for _, kbs in jobs))
        chains = [(i, b) for i, (_, kbs) in enumerate(jobs) for b in range(len(kbs))]

        def is_partial(i, b):
            return partial and b == len(jobs[i][1]) - 1

        def visiting(x, i, b):
            if is_partial(i, b):
                return jnp.concatenate([x[:last_rows], x[Q_BLOCK:Q_BLOCK + last_rows]], axis=0)
            return x

        z = {}
        for i, (slot, kbs) in enumerate(jobs):
            n_full = len(kbs) - 1 if partial else len(kbs)
            for p in pairs:
                for b in range(0, n_full - 1, 2):
                    keys = jnp.concatenate([kv_block(kbs[b], p)[0], kv_block(kbs[b + 1], p)[0]], axis=0)
                    both = _dot_nt(q2_ref[slot, p], keys)
                    z[i, b, p], z[i, b + 1, p] = both[:, :Q_BLOCK], both[:, Q_BLOCK:]
                for b in list(range(n_full - n_full % 2, n_full)) + list(range(n_full, len(kbs))):
                    z[i, b, p] = _dot_nt(visiting(q2_ref[slot, p], i, b), kv_block(kbs[b], p)[0])
        if between[0] is not None:
            between[0]()
        split = {}
        for i, b in chains:
            for p in pairs:
                if from_diagonal and b == 0:
                    z[i, b, p] = jnp.where(before, z[i, b, p], SB_MASKED_LOGIT)
                nz = -z[i, b, p]
                lf = jnp.minimum(nz, 0.0) - jnp.log(1.0 + jnp.exp(jnp.minimum(z[i, b, p], nz)))
                split[i, b, p] = jnp.concatenate(_split_bf16(lf), axis=1)
        sums = {(i, b, p): _dot(split[i, b, p], suffix) for i, b in chains for p in pairs}
        if between[1] is not None:
            between[1]()
        w2, w_last, tops = {}, {}, []
        for i, (slot, kbs) in enumerate(jobs):
            top = None
            for p in pairs:
                carry = None if from_diagonal else carry_ref[slot, p]
                ws = []
                for b in range(len(kbs)):
                    logw = z[i, b, p] + sums[i, b, p][:, :Q_BLOCK]
                    total = sums[i, b, p][:, Q_BLOCK:]
                    if carry is not None:
                        logw = logw + visiting(carry, i, b)
                        total = total + visiting(carry, i, b)
                    w = jnp.exp(logw).astype(BF16)
                    half = w.shape[0] // 2
                    if is_partial(i, b):
                        w_last[i, p] = jnp.concatenate([w[:half], w[half:]], axis=1)
                        carry = jnp.concatenate([total[:half], carry[last_rows:Q_BLOCK],
                                                 total[half:], carry[Q_BLOCK + last_rows:]], axis=0)
                    else:
                        ws += [w[:half], w[half:]]
                        carry = total
                w2[i, p] = jnp.concatenate(ws, axis=1)
                if not partial:
                    carry_ref[slot, p] = carry
                top = carry if top is None else jnp.maximum(top, carry)
            tops.append(top)
        for i, (slot, kbs) in enumerate(jobs):
            n_full = len(kbs) - 1 if partial else len(kbs)
            for p in pairs:
                v2 = jnp.concatenate([stack(kv_block(kbs[b], p)[1]) for b in range(n_full)], axis=0)
                pv = _dot(w2[i, p], v2)
                if partial:
                    extra = _dot(w_last[i, p], stack(kv_block(kbs[n_full], p)[1]))
                    pv = jnp.concatenate([pv[:last_rows] + extra, pv[last_rows:]], axis=0)
                acc_ref[slot, p] = pv if from_diagonal else acc_ref[slot, p] + pv
        if between[2] is not None:
            between[2]()
        return tops

    dense = {}

    def merge_stage():
        x = x_ref[...]
        d = x.shape[1]
        h = _rms(x, gm_ref[...]).astype(BF16)
        ya = ya_ref[(step + 1) % 2]
        merged = None
        for i, (y, w_ref) in enumerate(((ya, wa_ref), (yb_ref[...], wb_ref), (yc_ref[...], wc_ref))):
            logits = _dot(h, wg_ref[:, i * d:(i + 1) * d]) + bg_ref[:, i * d:(i + 1) * d]
            term = _dot(y, w_ref[...]) / (1.0 + jnp.exp(-logits))
            merged = term if merged is None else merged + term
        dense["x"] = x + _dot(merged.astype(BF16), wo_ref[...])
        dense["h"] = _rms(dense["x"], gf_ref[...]).astype(BF16)

    def up_stage():
        a = _dot(dense["h"], w1_ref[...])
        b = _dot(dense["h"], w3_ref[...])
        dense["act"] = (a * b / (1.0 + jnp.exp(-a))).astype(BF16)

    def down_stage():
        o_ref[...] = dense["x"] + 0.5 * _dot(dense["act"], w2_ref[...])

    def fetch(kb):
        rows = pl.ds(pl.multiple_of(kb * Q_BLOCK, Q_BLOCK), Q_BLOCK)
        copies = [pltpu.make_async_copy(qkv_hbm.at[rows, pl.ds(SB_W, SB_W)], kblk_ref, sem.at[0]),
                  pltpu.make_async_copy(qkv_hbm.at[rows, pl.ds(2 * SB_W, SB_W)], vblk_ref, sem.at[1])]
        for c in copies:
            c.start()
        for c in copies:
            c.wait()

    def cond(state):
        kb, top = state
        return jnp.logical_and(kb >= 0, top > SB_LOG_CUTOFF)

    def general():
        for slot in range(n_slots):
            def visit(kb, from_diagonal, slot=slot):
                fetch(kb)
                return jnp.max(walk([(slot, [None])], from_diagonal)[0])

            def body(state, visit=visit):
                kb, _ = state
                return kb - 1, visit(kb, False)

            qb = first_qb + slot
            lax.while_loop(cond, body, (qb - 1, visit(qb, True)))

    n_fused = SB_FUSED_BLOCKS
    assert n_fused - 1 <= n_slots

    def usual():
        jobs = [(slot, [n_slots + slot - b for b in range(n_fused)]) for slot in range(n_slots)]
        tops = walk(jobs, True, SB_LAST_BLOCK_ROWS, (merge_stage, up_stage, down_stage))
        return jnp.max(functools.reduce(jnp.maximum, tops))

    def edge():
        @pl.when(step == n_tiles)
        def _():
            merge_stage()
            up_stage()
            down_stage()
        return jnp.float32(jnp.inf)

    fused_top = lax.cond(jnp.logical_and(step >= 1, step < n_tiles), usual, edge)
    lax.cond(jnp.logical_and(fused_top > SB_LOG_CUTOFF, step < n_tiles), general, lambda: None)
    for slot in range(n_slots):
        for p in pairs:
            ya_ref[step % 2, slot * Q_BLOCK:(slot + 1) * Q_BLOCK, sls[p]] = acc_ref[slot, p].astype(BF16)
    kvprev_ref[...] = qkv_ref[:, SB_W:]


def _tail(qkv, x, gm, wg, bg, yb, yc, wa, wb, wc, wo, gf, w1, w3, w2):
    s, w = qkv.shape[0], qkv.shape[1] // 3
    d = x.shape[1]
    n_pairs = w // LANES
    n_slots = SB_QBLOCKS_PER_STEP
    rows = n_slots * Q_BLOCK
    n_tiles = s // rows
    this = lambda i: (jnp.minimum(i, n_tiles - 1), 0)
    last = lambda i: (jnp.maximum(i - 1, 0), 0)
    hbm = pl.BlockSpec(memory_space=pl.ANY)
    return pl.pallas_call(
        _tail_kernel,
        out_shape=jax.ShapeDtypeStruct((s, d), F32),
        grid=(n_tiles + 1,),
        in_specs=[pl.BlockSpec((rows, 3 * w), this), hbm,
                  pl.BlockSpec((rows, d), last), _resident((1, d)), _resident(wg.shape), _resident(bg.shape),
                  pl.BlockSpec((rows, DSA_OUT_W), last), pl.BlockSpec((rows, MEM_W), last),
                  _resident(wa.shape), _resident(wb.shape), _resident(wc.shape), _resident(wo.shape),
                  _resident((1, d)), _resident(w1.shape), _resident(w3.shape), _resident(w2.shape)],
        out_specs=pl.BlockSpec((rows, d), last),
        scratch_shapes=[pltpu.VMEM((n_slots, n_pairs, 2 * Q_BLOCK, LANES), BF16),
                        pltpu.VMEM((n_slots, n_pairs, 2 * Q_BLOCK, LANES), F32),
                        pltpu.VMEM((n_slots, n_pairs, Q_BLOCK, LANES), F32),
                        pltpu.VMEM((2, rows, w), BF16),
                        pltpu.VMEM((rows, 2 * w), BF16),
                        pltpu.VMEM((Q_BLOCK, w), BF16), pltpu.VMEM((Q_BLOCK, w), BF16),
                        pltpu.SemaphoreType.DMA((2,))],
        compiler_params=_params(1),
        name="tail",
    )(qkv, qkv, x, gm, wg, bg, yb, yc, wa, wb, wc, wo, gf, w1, w3, w2)


def _layer(x, mem, p):
    bf = lambda w: w.astype(BF16)
    vec = lambda v: v.reshape(1, -1)
    heads = lambda v, n: jnp.tile(v, n).reshape(1, -1)
    half = HEAD_DIM // 2
    inv_freq = jnp.power(ROPE_THETA, -jnp.arange(half, dtype=F32) / half).reshape(half, 1)

    later = ("w_in", "w_gate", "w_branch_sb", "w_branch_dsa", "w_branch_mem", "w_out",
             "ffn2_w1", "ffn2_w3", "ffn2_w2")
    x, cos, sin, casted = _ffn(x, vec(p["ffn1_norm"]), p["ffn1_w1"], p["ffn1_w3"], p["ffn1_w2"], inv_freq,
                               cast=[p[name] for name in later])
    w = dict(zip(later, casted))
    km, vm = _memkv(mem, vec(p["mem_norm"]), bf(p["w_mem_kv"]), heads(p["kn_mem"], MEM_HEADS))
    n_dsa = DSA_W // HEAD_DIM
    qkv, qb, kb, vb, yc = _proj(
        x, vec(p["mix_norm"]), w["w_in"], cos, sin,
        heads(p["qn_dsa"], n_dsa), heads(p["kn_dsa"], n_dsa), heads(p["qn_mem"], MEM_HEADS), km, vm)
    yb = _dilated(qb, kb, vb)
    return _tail(qkv, x, vec(p["mix_norm"]), w["w_gate"], vec(p["b_gate"]), yb, yc,
                 w["w_branch_sb"], w["w_branch_dsa"], w["w_branch_mem"], w["w_out"],
                 vec(p["ffn2_norm"]), w["ffn2_w1"], w["ffn2_w3"], w["ffn2_w2"])


_PARAM_NAMES = ("ffn1_norm", "ffn1_w1", "ffn1_w3", "ffn1_w2", "mix_norm", "mem_norm", "w_in", "w_mem_kv",
                "qn_dsa", "kn_dsa", "qn_mem", "kn_mem", "w_branch_sb", "w_branch_dsa", "w_branch_mem",
                "w_gate", "b_gate", "w_out", "ffn2_norm", "ffn2_w1", "ffn2_w3", "ffn2_w2")


def kernel(x, mem, ffn1_norm, ffn1_w1, ffn1_w3, ffn1_w2, mix_norm, mem_norm, w_in, w_mem_kv, qn_dsa, kn_dsa, qn_mem, kn_mem, w_branch_sb, w_branch_dsa, w_branch_mem, w_gate, b_gate, w_out, ffn2_norm, ffn2_w1, ffn2_w3, ffn2_w2):
    stacked = dict(zip(_PARAM_NAMES, (ffn1_norm, ffn1_w1, ffn1_w3, ffn1_w2, mix_norm, mem_norm, w_in,
                                      w_mem_kv, qn_dsa, kn_dsa, qn_mem, kn_mem, w_branch_sb, w_branch_dsa,
                                      w_branch_mem, w_gate, b_gate, w_out, ffn2_norm, ffn2_w1, ffn2_w3,
                                      ffn2_w2)))
    depth = ffn1_norm.shape[0]
    outs = []
    for b in range(x.shape[0]):
        xb = x[b]
        for l in range(depth):
            xb = _layer(xb, mem[b], {k: v[l] for k, v in stacked.items()})
        outs.append(xb)
    return jnp.stack(outs)
```

```python
import functools

import jax
import jax.numpy as jnp
from jax import lax
from jax.experimental import pallas as pl
from jax.experimental.pallas import tpu as pltpu

F32 = jnp.float32
BF16 = jnp.bfloat16

HEAD_DIM = 64
SB_HEADS = 8
DSA_GROUPS = ((128, 1), (512, 4), (2048, 16))
DSA_HEADS_PER_GROUP = 4
MEM_HEADS = 4
ROPE_THETA = 10000.0
NORM_EPS = 1e-6
Q_BLOCK = 128
SB_W = SB_HEADS * HEAD_DIM
DSA_W = DSA_HEADS_PER_GROUP * len(DSA_GROUPS) * HEAD_DIM
DSA_OUT_W = DSA_HEADS_PER_GROUP * HEAD_DIM
MEM_W = MEM_HEADS * HEAD_DIM
QK_SCALE = HEAD_DIM ** -0.5
GAIN_QN_DSA, GAIN_KN_DSA, GAIN_QN_MEM, GAIN_KN_MEM = 0, DSA_W, 2 * DSA_W, 2 * DSA_W + MEM_W

LANES = 128
MXU_WIDTH = 256
DSA_UNIT = Q_BLOCK * max(r for _, r in DSA_GROUPS)
DSA_UNROLL = 16
DSA_STAGE = 4
ROW_TILE = 512
VMEM_LIMIT = 56 * 1024 * 1024
FFN_CHUNK = 256
CAST_SLABS = 16
SB_LOG_CUTOFF = -104.0
SB_MASKED_LOGIT = -1e30
SB_FUSED_BLOCKS = 3
SB_QBLOCKS_PER_STEP = 2
SB_LAST_BLOCK_ROWS = 48


def _resident(shape):
    zeros = (0,) * len(shape)
    return pl.BlockSpec(shape, lambda *_: zeros, pipeline_mode=pl.Buffered(1))


def _params(n_axes):
    return pltpu.CompilerParams(dimension_semantics=("arbitrary",) * n_axes,
                                vmem_limit_bytes=VMEM_LIMIT)


def _rms(x, g):
    return x * lax.rsqrt(jnp.mean(x * x, axis=-1, keepdims=True) + NORM_EPS) * g


def _dot(a, b):
    return jnp.dot(a, b, preferred_element_type=F32)


def _dot_nt(a, b):
    return lax.dot_general(a, b, (((1,), (1,)), ((), ())), preferred_element_type=F32)


def _split_bf16(x):
    hi = x.astype(BF16)
    lo = (x - hi.astype(F32)).astype(BF16)
    return hi, lo


def _head_norm(x, g):
    n = x.shape[-1]
    w = min(n, MXU_WIDTH)
    r = lax.broadcasted_iota(jnp.int32, (w, w), 0) // HEAD_DIM
    c = lax.broadcasted_iota(jnp.int32, (w, w), 1) // HEAD_DIM
    bd = jnp.where(r == c, 1.0, 0.0).astype(BF16)
    hi, lo = _split_bf16(x * x)
    ms = jnp.concatenate([_dot(hi[:, j:j + w], bd) + _dot(lo[:, j:j + w], bd) for j in range(0, n, w)],
                         axis=1) * (1.0 / HEAD_DIM)
    return x * lax.rsqrt(ms + NORM_EPS) * g


def _ffn_kernel(x_ref, g_ref, freq_ref, w1_hbm, w3_hbm, w2_hbm, *refs):
    n_cast = (len(refs) - 10) // 2
    o_ref, cos_ref, sin_ref = refs[n_cast:n_cast + 3]
    w1_ref, w3_ref, w2_ref, s1_ref, s3_ref, s2_ref, sem = refs[2 * n_cast + 3:]
    f = w1_ref.shape[1]
    x = x_ref[...]
    h = _rms(x, g_ref[...]).astype(BF16)

    def side_jobs():
        rows = x_ref.shape[0]
        pos = pl.program_id(0) * rows + lax.broadcasted_iota(jnp.int32, (freq_ref.shape[0], rows), 1)
        ang = freq_ref[...] * pos.astype(F32)
        cos_ref[...] = jnp.cos(ang)
        sin_ref[...] = jnp.sin(ang)
        for src, dst in zip(refs[:n_cast], refs[n_cast + 3:2 * n_cast + 3]):
            dst[...] = src[...].astype(BF16)

    def swiglu(w1, w3, w2):
        a = _dot(h, w1)
        b = _dot(h, w3)
        return _dot((a * b / (1.0 + jnp.exp(-a))).astype(BF16), w2)

    def slice_copies(c, slot):
        cols = pl.ds(c * FFN_CHUNK, FFN_CHUNK)
        return [pltpu.make_async_copy(w1_hbm.at[:, cols], s1_ref.at[slot], sem.at[slot, 0]),
                pltpu.make_async_copy(w3_hbm.at[:, cols], s3_ref.at[slot], sem.at[slot, 1]),
                pltpu.make_async_copy(w2_hbm.at[cols, :], s2_ref.at[slot], sem.at[slot, 2])]

    @pl.when(pl.program_id(0) == 0)
    def _():
        n_chunks = f // FFN_CHUNK
        for copy in slice_copies(0, 0):
            copy.start()
        total = None
        for c in range(n_chunks):
            slot = c % 2
            if c + 1 < n_chunks:
                for copy in slice_copies(c + 1, 1 - slot):
                    copy.start()
            for copy in slice_copies(c, slot):
                copy.wait()
            cols = slice(c * FFN_CHUNK, (c + 1) * FFN_CHUNK)
            w1_ref[:, cols] = s1_ref[slot].astype(BF16)
            w3_ref[:, cols] = s3_ref[slot].astype(BF16)
            w2_ref[cols, :] = s2_ref[slot].astype(BF16)
            part = swiglu(w1_ref[:, cols], w3_ref[:, cols], w2_ref[cols, :])
            total = part if total is None else total + part
        o_ref[...] = x + 0.5 * total
        side_jobs()

    @pl.when(pl.program_id(0) > 0)
    def _():
        o_ref[...] = x + 0.5 * swiglu(w1_ref[...], w3_ref[...], w2_ref[...])
        side_jobs()


def _ffn(x, g, w1, w3, w2, inv_freq, cast=()):
    s, d = x.shape
    f = w1.shape[1]
    half = inv_freq.shape[0]
    table = pl.BlockSpec((half, ROW_TILE), lambda i: (0, i))
    steps = s // ROW_TILE
    row = lambda i: (i, 0)
    n_slabs = min(CAST_SLABS, steps)
    per_slab = steps // n_slabs
    slab = lambda i: (i // per_slab, 0)
    slabs = [pl.BlockSpec((w.shape[0] // n_slabs, w.shape[1]), slab) for w in cast]
    hbm = pl.BlockSpec(memory_space=pl.ANY)
    out = pl.pallas_call(
        _ffn_kernel,
        out_shape=(jax.ShapeDtypeStruct((s, d), F32), jax.ShapeDtypeStruct((half, s), F32),
                   jax.ShapeDtypeStruct((half, s), F32), *[jax.ShapeDtypeStruct(w.shape, BF16) for w in cast]),
        grid=(steps,),
        in_specs=[pl.BlockSpec((ROW_TILE, d), row), _resident((1, d)), _resident((half, 1)), hbm, hbm, hbm, *slabs],
        out_specs=(pl.BlockSpec((ROW_TILE, d), row), table, table, *slabs),
        scratch_shapes=[pltpu.VMEM((d, f), BF16), pltpu.VMEM((d, f), BF16), pltpu.VMEM((f, d), BF16),
                        pltpu.VMEM((2, d, FFN_CHUNK), F32), pltpu.VMEM((2, d, FFN_CHUNK), F32),
                        pltpu.VMEM((2, FFN_CHUNK, d), F32), pltpu.SemaphoreType.DMA((2, 3))],
        compiler_params=_params(1),
        name="ffn",
    )(x, g, inv_freq, w1, w3, w2, *cast)
    return out[0], out[1], out[2], list(out[3:])


def _memkv_kernel(mem_ref, g_ref, w_ref, gains_ref, k_ref, v_ref):
    h = _rms(mem_ref[...], g_ref[...]).astype(BF16)
    kv = _dot(h, w_ref[...].astype(BF16))
    k_ref[...] = _head_norm(kv[:, :MEM_W], gains_ref[:, GAIN_KN_MEM:GAIN_KN_MEM + MEM_W]).astype(BF16)
    v_ref[...] = kv[:, MEM_W:].astype(BF16)


def _memkv(mem, g, w, gains):
    m = mem.shape[0]
    out = jax.ShapeDtypeStruct((m, MEM_W), BF16)
    return pl.pallas_call(_memkv_kernel, out_shape=(out, out), name="memkv",
                          compiler_params=pltpu.CompilerParams(vmem_limit_bytes=VMEM_LIMIT),
                          )(mem, g, w, gains)


def _rope(x, cos, sin_signed):
    lane = lax.broadcasted_iota(jnp.int32, (x.shape[0], LANES), 1)
    first_half = (lane % HEAD_DIM) < (HEAD_DIM // 2)
    out = []
    for j in range(x.shape[1] // LANES):
        xs = x[:, j * LANES:(j + 1) * LANES]
        partner = jnp.where(first_half, pltpu.roll(xs, LANES - HEAD_DIM // 2, 1),
                            pltpu.roll(xs, HEAD_DIM // 2, 1))
        out.append(xs * cos + partner * sin_signed)
    return jnp.concatenate(out, axis=1)


def _rope_lanes(cos_half, sin_half):
    half = HEAD_DIM // 2
    f = lax.broadcasted_iota(jnp.int32, (half, LANES), 0)
    lane = lax.broadcasted_iota(jnp.int32, (half, LANES), 1)
    hit = (lane % half) == f
    spread = jnp.where(hit, 1.0, 0.0).astype(BF16)
    signed = jnp.where(hit, jnp.where((lane % HEAD_DIM) < half, -1.0, 1.0), 0.0).astype(BF16)
    c_hi, c_lo = _split_bf16(cos_half)
    s_hi, s_lo = _split_bf16(sin_half)
    return _dot(c_hi, spread) + _dot(c_lo, spread), _dot(s_hi, signed) + _dot(s_lo, signed)


def _proj_kernel(x_ref, g_ref, w_ref, cos_ref, sin_ref, gains_ref, km_ref, vm_ref,
                 qkv_ref, qb_ref, kb_ref, vb_ref, yc_ref):
    qn_d = gains_ref[:, GAIN_QN_DSA:GAIN_QN_DSA + DSA_W]
    kn_d = gains_ref[:, GAIN_KN_DSA:GAIN_KN_DSA + DSA_W]
    qn_m = gains_ref[:, GAIN_QN_MEM:GAIN_QN_MEM + MEM_W]
    h = _rms(x_ref[...], g_ref[...]).astype(BF16)
    cos, sin_signed = _rope_lanes(cos_ref[...].T, sin_ref[...].T)

    def cols(lo, width):
        return _dot(h, w_ref[:, lo:lo + width])

    base = 3 * SB_W
    qc_raw = cols(base + 3 * DSA_W, MEM_W)
    qb_raw = cols(base, DSA_W)
    qc = _head_norm(qc_raw, qn_m) * QK_SCALE
    kb_raw = cols(base + DSA_W, DSA_W)

    lane = lax.broadcasted_iota(jnp.int32, (qc.shape[0], LANES), 1)
    groups = [slice(j * LANES, (j + 1) * LANES) for j in range(MEM_W // LANES)]
    scores = [[_dot_nt(jnp.where((lane < HEAD_DIM) == (half == 0), qc[:, sl], 0.0).astype(BF16), km_ref[:, sl])
               for half in range(2)] for sl in groups]
    qb = _head_norm(qb_raw, qn_d)
    vb_ref[...] = cols(base + 2 * DSA_W, DSA_W)
    for sl, (sc0, sc1) in zip(groups, scores):
        outs = []
        for sc in (sc0, sc1):
            p = jnp.exp(sc - jnp.max(sc, axis=-1, keepdims=True))
            outs.append(_dot(p.astype(BF16), vm_ref[:, sl]) / jnp.sum(p, axis=-1, keepdims=True))
        yc_ref[:, sl] = jnp.where(lane < HEAD_DIM, outs[0], outs[1]).astype(BF16)
    kb = _head_norm(kb_raw, kn_d)
    qb_ref[...] = _rope(qb, cos, sin_signed) * QK_SCALE
    kb_ref[...] = _rope(kb, cos, sin_signed)
    qkv_ref[:, :SB_W] = (cols(0, SB_W) * QK_SCALE).astype(BF16)
    qkv_ref[:, SB_W:] = cols(SB_W, 2 * SB_W).astype(BF16)


def _proj(x, g, w_in, cos, sin_signed, gains, km, vm):
    s, d = x.shape
    row = lambda i: (i, 0)
    tile = lambda w: pl.BlockSpec((ROW_TILE, w), row)
    dsa = jax.ShapeDtypeStruct((s, DSA_W), F32)
    table = pl.BlockSpec((HEAD_DIM // 2, ROW_TILE), lambda i: (0, i))
    return pl.pallas_call(
        _proj_kernel,
        out_shape=(jax.ShapeDtypeStruct((s, 3 * SB_W), BF16), dsa, dsa, dsa,
                   jax.ShapeDtypeStruct((s, MEM_W), BF16)),
        grid=(s // ROW_TILE,),
        in_specs=[tile(d), _resident((1, d)), _resident(w_in.shape), table, table,
                  _resident(gains.shape), _resident(km.shape), _resident(vm.shape)],
        out_specs=(tile(3 * SB_W), tile(DSA_W), tile(DSA_W), tile(DSA_W), tile(MEM_W)),
        compiler_params=_params(1),
        name="proj",
    )(x, g, w_in, cos, sin_signed, gains, km, vm)


def _dsa_kernel(*refs):
    n_g = len(DSA_GROUPS)
    ins = [refs[5 * g:5 * g + 5] for g in range(n_g)]
    o_ref = refs[5 * n_g]
    bias_ref = refs[5 * n_g + 1]
    out_scratch = refs[5 * n_g + 2:5 * n_g + 2 + 2 * n_g]
    stage_scratch = refs[5 * n_g + 2 + 2 * n_g:]
    step = pl.program_id(0)
    lane = lax.broadcasted_iota(jnp.int32, (Q_BLOCK, LANES), 1)
    head0 = lane < HEAD_DIM

    def own_lanes(rows):
        ln = lax.broadcasted_iota(jnp.int32, (rows, LANES), 1)
        rw = lax.broadcasted_iota(jnp.int32, (rows, LANES), 0)
        return (ln < HEAD_DIM) == (rw < rows // 2)

    own_q, own_v = own_lanes(2 * Q_BLOCK), own_lanes(4 * Q_BLOCK)
    qi = lax.broadcasted_iota(jnp.int32, (2 * Q_BLOCK, 2 * Q_BLOCK), 0) % Q_BLOCK
    kj = lax.broadcasted_iota(jnp.int32, (2 * Q_BLOCK, 2 * Q_BLOCK), 1)
    dist = Q_BLOCK + qi - kj
    in_band = (dist >= 0) & (dist <= Q_BLOCK)
    bias_ref[0] = jnp.where(in_band, 0.0, -jnp.inf)
    bias_ref[1] = jnp.where(in_band & (kj >= Q_BLOCK), 0.0, -jnp.inf)
    seq_start = jnp.where(step == 0, 1, 0)

    def attend(qs, ks, vs, biases):
        units = range(len(qs))
        sc, v2 = [], []
        for j in units:
            q2 = jnp.where(own_q, jnp.concatenate([qs[j], qs[j]], axis=0), 0.0).astype(BF16)
            v = vs[j].astype(BF16)
            v2.append(jnp.where(own_v, jnp.concatenate([v, v], axis=0), jnp.zeros((), BF16)))
            sc.append(_dot_nt(q2, ks[j].astype(BF16)))
        p2, m, den = [], [], []
        for j in units:
            s_j = sc[j] + biases[j]
            m.append(jnp.max(s_j, axis=-1, keepdims=True))
            p = jnp.exp(s_j - m[j])
            den.append(jnp.sum(p, axis=-1, keepdims=True))
            p = p.astype(BF16)
            p2.append(jnp.concatenate([p[:Q_BLOCK], p[Q_BLOCK:]], axis=1))
        pv = [_dot(p2[j], v2[j]) for j in units]
        outs = []
        for j in units:
            inv = 1.0 / den[j]
            lse = m[j] + jnp.log(den[j])
            outs.append((pv[j] * jnp.where(head0, inv[:Q_BLOCK], inv[Q_BLOCK:]),
                         jnp.where(head0, lse[:Q_BLOCK], lse[Q_BLOCK:])))
        return outs

    n_trips = DSA_UNIT // Q_BLOCK // DSA_UNROLL
    staged = 0
    for g, (window, r) in enumerate(DSA_GROUPS):
        assert window // r == Q_BLOCK
        q_ref, k_ref, kp_ref, v_ref, vp_ref = ins[g]
        og_ref, lg_ref = out_scratch[2 * g:2 * g + 2]
        prev = Q_BLOCK * r

        if r <= DSA_STAGE:
            assert DSA_UNROLL % r == 0

            def trip(t, first, r=r, prev=prev, q_ref=q_ref, k_ref=k_ref, kp_ref=kp_ref, v_ref=v_ref,
                     vp_ref=vp_ref, og_ref=og_ref, lg_ref=lg_ref):
                qs, ks, vs, biases, rows = [], [], [], [], []
                for j in range(DSA_UNROLL):
                    sub, c = t * (DSA_UNROLL // r) + j // r, j % r
                    rows.append(pl.ds(sub * prev + c, Q_BLOCK, stride=r))
                    qs.append(q_ref[rows[j], :])
                    if first and sub == 0:
                        half = pl.ds(c, Q_BLOCK, stride=r)
                        ks.append(jnp.concatenate([kp_ref[half, :], k_ref[half, :]], axis=0))
                        vs.append(jnp.concatenate([vp_ref[half, :], v_ref[half, :]], axis=0))
                        biases.append(bias_ref[seq_start])
                    else:
                        both = pl.ds((sub - 1) * prev + c, 2 * Q_BLOCK, stride=r)
                        ks.append(k_ref[both, :])
                        vs.append(v_ref[both, :])
                        biases.append(bias_ref[0])
                for j, (o, lse) in enumerate(attend(qs, ks, vs, biases)):
                    og_ref[rows[j], :] = o
                    lg_ref[rows[j], :] = lse

            trip(0, True)

            def later_trip(t, carry, trip=trip):
                trip(t, False)
                return carry

            lax.fori_loop(1, n_trips, later_trip, 0)
        else:
            inner = r // DSA_STAGE
            per_trip = DSA_UNROLL // DSA_STAGE
            assert inner <= DSA_STAGE and prev == DSA_UNIT and DSA_UNROLL % DSA_STAGE == 0 and inner % per_trip == 0
            qs_ref, ks_ref, vs_ref, os_ref, ls_ref = stage_scratch[5 * staged:5 * staged + 5]
            staged += 1
            half = Q_BLOCK * inner
            for c in range(DSA_STAGE):
                coarse = pl.ds(c, half, stride=DSA_STAGE)
                qs_ref[c] = q_ref[coarse, :]
                ks_ref[c, :half] = kp_ref[coarse, :]
                ks_ref[c, half:] = k_ref[coarse, :]
                vs_ref[c, :half] = vp_ref[coarse, :]
                vs_ref[c, half:] = v_ref[coarse, :]

            def staged_trip(t, carry, inner=inner, per_trip=per_trip, qs_ref=qs_ref, ks_ref=ks_ref,
                            vs_ref=vs_ref, os_ref=os_ref, ls_ref=ls_ref):
                bias = bias_ref[seq_start]
                spots = [(c, t * per_trip + f) for f in range(per_trip) for c in range(DSA_STAGE)]
                fine_q = lambda fine: pl.ds(fine, Q_BLOCK, stride=inner)
                fine_k = lambda fine: pl.ds(fine, 2 * Q_BLOCK, stride=inner)
                outs = attend([qs_ref.at[c][fine_q(fine), :] for c, fine in spots],
                              [ks_ref.at[c][fine_k(fine), :] for c, fine in spots],
                              [vs_ref.at[c][fine_k(fine), :] for c, fine in spots],
                              [bias] * len(spots))
                for (c, fine), (o, lse) in zip(spots, outs):
                    os_ref.at[c][fine_q(fine), :] = o
                    ls_ref.at[c][fine_q(fine), :] = lse
                return carry

            lax.fori_loop(0, n_trips, staged_trip, 0)
            for c in range(DSA_STAGE):
                coarse = pl.ds(c, half, stride=DSA_STAGE)
                og_ref[coarse, :] = os_ref[c]
                lg_ref[coarse, :] = ls_ref[c]

    lse = [out_scratch[2 * g + 1][...] for g in range(n_g)]
    top = functools.reduce(jnp.maximum, lse)
    e = [jnp.exp(l - top) for l in lse]
    num = sum(e[g] * out_scratch[2 * g][...] for g in range(n_g))
    o_ref[...] = (num / sum(e)).astype(o_ref.dtype)


def _dilated(q, k, v):
    s = q.shape[0]
    n_pairs = DSA_OUT_W // LANES
    in_specs, stage_scratch = [], []
    for g, (_, r) in enumerate(DSA_GROUPS):
        prev = Q_BLOCK * r
        per_unit = DSA_UNIT // prev
        cur = pl.BlockSpec((DSA_UNIT, LANES), lambda i, p, g=g: (i, n_pairs * g + p))
        prv = pl.BlockSpec((prev, LANES),
                           lambda i, p, g=g, n=per_unit: (jnp.maximum(i * n - 1, 0), n_pairs * g + p))
        in_specs += [cur, cur, prv, cur, prv]
        if r > DSA_STAGE:
            rows = DSA_UNIT // DSA_STAGE
            stage_scratch += [pltpu.VMEM((DSA_STAGE, rows, LANES), F32),
                              pltpu.VMEM((DSA_STAGE, 2 * rows, LANES), F32),
                              pltpu.VMEM((DSA_STAGE, 2 * rows, LANES), F32),
                              pltpu.VMEM((DSA_STAGE, rows, LANES), F32),
                              pltpu.VMEM((DSA_STAGE, rows, LANES), F32)]
    scratch = ([pltpu.VMEM((2, 2 * Q_BLOCK, 2 * Q_BLOCK), F32)]
               + [pltpu.VMEM((DSA_UNIT, LANES), F32)] * (2 * len(DSA_GROUPS)) + stage_scratch)
    args = []
    for _ in DSA_GROUPS:
        args += [q, k, k, v, v]
    return pl.pallas_call(
        _dsa_kernel,
        out_shape=jax.ShapeDtypeStruct((s, DSA_OUT_W), BF16),
        grid=(s // DSA_UNIT, n_pairs),
        in_specs=in_specs,
        out_specs=pl.BlockSpec((DSA_UNIT, LANES), lambda i, p: (i, p)),
        scratch_shapes=scratch,
        compiler_params=_params(2),
        name="dilated",
    )(*args)


def _tail_kernel(qkv_ref, qkv_hbm,
                 x_ref, gm_ref, wg_hbm, bg_ref, yb_ref, yc_ref, wa_hbm, wb_hbm, wc_hbm, wo_hbm,
                 gf_ref, w1_hbm, w3_hbm, w2_hbm, o_ref,
                 q2_ref, carry_ref, acc_ref, ya_ref, kvprev_ref, kblk_ref, vblk_ref, sem,
                 wg_ref, wa_ref, wb_ref, wc_ref, wo_ref, w1_ref, w3_ref, w2_ref, wsem):
    n_slots = SB_QBLOCKS_PER_STEP
    step = pl.program_id(0)
    n_tiles = pl.num_programs(0) - 1
    first_qb = step * n_slots
    n_pairs = SB_W // LANES
    lane = lax.broadcasted_iota(jnp.int32, (2 * Q_BLOCK, LANES), 1)
    row = lax.broadcasted_iota(jnp.int32, (2 * Q_BLOCK, LANES), 0)
    own_lanes = (lane < HEAD_DIM) == (row < Q_BLOCK)
    before = lane < (row % Q_BLOCK)
    j = lax.broadcasted_iota(jnp.int32, (2 * Q_BLOCK, 2 * Q_BLOCK), 0) % Q_BLOCK
    s = lax.broadcasted_iota(jnp.int32, (2 * Q_BLOCK, 2 * Q_BLOCK), 1)
    suffix = jnp.where((s >= Q_BLOCK) | (j >= s), 1.0, 0.0).astype(BF16)

    def stack(x):
        return jnp.where(own_lanes, jnp.concatenate([x, x], axis=0), jnp.zeros((), x.dtype))

    pairs = range(n_pairs)
    sls = [slice(p * LANES, (p + 1) * LANES) for p in pairs]
    for slot in range(n_slots):
        for p in pairs:
            q2_ref[slot, p] = stack(qkv_ref[slot * Q_BLOCK:(slot + 1) * Q_BLOCK, sls[p]])

    def kv_block(handle, p):
        if handle is None:
            return kblk_ref[:, sls[p]], vblk_ref[:, sls[p]]
        rows = slice((handle % n_slots) * Q_BLOCK, (handle % n_slots + 1) * Q_BLOCK)
        k_cols = pl.ds(p * LANES, LANES)
        v_cols = pl.ds(SB_W + p * LANES, LANES)
        if handle < n_slots:
            return kvprev_ref[rows, k_cols], kvprev_ref[rows, v_cols]
        return qkv_ref[rows, pl.ds(SB_W + p * LANES, LANES)], qkv_ref[rows, pl.ds(2 * SB_W + p * LANES, LANES)]

    def walk(jobs, from_diagonal, last_rows=Q_BLOCK, between=(None, None, None)):
        partial = last_rows < Q_BLOCK
        assert not partial or (from_diagonal and all(len(kbs) > 1 for _, kbs in jobs))
        chains = [(i, b) for i, (_, kbs) in enumerate(jobs) for b in range(len(kbs))]

        def is_partial(i, b):
            return partial and b == len(jobs[i][1]) - 1

        def visiting(x, i, b):
            if is_partial(i, b):
                return jnp.concatenate([x[:last_rows], x[Q_BLOCK:Q_BLOCK + last_rows]], axis=0)
            return x

        z = {}
        for i, (slot, kbs) in enumerate(jobs):
            n_full = len(kbs) - 1 if partial else len(kbs)
            for p in pairs:
                for b in range(0, n_full - 1, 2):
                    keys = jnp.concatenate([kv_block(kbs[b], p)[0], kv_block(kbs[b + 1], p)[0]], axis=0)
                    both = _dot_nt(q2_ref[slot, p], keys)
                    z[i, b, p], z[i, b + 1, p] = both[:, :Q_BLOCK], both[:, Q_BLOCK:]
                for b in list(range(n_full - n_full % 2, n_full)) + list(range(n_full, len(kbs))):
                    z[i, b, p] = _dot_nt(visiting(q2_ref[slot, p], i, b), kv_block(kbs[b], p)[0])
        if between[0] is not None:
            between[0]()
        split = {}
        for i, b in chains:
            for p in pairs:
                if from_diagonal and b == 0:
                    z[i, b, p] = jnp.where(before, z[i, b, p], SB_MASKED_LOGIT)
                nz = -z[i, b, p]
                lf = jnp.minimum(nz, 0.0) - jnp.log(1.0 + jnp.exp(jnp.minimum(z[i, b, p], nz)))
                split[i, b, p] = jnp.concatenate(_split_bf16(lf), axis=1)
        sums = {(i, b, p): _dot(split[i, b, p], suffix) for i, b in chains for p in pairs}
        if between[1] is not None:
            between[1]()
        w2, w_last, tops = {}, {}, []
        for i, (slot, kbs) in enumerate(jobs):
            top = None
            for p in pairs:
                carry = None if from_diagonal else carry_ref[slot, p]
                ws = []
                for b in range(len(kbs)):
                    logw = z[i, b, p] + sums[i, b, p][:, :Q_BLOCK]
                    total = sums[i, b, p][:, Q_BLOCK:]
                    if carry is not None:
                        logw = logw + visiting(carry, i, b)
                        total = total + visiting(carry, i, b)
                    w = jnp.exp(logw).astype(BF16)
                    half = w.shape[0] // 2
                    if is_partial(i, b):
                        w_last[i, p] = jnp.concatenate([w[:half], w[half:]], axis=1)
                        carry = jnp.concatenate([total[:half], carry[last_rows:Q_BLOCK],
                                                 total[half:], carry[Q_BLOCK + last_rows:]], axis=0)
                    else:
                        ws += [w[:half], w[half:]]
                        carry = total
                w2[i, p] = jnp.concatenate(ws, axis=1)
                if not partial:
                    carry_ref[slot, p] = carry
                top = carry if top is None else jnp.maximum(top, carry)
            tops.append(top)
        for i, (slot, kbs) in enumerate(jobs):
            n_full = len(kbs) - 1 if partial else len(kbs)
            for p in pairs:
                v2 = jnp.concatenate([stack(kv_block(kbs[b], p)[1]) for b in range(n_full)], axis=0)
                pv = _dot(w2[i, p], v2)
                if partial:
                    extra = _dot(w_last[i, p], stack(kv_block(kbs[n_full], p)[1]))
                    pv = jnp.concatenate([pv[:last_rows] + extra, pv[last_rows:]], axis=0)
                acc_ref[slot, p] = pv if from_diagonal else acc_ref[slot, p] + pv
        if between[2] is not None:
            between[2]()
        return tops

    dense = {}

    def merge_stage():
        x = x_ref[...]
        d = x.shape[1]
        h = _rms(x, gm_ref[...]).astype(BF16)
        ya = ya_ref[(step + 1) % 2]
        merged = None
        for i, (y, w_ref) in enumerate(((ya, wa_ref), (yb_ref[...], wb_ref), (yc_ref[...], wc_ref))):
            logits = _dot(h, wg_ref[:, i * d:(i + 1) * d]) + bg_ref[:, i * d:(i + 1) * d]
            term = _dot(y, w_ref[...]) / (1.0 + jnp.exp(-logits))
            merged = term if merged is None else merged + term
        dense["x"] = x + _dot(merged.astype(BF16), wo_ref[...])
        dense["h"] = _rms(dense["x"], gf_ref[...]).astype(BF16)

    def up_stage():
        a = _dot(dense["h"], w1_ref[...])
        b = _dot(dense["h"], w3_ref[...])
        dense["act"] = (a * b / (1.0 + jnp.exp(-a))).astype(BF16)

    def down_stage():
        o_ref[...] = dense["x"] + 0.5 * _dot(dense["act"], w2_ref[...])

    def fetch(kb):
        rows = pl.ds(pl.multiple_of(kb * Q_BLOCK, Q_BLOCK), Q_BLOCK)
        copies = [pltpu.make_async_copy(qkv_hbm.at[rows, pl.ds(SB_W, SB_W)], kblk_ref, sem.at[0]),
                  pltpu.make_async_copy(qkv_hbm.at[rows, pl.ds(2 * SB_W, SB_W)], vblk_ref, sem.at[1])]
        for c in copies:
            c.start()
        for c in copies:
            c.wait()

    def cond(state):
        kb, top = state
        return jnp.logical_and(kb >= 0, top > SB_LOG_CUTOFF)

    def general():
        for slot in range(n_slots):
            def visit(kb, from_diagonal, slot=slot):
                fetch(kb)
                return jnp.max(walk([(slot, [None])], from_diagonal)[0])

            def body(state, visit=visit):
                kb, _ = state
                return kb - 1, visit(kb, False)

            qb = first_qb + slot
            lax.while_loop(cond, body, (qb - 1, visit(qb, True)))

    n_fused = SB_FUSED_BLOCKS
    assert n_fused - 1 <= n_slots

    def usual():
        jobs = [(slot, [n_slots + slot - b for b in range(n_fused)]) for slot in range(n_slots)]
        tops = walk(jobs, True, SB_LAST_BLOCK_ROWS, (merge_stage, up_stage, down_stage))
        return jnp.max(functools.reduce(jnp.maximum, tops))

    def edge():
        @pl.when(step == n_tiles)
        def _():
            merge_stage()
            up_stage()
            down_stage()
        return jnp.float32(jnp.inf)

    weights = ((wg_hbm, wg_ref), (wa_hbm, wa_ref), (wb_hbm, wb_ref), (wc_hbm, wc_ref), (wo_hbm, wo_ref),
               (w1_hbm, w1_ref), (w3_hbm, w3_ref), (w2_hbm, w2_ref))
    weight_copies = [pltpu.make_async_copy(src, dst, wsem.at[i]) for i, (src, dst) in enumerate(weights)]

    @pl.when(step == 0)
    def _():
        for copy in weight_copies:
            copy.start()

    @pl.when(step == 1)
    def _():
        for copy in weight_copies:
            copy.wait()

    fused_top = lax.cond(jnp.logical_and(step >= 1, step < n_tiles), usual, edge)
    lax.cond(jnp.logical_and(fused_top > SB_LOG_CUTOFF, step < n_tiles), general, lambda: None)
    for slot in range(n_slots):
        for p in pairs:
            ya_ref[step % 2, slot * Q_BLOCK:(slot + 1) * Q_BLOCK, sls[p]] = acc_ref[slot, p].astype(BF16)
    kvprev_ref[...] = qkv_ref[:, SB_W:]


def _tail(qkv, x, gm, wg, bg, yb, yc, wa, wb, wc, wo, gf, w1, w3, w2):
    s, w = qkv.shape[0], qkv.shape[1] // 3
    d = x.shape[1]
    n_pairs = w // LANES
    n_slots = SB_QBLOCKS_PER_STEP
    rows = n_slots * Q_BLOCK
    n_tiles = s // rows
    this = lambda i: (jnp.minimum(i, n_tiles - 1), 0)
    last = lambda i: (jnp.maximum(i - 1, 0), 0)
    hbm = pl.BlockSpec(memory_space=pl.ANY)
    return pl.pallas_call(
        _tail_kernel,
        out_shape=jax.ShapeDtypeStruct((s, d), F32),
        grid=(n_tiles + 1,),
        in_specs=[pl.BlockSpec((rows, 3 * w), this), hbm,
                  pl.BlockSpec((rows, d), last), _resident((1, d)), hbm, _resident(bg.shape),
                  pl.BlockSpec((rows, DSA_OUT_W), last), pl.BlockSpec((rows, MEM_W), last),
                  hbm, hbm, hbm, hbm, _resident((1, d)), hbm, hbm, hbm],
        out_specs=pl.BlockSpec((rows, d), last),
        scratch_shapes=[pltpu.VMEM((n_slots, n_pairs, 2 * Q_BLOCK, LANES), BF16),
                        pltpu.VMEM((n_slots, n_pairs, 2 * Q_BLOCK, LANES), F32),
                        pltpu.VMEM((n_slots, n_pairs, Q_BLOCK, LANES), F32),
                        pltpu.VMEM((2, rows, w), BF16),
                        pltpu.VMEM((rows, 2 * w), BF16),
                        pltpu.VMEM((Q_BLOCK, w), BF16), pltpu.VMEM((Q_BLOCK, w), BF16),
                        pltpu.SemaphoreType.DMA((2,)),
                        *[pltpu.VMEM(m.shape, BF16) for m in (wg, wa, wb, wc, wo, w1, w3, w2)],
                        pltpu.SemaphoreType.DMA((8,))],
        compiler_params=_params(1),
        name="tail",
    )(qkv, qkv, x, gm, wg, bg, yb, yc, wa, wb, wc, wo, gf, w1, w3, w2)


def _layer(x, mem, p):
    vec = lambda v: v.reshape(1, -1)
    n_dsa = DSA_W // HEAD_DIM
    gains = jnp.concatenate([jnp.tile(p["qn_dsa"], n_dsa), jnp.tile(p["kn_dsa"], n_dsa),
                             jnp.tile(p["qn_mem"], MEM_HEADS), jnp.tile(p["kn_mem"], MEM_HEADS)]).reshape(1, -1)
    half = HEAD_DIM // 2
    inv_freq = jnp.power(ROPE_THETA, -jnp.arange(half, dtype=F32) / half).reshape(half, 1)

    later = ("w_in", "w_gate", "w_branch_sb", "w_branch_dsa", "w_branch_mem", "w_out",
             "ffn2_w1", "ffn2_w3", "ffn2_w2")
    x, cos, sin, casted = _ffn(x, vec(p["ffn1_norm"]), p["ffn1_w1"], p["ffn1_w3"], p["ffn1_w2"], inv_freq,
                               cast=[p[name] for name in later])
    w = dict(zip(later, casted))
    km, vm = _memkv(mem, vec(p["mem_norm"]), p["w_mem_kv"], gains)
    qkv, qb, kb, vb, yc = _proj(x, vec(p["mix_norm"]), w["w_in"], cos, sin, gains, km, vm)
    yb = _dilated(qb, kb, vb)
    return _tail(qkv, x, vec(p["mix_norm"]), w["w_gate"], vec(p["b_gate"]), yb, yc,
                 w["w_branch_sb"], w["w_branch_dsa"], w["w_branch_mem"], w["w_out"],
                 vec(p["ffn2_norm"]), w["ffn2_w1"], w["ffn2_w3"], w["ffn2_w2"])


_PARAM_NAMES = ("ffn1_norm", "ffn1_w1", "ffn1_w3", "ffn1_w2", "mix_norm", "mem_norm", "w_in", "w_mem_kv",
                "qn_dsa", "kn_dsa", "qn_mem", "kn_mem", "w_branch_sb", "w_branch_dsa", "w_branch_mem",
                "w_gate", "b_gate", "w_out", "ffn2_norm", "ffn2_w1", "ffn2_w3", "ffn2_w2")


def kernel(x, mem, ffn1_norm, ffn1_w1, ffn1_w3, ffn1_w2, mix_norm, mem_norm, w_in, w_mem_kv, qn_dsa, kn_dsa, qn_mem, kn_mem, w_branch_sb, w_branch_dsa, w_branch_mem, w_gate, b_gate, w_out, ffn2_norm, ffn2_w1, ffn2_w3, ffn2_w2):
    stacked = dict(zip(_PARAM_NAMES, (ffn1_norm, ffn1_w1, ffn1_w3, ffn1_w2, mix_norm, mem_norm, w_in,
                                      w_mem_kv, qn_dsa, kn_dsa, qn_mem, kn_mem, w_branch_sb, w_branch_dsa,
                                      w_branch_mem, w_gate, b_gate, w_out, ffn2_norm, ffn2_w1, ffn2_w3,
                                      ffn2_w2)))
    depth = ffn1_norm.shape[0]
    outs = []
    for b in range(x.shape[0]):
        xb = x[b]
        for l in range(depth):
            xb = _layer(xb, mem[b], {k: v[l] for k, v in stacked.items()})
        outs.append(xb)
    return jnp.stack(outs)
```

```python
import functools

import jax
import jax.numpy as jnp
from jax import lax
from jax.experimental import pallas as pl
from jax.experimental.pallas import tpu as pltpu

F32 = jnp.float32
BF16 = jnp.bfloat16

HEAD_DIM = 64
SB_HEADS = 8
DSA_GROUPS = ((128, 1), (512, 4), (2048, 16))
DSA_HEADS_PER_GROUP = 4
MEM_HEADS = 4
ROPE_THETA = 10000.0
NORM_EPS = 1e-6
Q_BLOCK = 128
SB_W = SB_HEADS * HEAD_DIM
DSA_W = DSA_HEADS_PER_GROUP * len(DSA_GROUPS) * HEAD_DIM
DSA_OUT_W = DSA_HEADS_PER_GROUP * HEAD_DIM
MEM_W = MEM_HEADS * HEAD_DIM
QK_SCALE = HEAD_DIM ** -0.5
GAIN_QN_DSA, GAIN_KN_DSA, GAIN_QN_MEM, GAIN_KN_MEM = 0, DSA_W, 2 * DSA_W, 2 * DSA_W + MEM_W

LANES = 128
MXU_WIDTH = 256
DSA_UNIT = Q_BLOCK * max(r for _, r in DSA_GROUPS)
DSA_UNROLL = 16
DSA_STAGE = 4
ROW_TILE = 512
VMEM_LIMIT = 56 * 1024 * 1024
FFN_CHUNK = 256
CAST_SLABS = 16
SB_LOG_CUTOFF = -104.0
SB_MASKED_LOGIT = -1e30
SB_FUSED_BLOCKS = 3
SB_QBLOCKS_PER_STEP = 2
SB_LAST_BLOCK_ROWS = 48


def _resident(shape):
    zeros = (0,) * len(shape)
    return pl.BlockSpec(shape, lambda *_: zeros, pipeline_mode=pl.Buffered(1))


def _params(n_axes):
    return pltpu.CompilerParams(dimension_semantics=("arbitrary",) * n_axes,
                                vmem_limit_bytes=VMEM_LIMIT)


def _rms(x, g):
    return x * lax.rsqrt(jnp.mean(x * x, axis=-1, keepdims=True) + NORM_EPS) * g


def _dot(a, b):
    return jnp.dot(a, b, preferred_element_type=F32)


def _dot_nt(a, b):
    return lax.dot_general(a, b, (((1,), (1,)), ((), ())), preferred_element_type=F32)


def _split_bf16(x):
    hi = x.astype(BF16)
    lo = (x - hi.astype(F32)).astype(BF16)
    return hi, lo


def _head_norm(x, g):
    n = x.shape[-1]
    w = min(n, MXU_WIDTH)
    r = lax.broadcasted_iota(jnp.int32, (w, w), 0) // HEAD_DIM
    c = lax.broadcasted_iota(jnp.int32, (w, w), 1) // HEAD_DIM
    bd = jnp.where(r == c, 1.0, 0.0).astype(BF16)
    hi, lo = _split_bf16(x * x)
    ms = jnp.concatenate([_dot(hi[:, j:j + w], bd) + _dot(lo[:, j:j + w], bd) for j in range(0, n, w)],
                         axis=1) * (1.0 / HEAD_DIM)
    return x * lax.rsqrt(ms + NORM_EPS) * g


def _ffn_kernel(x_ref, g_ref, freq_ref, w1_hbm, w3_hbm, w2_hbm, *refs):
    n_cast = (len(refs) - 10) // 2
    o_ref, cos_ref, sin_ref = refs[n_cast:n_cast + 3]
    w1_ref, w3_ref, w2_ref, s1_ref, s3_ref, s2_ref, sem = refs[2 * n_cast + 3:]
    f = w1_ref.shape[1]
    x = x_ref[...]
    h = _rms(x, g_ref[...]).astype(BF16)

    def side_jobs():
        rows = x_ref.shape[0]
        pos = pl.program_id(0) * rows + lax.broadcasted_iota(jnp.int32, (freq_ref.shape[0], rows), 1)
        ang = freq_ref[...] * pos.astype(F32)
        cos_ref[...] = jnp.cos(ang)
        sin_ref[...] = jnp.sin(ang)
        for src, dst in zip(refs[:n_cast], refs[n_cast + 3:2 * n_cast + 3]):
            dst[...] = src[...].astype(BF16)

    def swiglu(w1, w3, w2):
        a = _dot(h, w1)
        b = _dot(h, w3)
        return _dot((a * b / (1.0 + jnp.exp(-a))).astype(BF16), w2)

    def slice_copies(c, slot):
        cols = pl.ds(c * FFN_CHUNK, FFN_CHUNK)
        return [pltpu.make_async_copy(w1_hbm.at[:, cols], s1_ref.at[slot], sem.at[slot, 0]),
                pltpu.make_async_copy(w3_hbm.at[:, cols], s3_ref.at[slot], sem.at[slot, 1]),
                pltpu.make_async_copy(w2_hbm.at[cols, :], s2_ref.at[slot], sem.at[slot, 2])]

    @pl.when(pl.program_id(0) == 0)
    def _():
        n_chunks = f // FFN_CHUNK
        for copy in slice_copies(0, 0):
            copy.start()
        total = None
        for c in range(n_chunks):
            slot = c % 2
            if c + 1 < n_chunks:
                for copy in slice_copies(c + 1, 1 - slot):
                    copy.start()
            for copy in slice_copies(c, slot):
                copy.wait()
            cols = slice(c * FFN_CHUNK, (c + 1) * FFN_CHUNK)
            w1_ref[:, cols] = s1_ref[slot].astype(BF16)
            w3_ref[:, cols] = s3_ref[slot].astype(BF16)
            w2_ref[cols, :] = s2_ref[slot].astype(BF16)
            part = swiglu(w1_ref[:, cols], w3_ref[:, cols], w2_ref[cols, :])
            total = part if total is None else total + part
        o_ref[...] = x + 0.5 * total
        side_jobs()

    @pl.when(pl.program_id(0) > 0)
    def _():
        o_ref[...] = x + 0.5 * swiglu(w1_ref[...], w3_ref[...], w2_ref[...])
        side_jobs()


def _ffn(x, g, w1, w3, w2, inv_freq, cast=()):
    s, d = x.shape
    f = w1.shape[1]
    half = inv_freq.shape[0]
    table = pl.BlockSpec((half, ROW_TILE), lambda i: (0, i))
    steps = s // ROW_TILE
    row = lambda i: (i, 0)
    n_slabs = min(CAST_SLABS, steps)
    per_slab = steps // n_slabs
    slab = lambda i: (i // per_slab, 0)
    slabs = [pl.BlockSpec((w.shape[0] // n_slabs, w.shape[1]), slab) for w in cast]
    hbm = pl.BlockSpec(memory_space=pl.ANY)
    out = pl.pallas_call(
        _ffn_kernel,
        out_shape=(jax.ShapeDtypeStruct((s, d), F32), jax.ShapeDtypeStruct((half, s), F32),
                   jax.ShapeDtypeStruct((half, s), F32), *[jax.ShapeDtypeStruct(w.shape, BF16) for w in cast]),
        grid=(steps,),
        in_specs=[pl.BlockSpec((ROW_TILE, d), row), _resident((1, d)), _resident((half, 1)), hbm, hbm, hbm, *slabs],
        out_specs=(pl.BlockSpec((ROW_TILE, d), row), table, table, *slabs),
        scratch_shapes=[pltpu.VMEM((d, f), BF16), pltpu.VMEM((d, f), BF16), pltpu.VMEM((f, d), BF16),
                        pltpu.VMEM((2, d, FFN_CHUNK), F32), pltpu.VMEM((2, d, FFN_CHUNK), F32),
                        pltpu.VMEM((2, FFN_CHUNK, d), F32), pltpu.SemaphoreType.DMA((2, 3))],
        compiler_params=_params(1),
        name="ffn",
    )(x, g, inv_freq, w1, w3, w2, *cast)
    return out[0], out[1], out[2], list(out[3:])


def _memkv_kernel(mem_ref, g_ref, w_ref, gains_ref, k_ref, v_ref):
    h = _rms(mem_ref[...], g_ref[...]).astype(BF16)
    kv = _dot(h, w_ref[...].astype(BF16))
    k_ref[...] = _head_norm(kv[:, :MEM_W], gains_ref[:, GAIN_KN_MEM:GAIN_KN_MEM + MEM_W]).astype(BF16)
    v_ref[...] = kv[:, MEM_W:].astype(BF16)


def _memkv(mem, g, w, gains):
    m = mem.shape[0]
    out = jax.ShapeDtypeStruct((m, MEM_W), BF16)
    return pl.pallas_call(_memkv_kernel, out_shape=(out, out), name="memkv",
                          compiler_params=pltpu.CompilerParams(vmem_limit_bytes=VMEM_LIMIT),
                          )(mem, g, w, gains)


def _rope(x, cos, sin_signed):
    lane = lax.broadcasted_iota(jnp.int32, (x.shape[0], LANES), 1)
    first_half = (lane % HEAD_DIM) < (HEAD_DIM // 2)
    out = []
    for j in range(x.shape[1] // LANES):
        xs = x[:, j * LANES:(j + 1) * LANES]
        partner = jnp.where(first_half, pltpu.roll(xs, LANES - HEAD_DIM // 2, 1),
                            pltpu.roll(xs, HEAD_DIM // 2, 1))
        out.append(xs * cos + partner * sin_signed)
    return jnp.concatenate(out, axis=1)


def _rope_lanes(cos_half, sin_half):
    half = HEAD_DIM // 2
    f = lax.broadcasted_iota(jnp.int32, (half, LANES), 0)
    lane = lax.broadcasted_iota(jnp.int32, (half, LANES), 1)
    hit = (lane % half) == f
    spread = jnp.where(hit, 1.0, 0.0).astype(BF16)
    signed = jnp.where(hit, jnp.where((lane % HEAD_DIM) < half, -1.0, 1.0), 0.0).astype(BF16)
    c_hi, c_lo = _split_bf16(cos_half)
    s_hi, s_lo = _split_bf16(sin_half)
    return _dot(c_hi, spread) + _dot(c_lo, spread), _dot(s_hi, signed) + _dot(s_lo, signed)


def _proj_kernel(x_ref, g_ref, w_ref, cos_ref, sin_ref, gains_ref, km_ref, vm_ref,
                 qkv_ref, qb_ref, kb_ref, vb_ref, yc_ref):
    qn_d = gains_ref[:, GAIN_QN_DSA:GAIN_QN_DSA + DSA_W]
    kn_d = gains_ref[:, GAIN_KN_DSA:GAIN_KN_DSA + DSA_W]
    qn_m = gains_ref[:, GAIN_QN_MEM:GAIN_QN_MEM + MEM_W]
    h = _rms(x_ref[...], g_ref[...]).astype(BF16)
    cos, sin_signed = _rope_lanes(cos_ref[...].T, sin_ref[...].T)

    def cols(lo, width):
        return _dot(h, w_ref[:, lo:lo + width])

    base = 3 * SB_W
    qc_raw = cols(base + 3 * DSA_W, MEM_W)
    qb_raw = cols(base, DSA_W)
    qc = _head_norm(qc_raw, qn_m) * QK_SCALE
    kb_raw = cols(base + DSA_W, DSA_W)

    lane = lax.broadcasted_iota(jnp.int32, (qc.shape[0], LANES), 1)
    groups = [slice(j * LANES, (j + 1) * LANES) for j in range(MEM_W // LANES)]
    scores = [[_dot_nt(jnp.where((lane < HEAD_DIM) == (half == 0), qc[:, sl], 0.0).astype(BF16), km_ref[:, sl])
               for half in range(2)] for sl in groups]
    qb = _head_norm(qb_raw, qn_d)
    vb_ref[...] = cols(base + 2 * DSA_W, DSA_W)
    for sl, (sc0, sc1) in zip(groups, scores):
        outs = []
        for sc in (sc0, sc1):
            p = jnp.exp(sc - jnp.max(sc, axis=-1, keepdims=True))
            outs.append(_dot(p.astype(BF16), vm_ref[:, sl]) / jnp.sum(p, axis=-1, keepdims=True))
        yc_ref[:, sl] = jnp.where(lane < HEAD_DIM, outs[0], outs[1]).astype(BF16)
    kb = _head_norm(kb_raw, kn_d)
    qb_ref[...] = _rope(qb, cos, sin_signed) * QK_SCALE
    kb_ref[...] = _rope(kb, cos, sin_signed)
    qkv_ref[:, :SB_W] = (cols(0, SB_W) * QK_SCALE).astype(BF16)
    qkv_ref[:, SB_W:] = cols(SB_W, 2 * SB_W).astype(BF16)


def _proj(x, g, w_in, cos, sin_signed, gains, km, vm):
    s, d = x.shape
    row = lambda i: (i, 0)
    tile = lambda w: pl.BlockSpec((ROW_TILE, w), row)
    dsa = jax.ShapeDtypeStruct((s, DSA_W), F32)
    table = pl.BlockSpec((HEAD_DIM // 2, ROW_TILE), lambda i: (0, i))
    return pl.pallas_call(
        _proj_kernel,
        out_shape=(jax.ShapeDtypeStruct((s, 3 * SB_W), BF16), dsa, dsa, dsa,
                   jax.ShapeDtypeStruct((s, MEM_W), BF16)),
        grid=(s // ROW_TILE,),
        in_specs=[tile(d), _resident((1, d)), _resident(w_in.shape), table, table,
                  _resident(gains.shape), _resident(km.shape), _resident(vm.shape)],
        out_specs=(tile(3 * SB_W), tile(DSA_W), tile(DSA_W), tile(DSA_W), tile(MEM_W)),
        compiler_params=_params(1),
        name="proj",
    )(x, g, w_in, cos, sin_signed, gains, km, vm)


def _dsa_kernel(*refs):
    n_g = len(DSA_GROUPS)
    ins = [refs[5 * g:5 * g + 5] for g in range(n_g)]
    o_ref = refs[5 * n_g]
    bias_ref = refs[5 * n_g + 1]
    out_scratch = refs[5 * n_g + 2:5 * n_g + 2 + 2 * n_g]
    stage_scratch = refs[5 * n_g + 2 + 2 * n_g:]
    step = pl.program_id(0)
    lane = lax.broadcasted_iota(jnp.int32, (Q_BLOCK, LANES), 1)
    head0 = lane < HEAD_DIM

    def own_lanes(rows):
        ln = lax.broadcasted_iota(jnp.int32, (rows, LANES), 1)
        rw = lax.broadcasted_iota(jnp.int32, (rows, LANES), 0)
        return (ln < HEAD_DIM) == (rw < rows // 2)

    own_q, own_v = own_lanes(2 * Q_BLOCK), own_lanes(4 * Q_BLOCK)
    qi = lax.broadcasted_iota(jnp.int32, (2 * Q_BLOCK, 2 * Q_BLOCK), 0) % Q_BLOCK
    kj = lax.broadcasted_iota(jnp.int32, (2 * Q_BLOCK, 2 * Q_BLOCK), 1)
    dist = Q_BLOCK + qi - kj
    in_band = (dist >= 0) & (dist <= Q_BLOCK)
    bias_ref[0] = jnp.where(in_band, 0.0, -jnp.inf)
    bias_ref[1] = jnp.where(in_band & (kj >= Q_BLOCK), 0.0, -jnp.inf)
    seq_start = jnp.where(step == 0, 1, 0)

    def attend(qs, ks, vs, biases):
        units = range(len(qs))
        sc, v2 = [], []
        for j in units:
            q = qs[j].astype(BF16)
            q2 = jnp.where(own_q, jnp.concatenate([q, q], axis=0), jnp.zeros((), BF16))
            v = vs[j].astype(BF16)
            v2.append(jnp.where(own_v, jnp.concatenate([v, v], axis=0), jnp.zeros((), BF16)))
            sc.append(_dot_nt(q2, ks[j].astype(BF16)))
        p2, m, den = [], [], []
        for j in units:
            s_j = sc[j] + biases[j]
            m.append(jnp.max(s_j, axis=-1, keepdims=True))
            p = jnp.exp(s_j - m[j])
            den.append(jnp.sum(p, axis=-1, keepdims=True))
            p = p.astype(BF16)
            p2.append(jnp.concatenate([p[:Q_BLOCK], p[Q_BLOCK:]], axis=1))
        pv = [_dot(p2[j], v2[j]) for j in units]
        outs = []
        for j in units:
            inv = 1.0 / den[j]
            lse = m[j] + jnp.log(den[j])
            outs.append((pv[j] * jnp.where(head0, inv[:Q_BLOCK], inv[Q_BLOCK:]),
                         jnp.where(head0, lse[:Q_BLOCK], lse[Q_BLOCK:])))
        return outs

    n_trips = DSA_UNIT // Q_BLOCK // DSA_UNROLL
    staged = 0
    for g, (window, r) in enumerate(DSA_GROUPS):
        assert window // r == Q_BLOCK
        q_ref, k_ref, kp_ref, v_ref, vp_ref = ins[g]
        og_ref, lg_ref = out_scratch[2 * g:2 * g + 2]
        prev = Q_BLOCK * r

        if r <= DSA_STAGE:
            assert DSA_UNROLL % r == 0

            def trip(t, first, r=r, prev=prev, q_ref=q_ref, k_ref=k_ref, kp_ref=kp_ref, v_ref=v_ref,
                     vp_ref=vp_ref, og_ref=og_ref, lg_ref=lg_ref):
                qs, ks, vs, biases, rows = [], [], [], [], []
                for j in range(DSA_UNROLL):
                    sub, c = t * (DSA_UNROLL // r) + j // r, j % r
                    rows.append(pl.ds(sub * prev + c, Q_BLOCK, stride=r))
                    qs.append(q_ref[rows[j], :])
                    if first and sub == 0:
                        half = pl.ds(c, Q_BLOCK, stride=r)
                        ks.append(jnp.concatenate([kp_ref[half, :], k_ref[half, :]], axis=0))
                        vs.append(jnp.concatenate([vp_ref[half, :], v_ref[half, :]], axis=0))
                        biases.append(bias_ref[seq_start])
                    else:
                        both = pl.ds((sub - 1) * prev + c, 2 * Q_BLOCK, stride=r)
                        ks.append(k_ref[both, :])
                        vs.append(v_ref[both, :])
                        biases.append(bias_ref[0])
                for j, (o, lse) in enumerate(attend(qs, ks, vs, biases)):
                    og_ref[rows[j], :] = o
                    lg_ref[rows[j], :] = lse

            trip(0, True)

            def later_trip(t, carry, trip=trip):
                trip(t, False)
                return carry

            lax.fori_loop(1, n_trips, later_trip, 0)
        else:
            inner = r // DSA_STAGE
            per_trip = DSA_UNROLL // DSA_STAGE
            assert inner <= DSA_STAGE and prev == DSA_UNIT and DSA_UNROLL % DSA_STAGE == 0 and inner % per_trip == 0
            qs_ref, ks_ref, vs_ref, os_ref, ls_ref = stage_scratch[5 * staged:5 * staged + 5]
            staged += 1
            half = Q_BLOCK * inner
            for c in range(DSA_STAGE):
                coarse = pl.ds(c, half, stride=DSA_STAGE)
                qs_ref[c] = q_ref[coarse, :]
                ks_ref[c, :half] = kp_ref[coarse, :]
                ks_ref[c, half:] = k_ref[coarse, :]
                vs_ref[c, :half] = vp_ref[coarse, :]
                vs_ref[c, half:] = v_ref[coarse, :]

            def staged_trip(t, carry, inner=inner, per_trip=per_trip, qs_ref=qs_ref, ks_ref=ks_ref,
                            vs_ref=vs_ref, os_ref=os_ref, ls_ref=ls_ref):
                bias = bias_ref[seq_start]
                spots = [(c, t * per_trip + f) for f in range(per_trip) for c in range(DSA_STAGE)]
                fine_q = lambda fine: pl.ds(fine, Q_BLOCK, stride=inner)
                fine_k = lambda fine: pl.ds(fine, 2 * Q_BLOCK, stride=inner)
                outs = attend([qs_ref.at[c][fine_q(fine), :] for c, fine in spots],
                              [ks_ref.at[c][fine_k(fine), :] for c, fine in spots],
                              [vs_ref.at[c][fine_k(fine), :] for c, fine in spots],
                              [bias] * len(spots))
                for (c, fine), (o, lse) in zip(spots, outs):
                    os_ref.at[c][fine_q(fine), :] = o
                    ls_ref.at[c][fine_q(fine), :] = lse
                return carry

            lax.fori_loop(0, n_trips, staged_trip, 0)
            for c in range(DSA_STAGE):
                coarse = pl.ds(c, half, stride=DSA_STAGE)
                og_ref[coarse, :] = os_ref[c]
                lg_ref[coarse, :] = ls_ref[c]

    lse = [out_scratch[2 * g + 1][...] for g in range(n_g)]
    top = functools.reduce(jnp.maximum, lse)
    e = [jnp.exp(l - top) for l in lse]
    num = sum(e[g] * out_scratch[2 * g][...] for g in range(n_g))
    o_ref[...] = (num / sum(e)).astype(o_ref.dtype)


def _dilated(q, k, v):
    s = q.shape[0]
    n_pairs = DSA_OUT_W // LANES
    in_specs, stage_scratch = [], []
    for g, (_, r) in enumerate(DSA_GROUPS):
        prev = Q_BLOCK * r
        per_unit = DSA_UNIT // prev
        cur = pl.BlockSpec((DSA_UNIT, LANES), lambda i, p, g=g: (i, n_pairs * g + p))
        prv = pl.BlockSpec((prev, LANES),
                           lambda i, p, g=g, n=per_unit: (jnp.maximum(i * n - 1, 0), n_pairs * g + p))
        in_specs += [cur, cur, prv, cur, prv]
        if r > DSA_STAGE:
            rows = DSA_UNIT // DSA_STAGE
            stage_scratch += [pltpu.VMEM((DSA_STAGE, rows, LANES), F32),
                              pltpu.VMEM((DSA_STAGE, 2 * rows, LANES), F32),
                              pltpu.VMEM((DSA_STAGE, 2 * rows, LANES), F32),
                              pltpu.VMEM((DSA_STAGE, rows, LANES), F32),
                              pltpu.VMEM((DSA_STAGE, rows, LANES), F32)]
    scratch = ([pltpu.VMEM((2, 2 * Q_BLOCK, 2 * Q_BLOCK), F32)]
               + [pltpu.VMEM((DSA_UNIT, LANES), F32)] * (2 * len(DSA_GROUPS)) + stage_scratch)
    args = []
    for _ in DSA_GROUPS:
        args += [q, k, k, v, v]
    return pl.pallas_call(
        _dsa_kernel,
        out_shape=jax.ShapeDtypeStruct((s, DSA_OUT_W), BF16),
        grid=(s // DSA_UNIT, n_pairs),
        in_specs=in_specs,
        out_specs=pl.BlockSpec((DSA_UNIT, LANES), lambda i, p: (i, p)),
        scratch_shapes=scratch,
        compiler_params=_params(2),
        name="dilated",
    )(*args)


def _tail_kernel(qkv_ref, qkv_hbm,
                 x_ref, gm_ref, wg_ref, bg_ref, yb_ref, yc_ref, wa_ref, wb_ref, wc_ref, wo_ref,
                 gf_ref, w1_ref, w3_ref, w2_ref, o_ref,
                 q2_ref, carry_ref, acc_ref, ya_ref, kvprev_ref, kblk_ref, vblk_ref, sem):
    n_slots = SB_QBLOCKS_PER_STEP
    step = pl.program_id(0)
    n_tiles = pl.num_programs(0) - 1
    first_qb = step * n_slots
    n_pairs = SB_W // LANES
    lane = lax.broadcasted_iota(jnp.int32, (2 * Q_BLOCK, LANES), 1)
    row = lax.broadcasted_iota(jnp.int32, (2 * Q_BLOCK, LANES), 0)
    own_lanes = (lane < HEAD_DIM) == (row < Q_BLOCK)
    before = lane < (row % Q_BLOCK)
    j = lax.broadcasted_iota(jnp.int32, (2 * Q_BLOCK, 2 * Q_BLOCK), 0) % Q_BLOCK
    s = lax.broadcasted_iota(jnp.int32, (2 * Q_BLOCK, 2 * Q_BLOCK), 1)
    suffix = jnp.where((s >= Q_BLOCK) | (j >= s), 1.0, 0.0).astype(BF16)

    def stack(x):
        return jnp.where(own_lanes, jnp.concatenate([x, x], axis=0), jnp.zeros((), x.dtype))

    pairs = range(n_pairs)
    sls = [slice(p * LANES, (p + 1) * LANES) for p in pairs]
    for slot in range(n_slots):
        for p in pairs:
            q2_ref[slot, p] = stack(qkv_ref[slot * Q_BLOCK:(slot + 1) * Q_BLOCK, sls[p]])

    def kv_block(handle, p):
        if handle is None:
            return kblk_ref[:, sls[p]], vblk_ref[:, sls[p]]
        rows = slice((handle % n_slots) * Q_BLOCK, (handle % n_slots + 1) * Q_BLOCK)
        k_cols = pl.ds(p * LANES, LANES)
        v_cols = pl.ds(SB_W + p * LANES, LANES)
        if handle < n_slots:
            return kvprev_ref[rows, k_cols], kvprev_ref[rows, v_cols]
        return qkv_ref[rows, pl.ds(SB_W + p * LANES, LANES)], qkv_ref[rows, pl.ds(2 * SB_W + p * LANES, LANES)]

    def walk(jobs, from_diagonal, last_rows=Q_BLOCK, between=(None, None, None)):
        partial = last_rows < Q_BLOCK
        assert not partial or (from_diagonal and all(len(kbs) > 1 for _, kbs in jobs))
        chains = [(i, b) for i, (_, kbs) in enumerate(jobs) for b in range(len(kbs))]

        def is_partial(i, b):
            return partial and b == len(jobs[i][1]) - 1

        def visiting(x, i, b):
            if is_partial(i, b):
                return jnp.concatenate([x[:last_rows], x[Q_BLOCK:Q_BLOCK + last_rows]], axis=0)
            return x

        z = {}
        for i, (slot, kbs) in enumerate(jobs):
            n_full = len(kbs) - 1 if partial else len(kbs)
            for p in pairs:
                for b in range(0, n_full - 1, 2):
                    keys = jnp.concatenate([kv_block(kbs[b], p)[0], kv_block(kbs[b + 1], p)[0]], axis=0)
                    both = _dot_nt(q2_ref[slot, p], keys)
                    z[i, b, p], z[i, b + 1, p] = both[:, :Q_BLOCK], both[:, Q_BLOCK:]
                for b in list(range(n_full - n_full % 2, n_full)) + list(range(n_full, len(kbs))):
                    z[i, b, p] = _dot_nt(visiting(q2_ref[slot, p], i, b), kv_block(kbs[b], p)[0])
        if between[0] is not None:
            between[0]()
        split = {}
        for i, b in chains:
            for p in pairs:
                if from_diagonal and b == 0:
                    z[i, b, p] = jnp.where(before, z[i, b, p], SB_MASKED_LOGIT)
                nz = -z[i, b, p]
                lf = jnp.minimum(nz, 0.0) - jnp.log(1.0 + jnp.exp(jnp.minimum(z[i, b, p], nz)))
                split[i, b, p] = jnp.concatenate(_split_bf16(lf), axis=1)
        sums = {(i, b, p): _dot(split[i, b, p], suffix) for i, b in chains for p in pairs}
        if between[1] is not None:
            between[1]()
        w2, w_last, tops = {}, {}, []
        for i, (slot, kbs) in enumerate(jobs):
            top = None
            for p in pairs:
                carry = None if from_diagonal else carry_ref[slot, p]
                ws = []
                for b in range(len(kbs)):
                    logw = z[i, b, p] + sums[i, b, p][:, :Q_BLOCK]
                    total = sums[i, b, p][:, Q_BLOCK:]
                    if carry is not None:
                        logw = logw + visiting(carry, i, b)
                        total = total + visiting(carry, i, b)
                    w = jnp.exp(logw).astype(BF16)
                    half = w.shape[0] // 2
                    if is_partial(i, b):
                        w_last[i, p] = jnp.concatenate([w[:half], w[half:]], axis=1)
                        carry = jnp.concatenate([total[:half], carry[last_rows:Q_BLOCK],
                                                 total[half:], carry[Q_BLOCK + last_rows:]], axis=0)
                    else:
                        ws += [w[:half], w[half:]]
                        carry = total
                w2[i, p] = jnp.concatenate(ws, axis=1)
                if not partial:
                    carry_ref[slot, p] = carry
                top = carry if top is None else jnp.maximum(top, carry)
            tops.append(top)
        for i, (slot, kbs) in enumerate(jobs):
            n_full = len(kbs) - 1 if partial else len(kbs)
            for p in pairs:
                v2 = jnp.concatenate([stack(kv_block(kbs[b], p)[1]) for b in range(n_full)], axis=0)
                pv = _dot(w2[i, p], v2)
                if partial:
                    extra = _dot(w_last[i, p], stack(kv_block(kbs[n_full], p)[1]))
                    pv = jnp.concatenate([pv[:last_rows] + extra, pv[last_rows:]], axis=0)
                acc_ref[slot, p] = pv if from_diagonal else acc_ref[slot, p] + pv
        if between[2] is not None:
            between[2]()
        return tops

    dense = {}

    def merge_stage():
        x = x_ref[...]
        d = x.shape[1]
        h = _rms(x, gm_ref[...]).astype(BF16)
        ya = ya_ref[(step + 1) % 2]
        merged = None
        for i, (y, w_ref) in enumerate(((ya, wa_ref), (yb_ref[...], wb_ref), (yc_ref[...], wc_ref))):
            logits = _dot(h, wg_ref[:, i * d:(i + 1) * d]) + bg_ref[:, i * d:(i + 1) * d]
            term = _dot(y, w_ref[...]) / (1.0 + jnp.exp(-logits))
            merged = term if merged is None else merged + term
        dense["x"] = x + _dot(merged.astype(BF16), wo_ref[...])
        dense["h"] = _rms(dense["x"], gf_ref[...]).astype(BF16)

    def up_stage():
        a = _dot(dense["h"], w1_ref[...])
        b = _dot(dense["h"], w3_ref[...])
        dense["act"] = (a * b / (1.0 + jnp.exp(-a))).astype(BF16)

    def down_stage():
        o_ref[...] = dense["x"] + 0.5 * _dot(dense["act"], w2_ref[...])

    def fetch(kb):
        rows = pl.ds(pl.multiple_of(kb * Q_BLOCK, Q_BLOCK), Q_BLOCK)
        copies = [pltpu.make_async_copy(qkv_hbm.at[rows, pl.ds(SB_W, SB_W)], kblk_ref, sem.at[0]),
                  pltpu.make_async_copy(qkv_hbm.at[rows, pl.ds(2 * SB_W, SB_W)], vblk_ref, sem.at[1])]
        for c in copies:
            c.start()
        for c in copies:
            c.wait()

    def cond(state):
        kb, top = state
        return jnp.logical_and(kb >= 0, top > SB_LOG_CUTOFF)

    def general():
        for slot in range(n_slots):
            def visit(kb, from_diagonal, slot=slot):
                fetch(kb)
                return jnp.max(walk([(slot, [None])], from_diagonal)[0])

            def body(state, visit=visit):
                kb, _ = state
                return kb - 1, visit(kb, False)

            qb = first_qb + slot
            lax.while_loop(cond, body, (qb - 1, visit(qb, True)))

    n_fused = SB_FUSED_BLOCKS
    assert n_fused - 1 <= n_slots

    def usual():
        jobs = [(slot, [n_slots + slot - b for b in range(n_fused)]) for slot in range(n_slots)]
        tops = walk(jobs, True, SB_LAST_BLOCK_ROWS, (merge_stage, up_stage, down_stage))
        return jnp.max(functools.reduce(jnp.maximum, tops))

    def edge():
        @pl.when(step == n_tiles)
        def _():
            merge_stage()
            up_stage()
            down_stage()
        return jnp.float32(jnp.inf)

    fused_top = lax.cond(jnp.logical_and(step >= 1, step < n_tiles), usual, edge)
    lax.cond(jnp.logical_and(fused_top > SB_LOG_CUTOFF, step < n_tiles), general, lambda: None)
    for slot in range(n_slots):
        for p in pairs:
            ya_ref[step % 2, slot * Q_BLOCK:(slot + 1) * Q_BLOCK, sls[p]] = acc_ref[slot, p].astype(BF16)
    kvprev_ref[...] = qkv_ref[:, SB_W:]


def _tail(qkv, x, gm, wg, bg, yb, yc, wa, wb, wc, wo, gf, w1, w3, w2):
    s, w = qkv.shape[0], qkv.shape[1] // 3
    d = x.shape[1]
    n_pairs = w // LANES
    n_slots = SB_QBLOCKS_PER_STEP
    rows = n_slots * Q_BLOCK
    n_tiles = s // rows
    this = lambda i: (jnp.minimum(i, n_tiles - 1), 0)
    last = lambda i: (jnp.maximum(i - 1, 0), 0)
    hbm = pl.BlockSpec(memory_space=pl.ANY)
    return pl.pallas_call(
        _tail_kernel,
        out_shape=jax.ShapeDtypeStruct((s, d), F32),
        grid=(n_tiles + 1,),
        in_specs=[pl.BlockSpec((rows, 3 * w), this), hbm,
                  pl.BlockSpec((rows, d), last), _resident((1, d)), _resident(wg.shape), _resident(bg.shape),
                  pl.BlockSpec((rows, DSA_OUT_W), last), pl.BlockSpec((rows, MEM_W), last),
                  _resident(wa.shape), _resident(wb.shape), _resident(wc.shape), _resident(wo.shape),
                  _resident((1, d)), _resident(w1.shape), _resident(w3.shape), _resident(w2.shape)],
        out_specs=pl.BlockSpec((rows, d), last),
        scratch_shapes=[pltpu.VMEM((n_slots, n_pairs, 2 * Q_BLOCK, LANES), BF16),
                        pltpu.VMEM((n_slots, n_pairs, 2 * Q_BLOCK, LANES), F32),
                        pltpu.VMEM((n_slots, n_pairs, Q_BLOCK, LANES), F32),
                        pltpu.VMEM((2, rows, w), BF16),
                        pltpu.VMEM((rows, 2 * w), BF16),
                        pltpu.VMEM((Q_BLOCK, w), BF16), pltpu.VMEM((Q_BLOCK, w), BF16),
                        pltpu.SemaphoreType.DMA((2,))],
        compiler_params=_params(1),
        name="tail",
    )(qkv, qkv, x, gm, wg, bg, yb, yc, wa, wb, wc, wo, gf, w1, w3, w2)


def _layer(x, mem, p):
    vec = lambda v: v.reshape(1, -1)
    n_dsa = DSA_W // HEAD_DIM
    gains = jnp.concatenate([jnp.tile(p["qn_dsa"], n_dsa), jnp.tile(p["kn_dsa"], n_dsa),
                             jnp.tile(p["qn_mem"], MEM_HEADS), jnp.tile(p["kn_mem"], MEM_HEADS)]).reshape(1, -1)
    half = HEAD_DIM // 2
    inv_freq = jnp.power(ROPE_THETA, -jnp.arange(half, dtype=F32) / half).reshape(half, 1)

    later = ("w_in", "w_gate", "w_branch_sb", "w_branch_dsa", "w_branch_mem", "w_out",
             "ffn2_w1", "ffn2_w3", "ffn2_w2")
    x, cos, sin, casted = _ffn(x, vec(p["ffn1_norm"]), p["ffn1_w1"], p["ffn1_w3"], p["ffn1_w2"], inv_freq,
                               cast=[p[name] for name in later])
    w = dict(zip(later, casted))
    km, vm = _memkv(mem, vec(p["mem_norm"]), p["w_mem_kv"], gains)
    qkv, qb, kb, vb, yc = _proj(x, vec(p["mix_norm"]), w["w_in"], cos, sin, gains, km, vm)
    yb = _dilated(qb, kb, vb)
    return _tail(qkv, x, vec(p["mix_norm"]), w["w_gate"], vec(p["b_gate"]), yb, yc,
                 w["w_branch_sb"], w["w_branch_dsa"], w["w_branch_mem"], w["w_out"],
                 vec(p["ffn2_norm"]), w["ffn2_w1"], w["ffn2_w3"], w["ffn2_w2"])


_PARAM_NAMES = ("ffn1_norm", "ffn1_w1", "ffn1_w3", "ffn1_w2", "mix_norm", "mem_norm", "w_in", "w_mem_kv",
                "qn_dsa", "kn_dsa", "qn_mem", "kn_mem", "w_branch_sb", "w_branch_dsa", "w_branch_mem",
                "w_gate", "b_gate", "w_out", "ffn2_norm", "ffn2_w1", "ffn2_w3", "ffn2_w2")


def kernel(x, mem, ffn1_norm, ffn1_w1, ffn1_w3, ffn1_w2, mix_norm, mem_norm, w_in, w_mem_kv, qn_dsa, kn_dsa, qn_mem, kn_mem, w_branch_sb, w_branch_dsa, w_branch_mem, w_gate, b_gate, w_out, ffn2_norm, ffn2_w1, ffn2_w3, ffn2_w2):
    stacked = dict(zip(_PARAM_NAMES, (ffn1_norm, ffn1_w1, ffn1_w3, ffn1_w2, mix_norm, mem_norm, w_in,
                                      w_mem_kv, qn_dsa, kn_dsa, qn_mem, kn_mem, w_branch_sb, w_branch_dsa,
                                      w_branch_mem, w_gate, b_gate, w_out, ffn2_norm, ffn2_w1, ffn2_w3,
                                      ffn2_w2)))
    depth = ffn1_norm.shape[0]
    outs = []
    for b in range(x.shape[0]):
        xb = x[b]
        for l in range(depth):
            xb = _layer(xb, mem[b], {k: v[l] for k, v in stacked.items()})
        outs.append(xb)
    return jnp.stack(outs)
```

```python
import functools

import jax
import jax.numpy as jnp
from jax import lax
from jax.experimental import pallas as pl
from jax.experimental.pallas import tpu as pltpu

F32 = jnp.float32
BF16 = jnp.bfloat16

HEAD_DIM = 64
SB_HEADS = 8
DSA_GROUPS = ((128, 1), (512, 4), (2048, 16))
DSA_HEADS_PER_GROUP = 4
MEM_HEADS = 4
ROPE_THETA = 10000.0
NORM_EPS = 1e-6
Q_BLOCK = 128
SB_W = SB_HEADS * HEAD_DIM
DSA_W = DSA_HEADS_PER_GROUP * len(DSA_GROUPS) * HEAD_DIM
DSA_OUT_W = DSA_HEADS_PER_GROUP * HEAD_DIM
MEM_W = MEM_HEADS * HEAD_DIM
QK_SCALE = HEAD_DIM ** -0.5
GAIN_QN_DSA, GAIN_KN_DSA, GAIN_QN_MEM, GAIN_KN_MEM = 0, DSA_W, 2 * DSA_W, 2 * DSA_W + MEM_W

LANES = 128
MXU_WIDTH = 256
DSA_UNIT = Q_BLOCK * max(r for _, r in DSA_GROUPS)
DSA_UNROLL = 16
DSA_STAGE = 4
ROW_TILE = 512
VMEM_LIMIT = 56 * 1024 * 1024
FFN_CHUNK = 256
CAST_SLABS = 16
SB_LOG_CUTOFF = -104.0
SB_MASKED_LOGIT = -1e30
SB_FUSED_BLOCKS = 3
SB_QBLOCKS_PER_STEP = 2
SB_LAST_BLOCK_ROWS = 48


def _resident(shape):
    zeros = (0,) * len(shape)
    return pl.BlockSpec(shape, lambda *_: zeros, pipeline_mode=pl.Buffered(1))


def _params(n_axes):
    return pltpu.CompilerParams(dimension_semantics=("arbitrary",) * n_axes,
                                vmem_limit_bytes=VMEM_LIMIT)


def _rms(x, g):
    return x * lax.rsqrt(jnp.mean(x * x, axis=-1, keepdims=True) + NORM_EPS) * g


def _dot(a, b):
    return jnp.dot(a, b, preferred_element_type=F32)


def _dot_nt(a, b):
    return lax.dot_general(a, b, (((1,), (1,)), ((), ())), preferred_element_type=F32)


def _split_bf16(x):
    hi = x.astype(BF16)
    lo = (x - hi.astype(F32)).astype(BF16)
    return hi, lo


def _head_norm(x, g):
    n = x.shape[-1]
    w = min(n, MXU_WIDTH)
    r = lax.broadcasted_iota(jnp.int32, (w, w), 0) // HEAD_DIM
    c = lax.broadcasted_iota(jnp.int32, (w, w), 1) // HEAD_DIM
    bd = jnp.where(r == c, 1.0, 0.0).astype(BF16)
    hi, lo = _split_bf16(x * x)
    ms = jnp.concatenate([_dot(hi[:, j:j + w], bd) + _dot(lo[:, j:j + w], bd) for j in range(0, n, w)],
                         axis=1) * (1.0 / HEAD_DIM)
    return x * lax.rsqrt(ms + NORM_EPS) * g


def _ffn_kernel(x_ref, g_ref, freq_ref, w1_hbm, w3_hbm, w2_hbm, *refs):
    n_cast = (len(refs) - 10) // 2
    o_ref, cos_ref, sin_ref = refs[n_cast:n_cast + 3]
    w1_ref, w3_ref, w2_ref, s1_ref, s3_ref, s2_ref, sem = refs[2 * n_cast + 3:]
    f = w1_ref.shape[1]
    x = x_ref[...]
    h = _rms(x, g_ref[...]).astype(BF16)

    def side_jobs():
        rows = x_ref.shape[0]
        pos = pl.program_id(0) * rows + lax.broadcasted_iota(jnp.int32, (freq_ref.shape[0], rows), 1)
        ang = freq_ref[...] * pos.astype(F32)
        cos_ref[...] = jnp.cos(ang)
        sin_ref[...] = jnp.sin(ang)
        for src, dst in zip(refs[:n_cast], refs[n_cast + 3:2 * n_cast + 3]):
            dst[...] = src[...].astype(BF16)

    def swiglu(w1, w3, w2):
        a = _dot(h, w1)
        b = _dot(h, w3)
        return _dot((a * b / (1.0 + jnp.exp(-a))).astype(BF16), w2)

    def slice_copies(c, slot):
        cols = pl.ds(c * FFN_CHUNK, FFN_CHUNK)
        return [pltpu.make_async_copy(w1_hbm.at[:, cols], s1_ref.at[slot], sem.at[slot, 0]),
                pltpu.make_async_copy(w3_hbm.at[:, cols], s3_ref.at[slot], sem.at[slot, 1]),
                pltpu.make_async_copy(w2_hbm.at[cols, :], s2_ref.at[slot], sem.at[slot, 2])]

    @pl.when(pl.program_id(0) == 0)
    def _():
        n_chunks = f // FFN_CHUNK
        for copy in slice_copies(0, 0):
            copy.start()
        total = None
        for c in range(n_chunks):
            slot = c % 2
            if c + 1 < n_chunks:
                for copy in slice_copies(c + 1, 1 - slot):
                    copy.start()
            for copy in slice_copies(c, slot):
                copy.wait()
            cols = slice(c * FFN_CHUNK, (c + 1) * FFN_CHUNK)
            w1_ref[:, cols] = s1_ref[slot].astype(BF16)
            w3_ref[:, cols] = s3_ref[slot].astype(BF16)
            w2_ref[cols, :] = s2_ref[slot].astype(BF16)
            part = swiglu(w1_ref[:, cols], w3_ref[:, cols], w2_ref[cols, :])
            total = part if total is None else total + part
        o_ref[...] = x + 0.5 * total
        side_jobs()

    @pl.when(pl.program_id(0) > 0)
    def _():
        o_ref[...] = x + 0.5 * swiglu(w1_ref[...], w3_ref[...], w2_ref[...])
        side_jobs()


def _ffn(x, g, w1, w3, w2, inv_freq, cast=()):
    s, d = x.shape
    f = w1.shape[1]
    half = inv_freq.shape[0]
    table = pl.BlockSpec((half, ROW_TILE), lambda i: (0, i))
    steps = s // ROW_TILE
    row = lambda i: (i, 0)
    n_slabs = min(CAST_SLABS, steps)
    per_slab = steps // n_slabs
    slab = lambda i: (i // per_slab, 0)
    slabs = [pl.BlockSpec((w.shape[0] // n_slabs, w.shape[1]), slab) for w in cast]
    hbm = pl.BlockSpec(memory_space=pl.ANY)
    out = pl.pallas_call(
        _ffn_kernel,
        out_shape=(jax.ShapeDtypeStruct((s, d), F32), jax.ShapeDtypeStruct((half, s), F32),
                   jax.ShapeDtypeStruct((half, s), F32), *[jax.ShapeDtypeStruct(w.shape, BF16) for w in cast]),
        grid=(steps,),
        in_specs=[pl.BlockSpec((ROW_TILE, d), row), _resident((1, d)), _resident((half, 1)), hbm, hbm, hbm, *slabs],
        out_specs=(pl.BlockSpec((ROW_TILE, d), row), table, table, *slabs),
        scratch_shapes=[pltpu.VMEM((d, f), BF16), pltpu.VMEM((d, f), BF16), pltpu.VMEM((f, d), BF16),
                        pltpu.VMEM((2, d, FFN_CHUNK), F32), pltpu.VMEM((2, d, FFN_CHUNK), F32),
                        pltpu.VMEM((2, FFN_CHUNK, d), F32), pltpu.SemaphoreType.DMA((2, 3))],
        compiler_params=_params(1),
        name="ffn",
    )(x, g, inv_freq, w1, w3, w2, *cast)
    return out[0], out[1], out[2], list(out[3:])


def _memkv_kernel(mem_ref, g_ref, w_ref, gains_ref, k_ref, v_ref):
    h = _rms(mem_ref[...], g_ref[...]).astype(BF16)
    kv = _dot(h, w_ref[...].astype(BF16))
    k_ref[...] = _head_norm(kv[:, :MEM_W], gains_ref[:, GAIN_KN_MEM:GAIN_KN_MEM + MEM_W]).astype(BF16)
    v_ref[...] = kv[:, MEM_W:].astype(BF16)


def _memkv(mem, g, w, gains):
    m = mem.shape[0]
    out = jax.ShapeDtypeStruct((m, MEM_W), BF16)
    return pl.pallas_call(_memkv_kernel, out_shape=(out, out), name="memkv",
                          compiler_params=pltpu.CompilerParams(vmem_limit_bytes=VMEM_LIMIT),
                          )(mem, g, w, gains)


def _rope(x, cos, sin_signed):
    lane = lax.broadcasted_iota(jnp.int32, (x.shape[0], LANES), 1)
    first_half = (lane % HEAD_DIM) < (HEAD_DIM // 2)
    out = []
    for j in range(x.shape[1] // LANES):
        xs = x[:, j * LANES:(j + 1) * LANES]
        partner = jnp.where(first_half, pltpu.roll(xs, LANES - HEAD_DIM // 2, 1),
                            pltpu.roll(xs, HEAD_DIM // 2, 1))
        out.append(xs * cos + partner * sin_signed)
    return jnp.concatenate(out, axis=1)


def _rope_lanes(cos_half, sin_half):
    half = HEAD_DIM // 2
    f = lax.broadcasted_iota(jnp.int32, (half, LANES), 0)
    lane = lax.broadcasted_iota(jnp.int32, (half, LANES), 1)
    hit = (lane % half) == f
    spread = jnp.where(hit, 1.0, 0.0).astype(BF16)
    signed = jnp.where(hit, jnp.where((lane % HEAD_DIM) < half, -1.0, 1.0), 0.0).astype(BF16)
    c_hi, c_lo = _split_bf16(cos_half)
    s_hi, s_lo = _split_bf16(sin_half)
    return _dot(c_hi, spread) + _dot(c_lo, spread), _dot(s_hi, signed) + _dot(s_lo, signed)


def _proj_kernel(x_ref, g_ref, w_ref, cos_ref, sin_ref, gains_ref, km_ref, vm_ref,
                 qkv_ref, qb_ref, kb_ref, vb_ref, yc_ref):
    qn_d = gains_ref[:, GAIN_QN_DSA:GAIN_QN_DSA + DSA_W]
    kn_d = gains_ref[:, GAIN_KN_DSA:GAIN_KN_DSA + DSA_W]
    qn_m = gains_ref[:, GAIN_QN_MEM:GAIN_QN_MEM + MEM_W]
    h = _rms(x_ref[...], g_ref[...]).astype(BF16)
    cos, sin_signed = _rope_lanes(cos_ref[...].T, sin_ref[...].T)

    def cols(lo, width):
        return _dot(h, w_ref[:, lo:lo + width])

    base = 3 * SB_W
    qc_raw = cols(base + 3 * DSA_W, MEM_W)
    qb_raw = cols(base, DSA_W)
    qc = _head_norm(qc_raw, qn_m) * QK_SCALE
    kb_raw = cols(base + DSA_W, DSA_W)

    lane = lax.broadcasted_iota(jnp.int32, (qc.shape[0], LANES), 1)
    groups = [slice(j * LANES, (j + 1) * LANES) for j in range(MEM_W // LANES)]
    scores = [[_dot_nt(jnp.where((lane < HEAD_DIM) == (half == 0), qc[:, sl], 0.0).astype(BF16), km_ref[:, sl])
               for half in range(2)] for sl in groups]
    def store_groups(ref, val):
        for j in range(DSA_W // LANES):
            ref[j] = val[:, j * LANES:(j + 1) * LANES]

    qb = _head_norm(qb_raw, qn_d)
    store_groups(vb_ref, cols(base + 2 * DSA_W, DSA_W))
    for sl, (sc0, sc1) in zip(groups, scores):
        outs = []
        for sc in (sc0, sc1):
            p = jnp.exp(sc - jnp.max(sc, axis=-1, keepdims=True))
            outs.append(_dot(p.astype(BF16), vm_ref[:, sl]) / jnp.sum(p, axis=-1, keepdims=True))
        yc_ref[:, sl] = jnp.where(lane < HEAD_DIM, outs[0], outs[1]).astype(BF16)
    kb = _head_norm(kb_raw, kn_d)
    store_groups(qb_ref, _rope(qb, cos, sin_signed) * QK_SCALE)
    store_groups(kb_ref, _rope(kb, cos, sin_signed))
    qkv_ref[:, :SB_W] = (cols(0, SB_W) * QK_SCALE).astype(BF16)
    qkv_ref[:, SB_W:] = cols(SB_W, 2 * SB_W).astype(BF16)


def _proj(x, g, w_in, cos, sin_signed, gains, km, vm):
    s, d = x.shape
    row = lambda i: (i, 0)
    tile = lambda w: pl.BlockSpec((ROW_TILE, w), row)
    dsa = jax.ShapeDtypeStruct((DSA_W // LANES, s, LANES), F32)
    dsa_tile = pl.BlockSpec((DSA_W // LANES, ROW_TILE, LANES), lambda i: (0, i, 0))
    table = pl.BlockSpec((HEAD_DIM // 2, ROW_TILE), lambda i: (0, i))
    return pl.pallas_call(
        _proj_kernel,
        out_shape=(jax.ShapeDtypeStruct((s, 3 * SB_W), BF16), dsa, dsa, dsa,
                   jax.ShapeDtypeStruct((s, MEM_W), BF16)),
        grid=(s // ROW_TILE,),
        in_specs=[tile(d), _resident((1, d)), _resident(w_in.shape), table, table,
                  _resident(gains.shape), _resident(km.shape), _resident(vm.shape)],
        out_specs=(tile(3 * SB_W), dsa_tile, dsa_tile, dsa_tile, tile(MEM_W)),
        compiler_params=_params(1),
        name="proj",
    )(x, g, w_in, cos, sin_signed, gains, km, vm)


def _dsa_kernel(*refs):
    n_g = len(DSA_GROUPS)
    ins = [refs[5 * g:5 * g + 5] for g in range(n_g)]
    o_ref = refs[5 * n_g]
    bias_ref = refs[5 * n_g + 1]
    out_scratch = refs[5 * n_g + 2:5 * n_g + 2 + 2 * n_g]
    stage_scratch = refs[5 * n_g + 2 + 2 * n_g:]
    step = pl.program_id(0)
    lane = lax.broadcasted_iota(jnp.int32, (Q_BLOCK, LANES), 1)
    head0 = lane < HEAD_DIM

    def own_lanes(rows):
        ln = lax.broadcasted_iota(jnp.int32, (rows, LANES), 1)
        rw = lax.broadcasted_iota(jnp.int32, (rows, LANES), 0)
        return (ln < HEAD_DIM) == (rw < rows // 2)

    own_q, own_v = own_lanes(2 * Q_BLOCK), own_lanes(4 * Q_BLOCK)
    qi = lax.broadcasted_iota(jnp.int32, (2 * Q_BLOCK, 2 * Q_BLOCK), 0) % Q_BLOCK
    kj = lax.broadcasted_iota(jnp.int32, (2 * Q_BLOCK, 2 * Q_BLOCK), 1)
    dist = Q_BLOCK + qi - kj
    in_band = (dist >= 0) & (dist <= Q_BLOCK)
    bias_ref[0] = jnp.where(in_band, 0.0, -jnp.inf)
    bias_ref[1] = jnp.where(in_band & (kj >= Q_BLOCK), 0.0, -jnp.inf)
    seq_start = jnp.where(step == 0, 1, 0)

    def attend(qs, ks, vs, biases):
        units = range(len(qs))
        sc, v2 = [], []
        for j in units:
            q = qs[j].astype(BF16)
            q2 = jnp.where(own_q, jnp.concatenate([q, q], axis=0), jnp.zeros((), BF16))
            v = vs[j].astype(BF16)
            v2.append(jnp.where(own_v, jnp.concatenate([v, v], axis=0), jnp.zeros((), BF16)))
            sc.append(_dot_nt(q2, ks[j].astype(BF16)))
        p2, m, den = [], [], []
        for j in units:
            s_j = sc[j] + biases[j]
            m.append(jnp.max(s_j, axis=-1, keepdims=True))
            p = jnp.exp(s_j - m[j])
            den.append(jnp.sum(p, axis=-1, keepdims=True))
            p = p.astype(BF16)
            p2.append(jnp.concatenate([p[:Q_BLOCK], p[Q_BLOCK:]], axis=1))
        pv = [_dot(p2[j], v2[j]) for j in units]
        outs = []
        for j in units:
            inv = 1.0 / den[j]
            lse = m[j] + jnp.log(den[j])
            outs.append((pv[j] * jnp.where(head0, inv[:Q_BLOCK], inv[Q_BLOCK:]),
                         jnp.where(head0, lse[:Q_BLOCK], lse[Q_BLOCK:])))
        return outs

    n_trips = DSA_UNIT // Q_BLOCK // DSA_UNROLL
    staged = 0
    for g, (window, r) in enumerate(DSA_GROUPS):
        assert window // r == Q_BLOCK
        q_ref, k_ref, kp_ref, v_ref, vp_ref = ins[g]
        og_ref, lg_ref = out_scratch[2 * g:2 * g + 2]
        prev = Q_BLOCK * r

        if r <= DSA_STAGE:
            assert DSA_UNROLL % r == 0

            def trip(t, first, r=r, prev=prev, q_ref=q_ref, k_ref=k_ref, kp_ref=kp_ref, v_ref=v_ref,
                     vp_ref=vp_ref, og_ref=og_ref, lg_ref=lg_ref):
                qs, ks, vs, biases, rows = [], [], [], [], []
                for j in range(DSA_UNROLL):
                    sub, c = t * (DSA_UNROLL // r) + j // r, j % r
                    rows.append(pl.ds(sub * prev + c, Q_BLOCK, stride=r))
                    qs.append(q_ref[rows[j], :])
                    if first and sub == 0:
                        half = pl.ds(c, Q_BLOCK, stride=r)
                        ks.append(jnp.concatenate([kp_ref[half, :], k_ref[half, :]], axis=0))
                        vs.append(jnp.concatenate([vp_ref[half, :], v_ref[half, :]], axis=0))
                        biases.append(bias_ref[seq_start])
                    else:
                        both = pl.ds((sub - 1) * prev + c, 2 * Q_BLOCK, stride=r)
                        ks.append(k_ref[both, :])
                        vs.append(v_ref[both, :])
                        biases.append(bias_ref[0])
                for j, (o, lse) in enumerate(attend(qs, ks, vs, biases)):
                    og_ref[rows[j], :] = o
                    lg_ref[rows[j], :] = lse

            trip(0, True)

            def later_trip(t, carry, trip=trip):
                trip(t, False)
                return carry

            lax.fori_loop(1, n_trips, later_trip, 0)
        else:
            inner = r // DSA_STAGE
            per_trip = DSA_UNROLL // DSA_STAGE
            assert inner <= DSA_STAGE and prev == DSA_UNIT and DSA_UNROLL % DSA_STAGE == 0 and inner % per_trip == 0
            qs_ref, ks_ref, vs_ref, os_ref, ls_ref = stage_scratch[5 * staged:5 * staged + 5]
            staged += 1
            half = Q_BLOCK * inner
            for c in range(DSA_STAGE):
                coarse = pl.ds(c, half, stride=DSA_STAGE)
                qs_ref[c] = q_ref[coarse, :]
                ks_ref[c, :half] = kp_ref[coarse, :]
                ks_ref[c, half:] = k_ref[coarse, :]
                vs_ref[c, :half] = vp_ref[coarse, :]
                vs_ref[c, half:] = v_ref[coarse, :]

            def staged_trip(t, carry, inner=inner, per_trip=per_trip, qs_ref=qs_ref, ks_ref=ks_ref,
                            vs_ref=vs_ref, os_ref=os_ref, ls_ref=ls_ref):
                bias = bias_ref[seq_start]
                spots = [(c, t * per_trip + f) for f in range(per_trip) for c in range(DSA_STAGE)]
                fine_q = lambda fine: pl.ds(fine, Q_BLOCK, stride=inner)
                fine_k = lambda fine: pl.ds(fine, 2 * Q_BLOCK, stride=inner)
                outs = attend([qs_ref.at[c][fine_q(fine), :] for c, fine in spots],
                              [ks_ref.at[c][fine_k(fine), :] for c, fine in spots],
                              [vs_ref.at[c][fine_k(fine), :] for c, fine in spots],
                              [bias] * len(spots))
                for (c, fine), (o, lse) in zip(spots, outs):
                    os_ref.at[c][fine_q(fine), :] = o
                    ls_ref.at[c][fine_q(fine), :] = lse
                return carry

            lax.fori_loop(0, n_trips, staged_trip, 0)
            for c in range(DSA_STAGE):
                coarse = pl.ds(c, half, stride=DSA_STAGE)
                og_ref[coarse, :] = os_ref[c]
                lg_ref[coarse, :] = ls_ref[c]

    lse = [out_scratch[2 * g + 1][...] for g in range(n_g)]
    top = functools.reduce(jnp.maximum, lse)
    e = [jnp.exp(l - top) for l in lse]
    num = sum(e[g] * out_scratch[2 * g][...] for g in range(n_g))
    o_ref[...] = (num / sum(e)).astype(o_ref.dtype)


def _dilated(q, k, v):
    s = q.shape[1]
    n_pairs = DSA_OUT_W // LANES
    in_specs, stage_scratch = [], []
    for g, (_, r) in enumerate(DSA_GROUPS):
        prev = Q_BLOCK * r
        per_unit = DSA_UNIT // prev
        cur = pl.BlockSpec((None, DSA_UNIT, LANES), lambda i, p, g=g: (n_pairs * g + p, i, 0))
        prv = pl.BlockSpec((None, prev, LANES),
                           lambda i, p, g=g, n=per_unit: (n_pairs * g + p, jnp.maximum(i * n - 1, 0), 0))
        in_specs += [cur, cur, prv, cur, prv]
        if r > DSA_STAGE:
            rows = DSA_UNIT // DSA_STAGE
            stage_scratch += [pltpu.VMEM((DSA_STAGE, rows, LANES), F32),
                              pltpu.VMEM((DSA_STAGE, 2 * rows, LANES), F32),
                              pltpu.VMEM((DSA_STAGE, 2 * rows, LANES), F32),
                              pltpu.VMEM((DSA_STAGE, rows, LANES), F32),
                              pltpu.VMEM((DSA_STAGE, rows, LANES), F32)]
    scratch = ([pltpu.VMEM((2, 2 * Q_BLOCK, 2 * Q_BLOCK), F32)]
               + [pltpu.VMEM((DSA_UNIT, LANES), F32)] * (2 * len(DSA_GROUPS)) + stage_scratch)
    args = []
    for _ in DSA_GROUPS:
        args += [q, k, k, v, v]
    return pl.pallas_call(
        _dsa_kernel,
        out_shape=jax.ShapeDtypeStruct((s, DSA_OUT_W), BF16),
        grid=(s // DSA_UNIT, n_pairs),
        in_specs=in_specs,
        out_specs=pl.BlockSpec((DSA_UNIT, LANES), lambda i, p: (i, p)),
        scratch_shapes=scratch,
        compiler_params=_params(2),
        name="dilated",
    )(*args)


def _tail_kernel(qkv_ref, qkv_hbm,
                 x_ref, gm_ref, wg_ref, bg_ref, yb_ref, yc_ref, wa_ref, wb_ref, wc_ref, wo_ref,
                 gf_ref, w1_ref, w3_ref, w2_ref, o_ref,
                 q2_ref, carry_ref, acc_ref, ya_ref, kvprev_ref, kblk_ref, vblk_ref, sem):
    n_slots = SB_QBLOCKS_PER_STEP
    step = pl.program_id(0)
    n_tiles = pl.num_programs(0) - 1
    first_qb = step * n_slots
    n_pairs = SB_W // LANES
    lane = lax.broadcasted_iota(jnp.int32, (2 * Q_BLOCK, LANES), 1)
    row = lax.broadcasted_iota(jnp.int32, (2 * Q_BLOCK, LANES), 0)
    own_lanes = (lane < HEAD_DIM) == (row < Q_BLOCK)
    before = lane < (row % Q_BLOCK)
    j = lax.broadcasted_iota(jnp.int32, (2 * Q_BLOCK, 2 * Q_BLOCK), 0) % Q_BLOCK
    s = lax.broadcasted_iota(jnp.int32, (2 * Q_BLOCK, 2 * Q_BLOCK), 1)
    suffix = jnp.where((s >= Q_BLOCK) | (j >= s), 1.0, 0.0).astype(BF16)

    def stack(x):
        return jnp.where(own_lanes, jnp.concatenate([x, x], axis=0), jnp.zeros((), x.dtype))

    pairs = range(n_pairs)
    sls = [slice(p * LANES, (p + 1) * LANES) for p in pairs]
    for slot in range(n_slots):
        for p in pairs:
            q2_ref[slot, p] = stack(qkv_ref[slot * Q_BLOCK:(slot + 1) * Q_BLOCK, sls[p]])

    def kv_block(handle, p):
        if handle is None:
            return kblk_ref[:, sls[p]], vblk_ref[:, sls[p]]
        rows = slice((handle % n_slots) * Q_BLOCK, (handle % n_slots + 1) * Q_BLOCK)
        k_cols = pl.ds(p * LANES, LANES)
        v_cols = pl.ds(SB_W + p * LANES, LANES)
        if handle < n_slots:
            return kvprev_ref[rows, k_cols], kvprev_ref[rows, v_cols]
        return qkv_ref[rows, pl.ds(SB_W + p * LANES, LANES)], qkv_ref[rows, pl.ds(2 * SB_W + p * LANES, LANES)]

    def walk(jobs, from_diagonal, last_rows=Q_BLOCK, between=(None, None, None)):
        partial = last_rows < Q_BLOCK
        assert not partial or (from_diagonal and all(len(kbs) > 1 for _, kbs in jobs))
        chains = [(i, b) for i, (_, kbs) in enumerate(jobs) for b in range(len(kbs))]

        def is_partial(i, b):
            return partial and b == len(jobs[i][1]) - 1

        def visiting(x, i, b):
            if is_partial(i, b):
                return jnp.concatenate([x[:last_rows], x[Q_BLOCK:Q_BLOCK + last_rows]], axis=0)
            return x

        z = {}
        for i, (slot, kbs) in enumerate(jobs):
            n_full = len(kbs) - 1 if partial else len(kbs)
            for p in pairs:
                for b in range(0, n_full - 1, 2):
                    keys = jnp.concatenate([kv_block(kbs[b], p)[0], kv_block(kbs[b + 1], p)[0]], axis=0)
                    both = _dot_nt(q2_ref[slot, p], keys)
                    z[i, b, p], z[i, b + 1, p] = both[:, :Q_BLOCK], both[:, Q_BLOCK:]
                for b in list(range(n_full - n_full % 2, n_full)) + list(range(n_full, len(kbs))):
                    z[i, b, p] = _dot_nt(visiting(q2_ref[slot, p], i, b), kv_block(kbs[b], p)[0])
        if between[0] is not None:
            between[0]()
        split = {}
        for i, b in chains:
            for p in pairs:
                if from_diagonal and b == 0:
                    z[i, b, p] = jnp.where(before, z[i, b, p], SB_MASKED_LOGIT)
                nz = -z[i, b, p]
                lf = jnp.minimum(nz, 0.0) - jnp.log(1.0 + jnp.exp(jnp.minimum(z[i, b, p], nz)))
                split[i, b, p] = jnp.concatenate(_split_bf16(lf), axis=1)
        sums = {(i, b, p): _dot(split[i, b, p], suffix) for i, b in chains for p in pairs}
        if between[1] is not None:
            between[1]()
        w2, w_last, tops = {}, {}, []
        for i, (slot, kbs) in enumerate(jobs):
            top = None
            for p in pairs:
                carry = None if from_diagonal else carry_ref[slot, p]
                ws = []
                for b in range(len(kbs)):
                    logw = z[i, b, p] + sums[i, b, p][:, :Q_BLOCK]
                    total = sums[i, b, p][:, Q_BLOCK:]
                    if carry is not None:
                        logw = logw + visiting(carry, i, b)
                        total = total + visiting(carry, i, b)
                    w = jnp.exp(logw).astype(BF16)
                    half = w.shape[0] // 2
                    if is_partial(i, b):
                        w_last[i, p] = jnp.concatenate([w[:half], w[half:]], axis=1)
                        carry = jnp.concatenate([total[:half], carry[last_rows:Q_BLOCK],
                                                 total[half:], carry[Q_BLOCK + last_rows:]], axis=0)
                    else:
                        ws += [w[:half], w[half:]]
                        carry = total
                w2[i, p] = jnp.concatenate(ws, axis=1)
                if not partial:
                    carry_ref[slot, p] = carry
                top = carry if top is None else jnp.maximum(top, carry)
            tops.append(top)
        for i, (slot, kbs) in enumerate(jobs):
            n_full = len(kbs) - 1 if partial else len(kbs)
            for p in pairs:
                v2 = jnp.concatenate([stack(kv_block(kbs[b], p)[1]) for b in range(n_full)], axis=0)
                pv = _dot(w2[i, p], v2)
                if partial:
                    extra = _dot(w_last[i, p], stack(kv_block(kbs[n_full], p)[1]))
                    pv = jnp.concatenate([pv[:last_rows] + extra, pv[last_rows:]], axis=0)
                acc_ref[slot, p] = pv if from_diagonal else acc_ref[slot, p] + pv
        if between[2] is not None:
            between[2]()
        return tops

    dense = {}

    def merge_stage():
        x = x_ref[...]
        d = x.shape[1]
        h = _rms(x, gm_ref[...]).astype(BF16)
        ya = ya_ref[(step + 1) % 2]
        merged = None
        for i, (y, w_ref) in enumerate(((ya, wa_ref), (yb_ref[...], wb_ref), (yc_ref[...], wc_ref))):
            logits = _dot(h, wg_ref[:, i * d:(i + 1) * d]) + bg_ref[:, i * d:(i + 1) * d]
            term = _dot(y, w_ref[...]) / (1.0 + jnp.exp(-logits))
            merged = term if merged is None else merged + term
        dense["x"] = x + _dot(merged.astype(BF16), wo_ref[...])
        dense["h"] = _rms(dense["x"], gf_ref[...]).astype(BF16)

    def up_stage():
        a = _dot(dense["h"], w1_ref[...])
        b = _dot(dense["h"], w3_ref[...])
        dense["act"] = (a * b / (1.0 + jnp.exp(-a))).astype(BF16)

    def down_stage():
        o_ref[...] = dense["x"] + 0.5 * _dot(dense["act"], w2_ref[...])

    def fetch(kb):
        rows = pl.ds(pl.multiple_of(kb * Q_BLOCK, Q_BLOCK), Q_BLOCK)
        copies = [pltpu.make_async_copy(qkv_hbm.at[rows, pl.ds(SB_W, SB_W)], kblk_ref, sem.at[0]),
                  pltpu.make_async_copy(qkv_hbm.at[rows, pl.ds(2 * SB_W, SB_W)], vblk_ref, sem.at[1])]
        for c in copies:
            c.start()
        for c in copies:
            c.wait()

    def cond(state):
        kb, top = state
        return jnp.logical_and(kb >= 0, top > SB_LOG_CUTOFF)

    def general():
        for slot in range(n_slots):
            def visit(kb, from_diagonal, slot=slot):
                fetch(kb)
                return jnp.max(walk([(slot, [None])], from_diagonal)[0])

            def body(state, visit=visit):
                kb, _ = state
                return kb - 1, visit(kb, False)

            qb = first_qb + slot
            lax.while_loop(cond, body, (qb - 1, visit(qb, True)))

    n_fused = SB_FUSED_BLOCKS
    assert n_fused - 1 <= n_slots

    def usual():
        jobs = [(slot, [n_slots + slot - b for b in range(n_fused)]) for slot in range(n_slots)]
        tops = walk(jobs, True, SB_LAST_BLOCK_ROWS, (merge_stage, up_stage, down_stage))
        return jnp.max(functools.reduce(jnp.maximum, tops))

    def edge():
        @pl.when(step == n_tiles)
        def _():
            merge_stage()
            up_stage()
            down_stage()
        return jnp.float32(jnp.inf)

    fused_top = lax.cond(jnp.logical_and(step >= 1, step < n_tiles), usual, edge)
    lax.cond(jnp.logical_and(fused_top > SB_LOG_CUTOFF, step < n_tiles), general, lambda: None)
    for slot in range(n_slots):
        for p in pairs:
            ya_ref[step % 2, slot * Q_BLOCK:(slot + 1) * Q_BLOCK, sls[p]] = acc_ref[slot, p].astype(BF16)
    kvprev_ref[...] = qkv_ref[:, SB_W:]


def _tail(qkv, x, gm, wg, bg, yb, yc, wa, wb, wc, wo, gf, w1, w3, w2):
    s, w = qkv.shape[0], qkv.shape[1] // 3
    d = x.shape[1]
    n_pairs = w // LANES
    n_slots = SB_QBLOCKS_PER_STEP
    rows = n_slots * Q_BLOCK
    n_tiles = s // rows
    this = lambda i: (jnp.minimum(i, n_tiles - 1), 0)
    last = lambda i: (jnp.maximum(i - 1, 0), 0)
    hbm = pl.BlockSpec(memory_space=pl.ANY)
    return pl.pallas_call(
        _tail_kernel,
        out_shape=jax.ShapeDtypeStruct((s, d), F32),
        grid=(n_tiles + 1,),
        in_specs=[pl.BlockSpec((rows, 3 * w), this), hbm,
                  pl.BlockSpec((rows, d), last), _resident((1, d)), _resident(wg.shape), _resident(bg.shape),
                  pl.BlockSpec((rows, DSA_OUT_W), last), pl.BlockSpec((rows, MEM_W), last),
                  _resident(wa.shape), _resident(wb.shape), _resident(wc.shape), _resident(wo.shape),
                  _resident((1, d)), _resident(w1.shape), _resident(w3.shape), _resident(w2.shape)],
        out_specs=pl.BlockSpec((rows, d), last),
        scratch_shapes=[pltpu.VMEM((n_slots, n_pairs, 2 * Q_BLOCK, LANES), BF16),
                        pltpu.VMEM((n_slots, n_pairs, 2 * Q_BLOCK, LANES), F32),
                        pltpu.VMEM((n_slots, n_pairs, Q_BLOCK, LANES), F32),
                        pltpu.VMEM((2, rows, w), BF16),
                        pltpu.VMEM((rows, 2 * w), BF16),
                        pltpu.VMEM((Q_BLOCK, w), BF16), pltpu.VMEM((Q_BLOCK, w), BF16),
                        pltpu.SemaphoreType.DMA((2,))],
        compiler_params=_params(1),
        name="tail",
    )(qkv, qkv, x, gm, wg, bg, yb, yc, wa, wb, wc, wo, gf, w1, w3, w2)


def _layer(x, mem, p):
    vec = lambda v: v.reshape(1, -1)
    n_dsa = DSA_W // HEAD_DIM
    gains = jnp.concatenate([jnp.tile(p["qn_dsa"], n_dsa), jnp.tile(p["kn_dsa"], n_dsa),
                             jnp.tile(p["qn_mem"], MEM_HEADS), jnp.tile(p["kn_mem"], MEM_HEADS)]).reshape(1, -1)
    half = HEAD_DIM // 2
    inv_freq = jnp.power(ROPE_THETA, -jnp.arange(half, dtype=F32) / half).reshape(half, 1)

    later = ("w_in", "w_gate", "w_branch_sb", "w_branch_dsa", "w_branch_mem", "w_out",
             "ffn2_w1", "ffn2_w3", "ffn2_w2")
    x, cos, sin, casted = _ffn(x, vec(p["ffn1_norm"]), p["ffn1_w1"], p["ffn1_w3"], p["ffn1_w2"], inv_freq,
                               cast=[p[name] for name in later])
    w = dict(zip(later, casted))
    km, vm = _memkv(mem, vec(p["mem_norm"]), p["w_mem_kv"], gains)
    qkv, qb, kb, vb, yc = _proj(x, vec(p["mix_norm"]), w["w_in"], cos, sin, gains, km, vm)
    yb = _dilated(qb, kb, vb)
    return _tail(qkv, x, vec(p["mix_norm"]), w["w_gate"], vec(p["b_gate"]), yb, yc,
                 w["w_branch_sb"], w["w_branch_dsa"], w["w_branch_mem"], w["w_out"],
                 vec(p["ffn2_norm"]), w["ffn2_w1"], w["ffn2_w3"], w["ffn2_w2"])


_PARAM_NAMES = ("ffn1_norm", "ffn1_w1", "ffn1_w3", "ffn1_w2", "mix_norm", "mem_norm", "w_in", "w_mem_kv",
                "qn_dsa", "kn_dsa", "qn_mem", "kn_mem", "w_branch_sb", "w_branch_dsa", "w_branch_mem",
                "w_gate", "b_gate", "w_out", "ffn2_norm", "ffn2_w1", "ffn2_w3", "ffn2_w2")


def kernel(x, mem, ffn1_norm, ffn1_w1, ffn1_w3, ffn1_w2, mix_norm, mem_norm, w_in, w_mem_kv, qn_dsa, kn_dsa, qn_mem, kn_mem, w_branch_sb, w_branch_dsa, w_branch_mem, w_gate, b_gate, w_out, ffn2_norm, ffn2_w1, ffn2_w3, ffn2_w2):
    stacked = dict(zip(_PARAM_NAMES, (ffn1_norm, ffn1_w1, ffn1_w3, ffn1_w2, mix_norm, mem_norm, w_in,
                                      w_mem_kv, qn_dsa, kn_dsa, qn_mem, kn_mem, w_branch_sb, w_branch_dsa,
                                      w_branch_mem, w_gate, b_gate, w_out, ffn2_norm, ffn2_w1, ffn2_w3,
                                      ffn2_w2)))
    depth = ffn1_norm.shape[0]
    outs = []
    for b in range(x.shape[0]):
        xb = x[b]
        for l in range(depth):
            xb = _layer(xb, mem[b], {k: v[l] for k, v in stacked.items()})
        outs.append(xb)
    return jnp.stack(outs)
```
